```python
import math
import jax
import jax.numpy as jnp
from jax import lax
import numpy as np

D_MODEL = 1024
BATCH = 4
SEQ = 4096
DEPTH = 2

RMS_EPS = 1e-6
GN_EPS = 1e-5
ATTN_BLOCK = 128

RET_HEADS = 4
RET_DK = 64
RET_DV = 64
RET_CHUNK = 128
RET_THETA = 10000.0

MLA_HEADS = 4
MLA_Q_RANK = 256
MLA_KV_RANK = 128
MLA_NOPE = 64
MLA_ROPE = 32
MLA_V = 64

SSM_HEADS = 4
SSM_HEAD_DIM = 64
SSM_GROUPS = 2
SSM_STATE = 128
SSM_CONV = 5
SSM_CHUNK = 128
SSM_INNER = SSM_HEADS * SSM_HEAD_DIM
SSM_CONV_DIM = SSM_INNER + 2 * SSM_GROUPS * SSM_STATE

DIL_HEADS = 4
DIL_HEAD_DIM = 64
DIL_PATTERNS = ((128, 1), (512, 4), (2048, 16))

ROPE_THETA = 500000.0
ROPE_DIM = DIL_HEAD_DIM // 4

N_EXPERTS = 16
EXPERT_FF = 2048
EC_CAPACITY_FACTOR = 2

PROJ_SIZES = (
    RET_HEADS * RET_DK, RET_HEADS * RET_DK, RET_HEADS * RET_DV, RET_HEADS * RET_DV,
    MLA_Q_RANK, MLA_KV_RANK, MLA_ROPE,
    SSM_INNER, SSM_CONV_DIM, 2 * SSM_HEADS,
    DIL_HEADS * DIL_HEAD_DIM, DIL_HEADS * DIL_HEAD_DIM, DIL_HEADS * DIL_HEAD_DIM,
)
PROJ_WIDTH = sum(PROJ_SIZES)
MIX_WIDTH = RET_HEADS * RET_DV + MLA_HEADS * MLA_V + SSM_INNER + DIL_HEADS * DIL_HEAD_DIM

kernel_name = 'hybrid_parallel_mixer_ec_moe_encoder'


def _split_points():
    pts, acc = [], 0
    for size in PROJ_SIZES[:-1]:
        acc += size
        pts.append(acc)
    return pts


def rms_norm(x, w, eps=RMS_EPS):
    xf = x.astype(jnp.float32)
    y = xf * lax.rsqrt(jnp.mean(xf * xf, axis=-1, keepdims=True) + eps)
    return (y * w.astype(jnp.float32)).astype(x.dtype)


def rotary(x, rot_dim, theta):
    s = x.shape[-2]
    inv = 1.0 / (theta ** (jnp.arange(0, rot_dim, 2, dtype=jnp.float32) / rot_dim))
    ang = jnp.arange(s, dtype=jnp.float32)[:, None] * inv[None, :]
    cos, sin = jnp.cos(ang), jnp.sin(ang)
    xr = x[..., :rot_dim].astype(jnp.float32)
    x1, x2 = xr[..., :rot_dim // 2], xr[..., rot_dim // 2:]
    rot = jnp.concatenate([x1 * cos - x2 * sin, x2 * cos + x1 * sin], axis=-1).astype(x.dtype)
    return jnp.concatenate([rot, x[..., rot_dim:]], axis=-1)


def split_heads(t, n_heads):
    b, s, w = t.shape
    return t.reshape(b, s, n_heads, w // n_heads).transpose(0, 2, 1, 3)


def merge_heads(t):
    b, h, s, d = t.shape
    return t.transpose(0, 2, 1, 3).reshape(b, s, h * d)


def retention_log_decays(offset):
    exps = -5.0 - offset - jnp.arange(RET_HEADS, dtype=jnp.float32)
    return jnp.log1p(-jnp.exp2(exps))


def retention_direction(q, k, v, log_gamma, strict):
    b, h, s, dk = q.shape
    dv = v.shape[-1]
    nc = s // RET_CHUNK
    qc = q.reshape(b, h, nc, RET_CHUNK, dk)
    kc = k.reshape(b, h, nc, RET_CHUNK, dk)
    vc = v.reshape(b, h, nc, RET_CHUNK, dv)
    pos = jnp.arange(RET_CHUNK, dtype=jnp.float32)
    diff = pos[:, None] - pos[None, :]
    mask = diff > 0 if strict else diff >= 0
    decay = jnp.where(mask, jnp.exp(log_gamma[:, None, None] * jnp.maximum(diff, 0.0)), 0.0)
    scores = jnp.einsum('bhcid,bhcjd->bhcij', qc, kc) * decay[None, :, None]
    inner = jnp.einsum('bhcij,bhcje->bhcie', scores, vc)
    zeta = jnp.exp(log_gamma[:, None] * (RET_CHUNK - 1.0 - pos))
    xi = jnp.exp(log_gamma[:, None] * (pos + 1.0))
    chunk_decay = jnp.exp(log_gamma * RET_CHUNK)[None, :, None, None]
    updates = jnp.einsum('bhcjd,hj,bhcje->cbhde', kc, zeta, vc)

    def step(state, upd):
        return state * chunk_decay + upd, state

    _, prev = lax.scan(step, jnp.zeros((b, h, dk, dv), jnp.float32), updates)
    cross = jnp.einsum('bhcid,cbhde,hi->bhcie', qc, prev, xi)
    return (inner + cross).reshape(b, h, s, dv)


def retention_mixer(q, k, v, g):
    dtype = g.dtype
    q = rotary(split_heads(q, RET_HEADS).astype(jnp.float32), RET_DK, RET_THETA) * (RET_DK ** -0.5)
    k = rotary(split_heads(k, RET_HEADS).astype(jnp.float32), RET_DK, RET_THETA)
    v = split_heads(v, RET_HEADS).astype(jnp.float32)
    flip = lambda t: jnp.flip(t, axis=2)
    o = (retention_direction(q, k, v, retention_log_decays(0.0), False)
         + flip(retention_direction(flip(q), flip(k), flip(v), retention_log_decays(0.5), True)))
    mu = jnp.mean(o, axis=-1, keepdims=True)
    var = jnp.mean(jnp.square(o - mu), axis=-1, keepdims=True)
    o = (o - mu) * lax.rsqrt(var + GN_EPS)
    return (jax.nn.silu(g.astype(jnp.float32)) * merge_heads(o)).astype(dtype)


def blocked_dense_attention(q, k, v, scale):
    b, h, s, d = q.shape
    dv = v.shape[-1]
    nb = s // ATTN_BLOCK
    qb = jnp.moveaxis(q.reshape(b, h, nb, ATTN_BLOCK, d), 2, 0)

    def block(qi):
        sc = jnp.einsum('bhqd,bhkd->bhqk', qi, k).astype(jnp.float32) * scale
        p = jax.nn.softmax(sc, axis=-1)
        return jnp.einsum('bhqk,bhkd->bhqd', p.astype(v.dtype), v)

    o = lax.map(block, qb)
    return jnp.moveaxis(o, 0, 2).reshape(b, h, s, dv)


def mla_mixer(c_q, c_kv, k_rope, q_norm_w, kv_norm_w, w_uq, w_ukv):
    b, s, _ = c_q.shape
    q = split_heads(rms_norm(c_q, q_norm_w) @ w_uq, MLA_HEADS)
    kv = split_heads(rms_norm(c_kv, kv_norm_w) @ w_ukv, MLA_HEADS)
    k_nope, v = kv[..., :MLA_NOPE], kv[..., MLA_NOPE:]
    q = jnp.concatenate([q[..., :MLA_NOPE], rotary(q[..., MLA_NOPE:], MLA_ROPE, ROPE_THETA)], axis=-1)
    kr = rotary(k_rope[:, None], MLA_ROPE, ROPE_THETA)
    k = jnp.concatenate([k_nope, jnp.broadcast_to(kr, (b, MLA_HEADS, s, MLA_ROPE))], axis=-1)
    o = blocked_dense_attention(q, k, v, (MLA_NOPE + MLA_ROPE) ** -0.5)
    return merge_heads(o)


def ssd_direction(x, dt, a, bm, cm, strict):
    b, s, h, p = x.shape
    n = bm.shape[-1]
    nc = s // SSM_CHUNK
    rep = h // SSM_GROUPS
    bh = jnp.repeat(bm, rep, axis=2).reshape(b, nc, SSM_CHUNK, h, n)
    ch = jnp.repeat(cm, rep, axis=2).reshape(b, nc, SSM_CHUNK, h, n)
    xdt = (x * dt[..., None]).reshape(b, nc, SSM_CHUNK, h, p)
    cs = jnp.cumsum((dt * a).reshape(b, nc, SSM_CHUNK, h), axis=2)
    seg = cs[:, :, :, None, :] - cs[:, :, None, :, :]
    pos = jnp.arange(SSM_CHUNK)
    mask = pos[:, None] > pos[None, :] if strict else pos[:, None] >= pos[None, :]
    decay = jnp.where(mask[None, None, :, :, None], jnp.exp(jnp.minimum(seg, 0.0)), 0.0)
    scores = jnp.einsum('bclhn,bcshn->bclsh', ch, bh) * decay
    y_diag = jnp.einsum('bclsh,bcshp->bclhp', scores, xdt)
    to_end = jnp.exp(cs[:, :, -1:, :] - cs)
    states = jnp.einsum('bclhn,bclh,bclhp->cbhpn', bh, to_end, xdt)
    chunk_decay = jnp.moveaxis(jnp.exp(cs[:, :, -1, :]), 1, 0)

    def step(state, inp):
        upd, dec = inp
        return state * dec[:, :, None, None] + upd, state

    _, prev = lax.scan(step, jnp.zeros((b, h, p, n), jnp.float32), (states, chunk_decay))
    y_off = jnp.einsum('bclhn,cbhpn,bclh->bclhp', ch, prev, jnp.exp(cs))
    return (y_diag + y_off).reshape(b, s, h, p)


def mamba2_mixer(z, xbc, dt, conv_w, conv_b, a_log, dt_bias, d_skip, norm_w):
    b, s, _ = z.shape
    pad = SSM_CONV // 2
    xbc = lax.conv_general_dilated(xbc, conv_w[:, None, :].astype(xbc.dtype), (1,), [(pad, pad)],
                                   dimension_numbers=('NWC', 'WIO', 'NWC'),
                                   feature_group_count=SSM_CONV_DIM)
    xbc = jax.nn.silu((xbc + conv_b.astype(xbc.dtype)).astype(jnp.float32))
    xs = xbc[..., :SSM_INNER].reshape(b, s, SSM_HEADS, SSM_HEAD_DIM)
    bm = xbc[..., SSM_INNER:SSM_INNER + SSM_GROUPS * SSM_STATE].reshape(b, s, SSM_GROUPS, SSM_STATE)
    cm = xbc[..., SSM_INNER + SSM_GROUPS * SSM_STATE:].reshape(b, s, SSM_GROUPS, SSM_STATE)
    dt = jax.nn.softplus(dt.astype(jnp.float32).reshape(b, s, 2, SSM_HEADS) + dt_bias.astype(jnp.float32))
    a = -jnp.exp(a_log.astype(jnp.float32))
    flip = lambda t: jnp.flip(t, axis=1)
    y_fwd = ssd_direction(xs, dt[:, :, 0], a[0], bm, cm, False)
    y_bwd = flip(ssd_direction(flip(xs), flip(dt[:, :, 1]), a[1], flip(bm), flip(cm), True))
    y = y_fwd + y_bwd + d_skip.astype(jnp.float32)[:, None] * xs
    y = y.reshape(b, s, SSM_INNER) * jax.nn.silu(z.astype(jnp.float32))
    return rms_norm(y, norm_w).astype(z.dtype)


def dilated_attention(q, k, v, scale):
    b, h, s, d = q.shape
    nb = s // ATTN_BLOCK
    offsets = [dil * jnp.arange(-(win // (2 * dil)), win // (2 * dil) + 1) for win, dil in DIL_PATTERNS]

    def block(i):
        start = i * ATTN_BLOCK
        qi = lax.dynamic_slice_in_dim(q, start, ATTN_BLOCK, axis=2).astype(jnp.float32)
        qpos = start + jnp.arange(ATTN_BLOCK)
        maxes, denoms, outs = [], [], []
        for off in offsets:
            kpos = qpos[:, None] + off[None, :]
            valid = (kpos >= 0) & (kpos < s)
            kpos = jnp.clip(kpos, 0, s - 1)
            kg = jnp.take(k, kpos, axis=2).astype(jnp.float32)
            vg = jnp.take(v, kpos, axis=2).astype(jnp.float32)
            sc = jnp.einsum('bhqd,bhqkd->bhqk', qi, kg) * scale
            sc = jnp.where(valid, sc, -jnp.inf)
            m = jnp.max(sc, axis=-1, keepdims=True)
            p = jnp.exp(sc - m)
            l = jnp.sum(p, axis=-1, keepdims=True)
            outs.append(jnp.einsum('bhqk,bhqkd->bhqd', p, vg) / l)
            maxes.append(m)
            denoms.append(l)
        m_st = jnp.stack(maxes)
        w = jnp.stack(denoms) * jnp.exp(m_st - jnp.max(m_st, axis=0, keepdims=True))
        return (jnp.sum(w * jnp.stack(outs), axis=0) / jnp.sum(w, axis=0)).astype(q.dtype)

    o = lax.map(block, jnp.arange(nb))
    return jnp.moveaxis(o, 0, 2).reshape(b, h, s, d)


def dilated_mixer(q, k, v):
    q = rotary(split_heads(q, DIL_HEADS), ROPE_DIM, ROPE_THETA)
    k = rotary(split_heads(k, DIL_HEADS), ROPE_DIM, ROPE_THETA)
    v = split_heads(v, DIL_HEADS)
    return merge_heads(dilated_attention(q, k, v, DIL_HEAD_DIM ** -0.5))


def expert_choice_ffn(h, router_w, w_gate, w_up, w_down):
    b, s, d = h.shape
    capacity = EC_CAPACITY_FACTOR * s // N_EXPERTS
    logits = jnp.einsum('bsd,de->bse', h, router_w).astype(jnp.float32)
    affinity = jax.nn.softmax(logits, axis=-1)
    gate, token_idx = lax.top_k(jnp.swapaxes(affinity, 1, 2), capacity)
    xin = jax.vmap(lambda hb, ib: hb[ib])(h, token_idx)
    hid = (jax.nn.silu(jnp.einsum('becd,edf->becf', xin, w_gate))
           * jnp.einsum('becd,edf->becf', xin, w_up))
    out = jnp.einsum('becf,efd->becd', hid, w_down) * gate[..., None].astype(h.dtype)

    def combine(ob, ib):
        return jnp.zeros((s, d), ob.dtype).at[ib.reshape(-1)].add(ob.reshape(-1, d))

    return jax.vmap(combine)(out, token_idx)


def setup_inputs(seed: int = 0) -> dict:
    key = jax.random.key(seed)
    ks = jax.random.split(key, 20)
    f32 = jnp.float32

    def dense(k, shape, fan_in):
        return jax.random.normal(k, shape, f32) * (fan_in ** -0.5)

    def gain(k, shape):
        return 1.0 + 0.02 * jax.random.normal(k, shape, f32)

    dt_init = jnp.exp(jax.random.uniform(ks[10], (DEPTH, 2, SSM_HEADS), f32,
                                         math.log(1e-3), math.log(1e-1)))
    return {
        'x': jax.random.normal(ks[0], (BATCH, SEQ, D_MODEL), f32),
        'ln1_w': gain(ks[1], (DEPTH, D_MODEL)),
        'w_in': dense(ks[2], (DEPTH, D_MODEL, PROJ_WIDTH), D_MODEL),
        'mla_q_norm_w': gain(ks[3], (DEPTH, MLA_Q_RANK)),
        'mla_kv_norm_w': gain(ks[4], (DEPTH, MLA_KV_RANK)),
        'mla_w_uq': dense(ks[5], (DEPTH, MLA_Q_RANK, MLA_HEADS * (MLA_NOPE + MLA_ROPE)), MLA_Q_RANK),
        'mla_w_ukv': dense(ks[6], (DEPTH, MLA_KV_RANK, MLA_HEADS * (MLA_NOPE + MLA_V)), MLA_KV_RANK),
        'ssm_conv_w': dense(ks[7], (DEPTH, SSM_CONV, SSM_CONV_DIM), SSM_CONV),
        'ssm_conv_b': 0.02 * jax.random.normal(ks[8], (DEPTH, SSM_CONV_DIM), f32),
        'ssm_a_log': jnp.log(jax.random.uniform(ks[9], (DEPTH, 2, SSM_HEADS), f32, 1.0, 16.0)),
        'ssm_dt_bias': dt_init + jnp.log(-jnp.expm1(-dt_init)),
        'ssm_d': gain(ks[11], (DEPTH, SSM_HEADS)),
        'ssm_norm_w': gain(ks[12], (DEPTH, SSM_INNER)),
        'w_out': dense(ks[13], (DEPTH, MIX_WIDTH, D_MODEL), MIX_WIDTH),
        'ln2_w': gain(ks[14], (DEPTH, D_MODEL)),
        'router_w': dense(ks[15], (DEPTH, D_MODEL, N_EXPERTS), D_MODEL),
        'exp_w_gate': dense(ks[16], (DEPTH, N_EXPERTS, D_MODEL, EXPERT_FF), D_MODEL),
        'exp_w_up': dense(ks[17], (DEPTH, N_EXPERTS, D_MODEL, EXPERT_FF), D_MODEL),
        'exp_w_down': dense(ks[18], (DEPTH, N_EXPERTS, EXPERT_FF, D_MODEL), EXPERT_FF),
        'final_norm_w': gain(ks[19], (D_MODEL,)),
    }


def reference(x, ln1_w, w_in, mla_q_norm_w, mla_kv_norm_w, mla_w_uq, mla_w_ukv,
              ssm_conv_w, ssm_conv_b, ssm_a_log, ssm_dt_bias, ssm_d, ssm_norm_w,
              w_out, ln2_w, router_w, exp_w_gate, exp_w_up, exp_w_down, final_norm_w):
    split_pts = _split_points()
    for i in range(DEPTH):
        hn = rms_norm(x, ln1_w[i])
        proj = hn @ w_in[i]
        (r_q, r_k, r_v, r_g, m_cq, m_ckv, m_kr,
         s_z, s_xbc, s_dt, d_q, d_k, d_v) = jnp.split(proj, split_pts, axis=-1)
        y_a = retention_mixer(r_q, r_k, r_v, r_g)
        y_b = mla_mixer(m_cq, m_ckv, m_kr, mla_q_norm_w[i], mla_kv_norm_w[i], mla_w_uq[i], mla_w_ukv[i])
        y_c = mamba2_mixer(s_z, s_xbc, s_dt, ssm_conv_w[i], ssm_conv_b[i], ssm_a_log[i],
                           ssm_dt_bias[i], ssm_d[i], ssm_norm_w[i])
        y_d = dilated_mixer(d_q, d_k, d_v)
        mixed = jnp.concatenate([y_a, y_b.astype(x.dtype), y_c, y_d.astype(x.dtype)], axis=-1)
        x = x + mixed @ w_out[i]
        x = x + expert_choice_ffn(rms_norm(x, ln2_w[i]), router_w[i], exp_w_gate[i],
                                  exp_w_up[i], exp_w_down[i])
    return rms_norm(x, final_norm_w)
```

```python
import functools
import math

import jax
import jax.numpy as jnp
from jax import lax
from jax.experimental import pallas as pl
from jax.experimental.pallas import tpu as pltpu

F32 = jnp.float32
BF16 = jnp.bfloat16

D_MODEL = 1024
RMS_EPS = 1e-6
GN_EPS = 1e-5
CHUNK = 128
HEAD_DIM = 64
N_HEADS = 4

RET_THETA = 10000.0
ROPE_THETA = 500000.0
ROPE_DIM = 16

MLA_Q_RANK = 256
MLA_KV_RANK = 128
MLA_NOPE = 64
MLA_ROPE = 32

SSM_GROUPS = 2
SSM_STATE = 128
SSM_CONV = 5
SSM_INNER = 256

DIL_PATTERNS = ((128, 1), (512, 4), (2048, 16))

N_EXPERTS = 16
EXPERT_FF = 2048
EC_CAPACITY_FACTOR = 2

BLK = 256
COL_RQ, COL_RK, COL_RV, COL_RG = 0, 1, 2, 3
COL_MCQ, COL_MKV = 4, 5
COL_XBC = 6
COL_Z = 9
COL_DT = 10
COL_DQ, COL_DK, COL_DV = 11, 12, 13
PROJ_PAD = 14 * BLK

VMEM_LIMIT = 56 * 1024 * 1024


def _cparams(sem):
    return pltpu.CompilerParams(dimension_semantics=sem, vmem_limit_bytes=VMEM_LIMIT)


def _split3(x):
    hi = x.astype(BF16)
    r = x - hi.astype(F32)
    mid = r.astype(BF16)
    lo = (r - mid.astype(F32)).astype(BF16)
    return hi, mid, lo


def _dot(a, b):
    return jnp.dot(a, b, preferred_element_type=F32)


def _dot_nt(a, b):
    return lax.dot_general(a, b, (((1,), (1,)), ((), ())), preferred_element_type=F32)


def _dot_x_exact(x, m):
    hi, mid, lo = _split3(x)
    return _dot(hi, m) + _dot(mid, m) + _dot(lo, m)


def _dot_exact_x(m, x):
    hi, mid, lo = _split3(x)
    return _dot(m, hi) + _dot(m, mid) + _dot(m, lo)


def _rope(x, c, s1, s2, half):
    w = x.shape[-1]
    return x * c + pltpu.roll(x, w - half, 1) * s1 + pltpu.roll(x, half, 1) * s2


def _silu(x):
    return x * (1.0 / (1.0 + jnp.exp(-x)))


def _softplus(x):
    return jnp.maximum(x, 0.0) + jnp.log(1.0 + jnp.exp(-jnp.abs(x)))


def _lane_head(shape, width):
    return lax.broadcasted_iota(jnp.int32, shape, 1) // width


def _rms(x, w):
    return x * lax.rsqrt(jnp.mean(x * x, axis=-1, keepdims=True) + RMS_EPS) * w


def _rope_tables(s, rot_dim, theta, head_dim, offset, n_heads):
    half = rot_dim // 2
    inv = 1.0 / (theta ** (jnp.arange(0, rot_dim, 2, dtype=F32) / rot_dim))
    ang = jnp.arange(s, dtype=F32)[:, None] * inv[None, :]
    cos, sin = jnp.cos(ang), jnp.sin(ang)
    pre0 = jnp.zeros((s, offset), F32)
    pre1 = jnp.ones((s, offset), F32)
    post0 = jnp.zeros((s, head_dim - offset - rot_dim), F32)
    post1 = jnp.ones((s, head_dim - offset - rot_dim), F32)
    zh = jnp.zeros((s, half), F32)
    c = jnp.concatenate([pre1, cos, cos, post1], axis=-1)
    s1 = jnp.concatenate([pre0, -sin, zh, post0], axis=-1)
    s2 = jnp.concatenate([pre0, zh, sin, post0], axis=-1)
    tile = lambda t: jnp.tile(t, (1, n_heads))
    return tile(c), tile(s1), tile(s2)


def _tri(n, fn):
    i = jnp.arange(n)
    return fn(i[:, None], i[None, :]).astype(BF16)


def _inproj_kernel(*refs, n_res, col_chunk):
    lnw_ref, w_ref, o_ref = refs[n_res:]
    x = refs[0][...]
    for r in refs[1:n_res]:
        x = x + r[...]
    yb = _rms(x, lnw_ref[...]).astype(BF16)
    for j in range(PROJ_PAD // col_chunk):
        sl = slice(j * col_chunk, (j + 1) * col_chunk)
        o_ref[:, sl] = _dot(yb, w_ref[:, sl])


def _inproj(res, lnw, w_pad):
    t = res[0].shape[0]
    tm = 512
    return pl.pallas_call(
        functools.partial(_inproj_kernel, n_res=len(res), col_chunk=512),
        grid=(t // tm,),
        in_specs=[pl.BlockSpec((tm, D_MODEL), lambda i: (i, 0))] * len(res) + [
            pl.BlockSpec((1, D_MODEL), lambda i: (0, 0)),
            pl.BlockSpec((D_MODEL, PROJ_PAD), lambda i: (0, 0)),
        ],
        out_specs=pl.BlockSpec((tm, PROJ_PAD), lambda i: (i, 0)),
        out_shape=jax.ShapeDtypeStruct((t, PROJ_PAD), F32),
        compiler_params=_cparams(("parallel",)),
    )(*res, lnw, w_pad)


def _pad_cols(w):
    sizes = (256, 256, 256, 256, 256, 128, 32, 256, 768, 8, 256, 256, 256)
    pts, acc = [], 0
    for sz in sizes:
        pts.append((acc, acc + sz))
        acc += sz
    seg = lambda i: w[:, pts[i][0]:pts[i][1]]
    z = lambda n: jnp.zeros((w.shape[0], n), w.dtype)
    cols = [seg(0), seg(1), seg(2), seg(3), seg(4),
            seg(5), seg(6), z(BLK - 128 - 32),
            seg(8), seg(7), seg(9), z(BLK - 8),
            seg(10), seg(11), seg(12)]
    return jnp.concatenate(cols, axis=1)


def _ret_tables():
    pos = jnp.arange(CHUNK, dtype=F32)
    hh = jnp.arange(N_HEADS, dtype=F32)
    lg_f = jnp.log1p(-jnp.exp2(-5.0 - hh))
    lg_b = jnp.log1p(-jnp.exp2(-5.5 - hh))
    diff = pos[:, None] - pos[None, :]
    d_f = jnp.where(diff >= 0, jnp.exp(lg_f[:, None, None] * jnp.maximum(diff, 0.0)), 0.0)
    d_b = jnp.where(diff < 0, jnp.exp(lg_b[:, None, None] * jnp.maximum(-diff, 0.0)), 0.0)
    dmat = (d_f + d_b).reshape(2, 2, CHUNK, CHUNK)

    def lanes(per_head):
        t = jnp.repeat(per_head[:, :, None], HEAD_DIM, axis=2)
        return t.reshape(2, 2, CHUNK, HEAD_DIM).transpose(0, 2, 1, 3).reshape(2, CHUNK, 2 * HEAD_DIM)

    xi_f = lanes(jnp.exp(lg_f[:, None] * (pos + 1.0)))
    zeta_f = lanes(jnp.exp(lg_f[:, None] * (CHUNK - 1.0 - pos)))
    xi_b = lanes(jnp.exp(lg_b[:, None] * (CHUNK - pos)))
    zeta_b = lanes(jnp.exp(lg_b[:, None] * pos))
    blk = (jnp.arange(128)[:, None] // HEAD_DIM) == (jnp.arange(128)[None, :] // HEAD_DIM)

    def cdec(lg):
        g = jnp.exp(lg * CHUNK).reshape(2, 2)
        rows = jnp.repeat(g, HEAD_DIM, axis=1)
        return jnp.where(blk[None], rows[:, :, None], 0.0)

    avg = jnp.where(blk, 1.0 / HEAD_DIM, 0.0).astype(BF16)
    return dmat, xi_f, zeta_f, xi_b, zeta_b, cdec(lg_f), cdec(lg_b), avg


def _ret_kernel(q_ref, k_ref, v_ref, g_ref, c_ref, s1_ref, s2_ref, dmat_ref, xif_ref, zf_ref,
                xib_ref, zb_ref, cdf_ref, cdb_ref, avg_ref, o_ref, sf_ref, sb_ref, pb_ref, *, nsteps, g):
    t = pl.program_id(1)

    @pl.when(t == 0)
    def _():
        sf_ref[...] = jnp.zeros_like(sf_ref)
        sb_ref[...] = jnp.zeros_like(sb_ref)

    def rotated(ref, rows):
        return _rope(ref[rows, :], c_ref[rows, :], s1_ref[rows, :], s2_ref[rows, :], HEAD_DIM // 2)

    def state_update(s_ref, p, k, vb, zeta, cdec):
        kz = (k * zeta).T.astype(BF16)
        blockmask = jnp.where(cdec > 0.0, 1.0, 0.0)
        s_ref[p] = s_ref[p] * cdec + _dot(kz, vb) * blockmask

    @pl.when(t < nsteps)
    def _():
        base = (nsteps - 1 - t) * g
        for ci in reversed(range(g)):
            rows = slice(ci * CHUNK, (ci + 1) * CHUNK)
            k = rotated(k_ref, rows)
            vb = v_ref[rows, :].astype(BF16)
            for p in range(2):
                sl = slice(p * 128, (p + 1) * 128)
                pb_ref[base + ci, p] = sb_ref[p].astype(BF16)
                state_update(sb_ref, p, k[:, sl], vb[:, sl], zb_ref[p], cdb_ref[p])

    @pl.when(t >= nsteps)
    def _():
        base = (t - nsteps) * g
        head = _lane_head((CHUNK, 128), HEAD_DIM)
        avg = avg_ref[...]
        for ci in range(g):
            rows = slice(ci * CHUNK, (ci + 1) * CHUNK)
            k_all = rotated(k_ref, rows)
            q_all = rotated(q_ref, rows) * (HEAD_DIM ** -0.5)
            vb_all = v_ref[rows, :].astype(BF16)
            for p in range(2):
                sl = slice(p * 128, (p + 1) * 128)
                q, k, vb = q_all[:, sl], k_all[:, sl], vb_all[:, sl]
                kb = k.astype(BF16)
                inner = None
                for hh in range(2):
                    qh = jnp.where(head == hh, q, 0.0).astype(BF16)
                    pm = (_dot_nt(qh, kb) * dmat_ref[p, hh]).astype(BF16)
                    oh = _dot(pm, vb)
                    inner = oh if inner is None else jnp.where(head == hh, oh, inner)
                cross_f = _dot((q * xif_ref[p]).astype(BF16), sf_ref[p].astype(BF16))
                cross_b = _dot((q * xib_ref[p]).astype(BF16), pb_ref[base + ci, p])
                o = inner + cross_f + cross_b
                mu = _dot_x_exact(o, avg)
                d = o - mu
                var = _dot_x_exact(d * d, avg)
                o_ref[rows, sl] = _silu(g_ref[rows, sl]) * (d * lax.rsqrt(var + GN_EPS))
                state_update(sf_ref, p, k, vb, zf_ref[p], cdf_ref[p])


def _retention(proj, tables, rope_tabs):
    b, s, _ = proj.shape
    nc = s // CHUNK
    g = math.gcd(nc, 4)
    nsteps = nc // g
    rows = g * CHUNK
    both = lambda t: jnp.where(t < nsteps, nsteps - 1 - t, t - nsteps)
    late = lambda t: jnp.where(t < nsteps, 0, t - nsteps)
    col = lambda base, idx: pl.BlockSpec((None, rows, BLK), lambda bi, t: (bi, idx(t), base))
    tab = pl.BlockSpec((rows, BLK), lambda bi, t: (both(t), 0))
    pair = pl.BlockSpec((2, CHUNK, 128), lambda bi, t: (0, 0, 0))
    return pl.pallas_call(
        functools.partial(_ret_kernel, nsteps=nsteps, g=g),
        grid=(b, 2 * nsteps),
        in_specs=[
            col(COL_RQ, late), col(COL_RK, both), col(COL_RV, both), col(COL_RG, late),
            tab, tab, tab,
            pl.BlockSpec((2, 2, CHUNK, CHUNK), lambda bi, t: (0, 0, 0, 0)),
            pair, pair, pair, pair, pair, pair,
            pl.BlockSpec((128, 128), lambda bi, t: (0, 0)),
        ],
        out_specs=pl.BlockSpec((None, rows, BLK), lambda bi, t: (bi, late(t), 0)),
        out_shape=jax.ShapeDtypeStruct((b, s, BLK), F32),
        scratch_shapes=[pltpu.VMEM((2, 128, 128), F32), pltpu.VMEM((2, 128, 128), F32),
                        pltpu.VMEM((nc, 2, 128, 128), BF16)],
        compiler_params=_cparams(("parallel", "arbitrary")),
    )(proj, proj, proj, proj, *rope_tabs, *tables)


def _mla_prep_kernel(cq_ref, kv_ref, qnw_ref, kvnw_ref, wq_ref, wk_ref, wv_ref, place_ref,
                     qc_ref, qs1_ref, qs2_ref, kc_ref, ks1_ref, ks2_ref, q_ref, k_ref, v_ref):
    cqn = _rms(cq_ref[...], qnw_ref[...])
    q = _dot(cqn.astype(BF16), wq_ref[...])
    q = _rope(q, qc_ref[...], qs1_ref[...], qs2_ref[...], MLA_ROPE // 2)
    q_ref[...] = (q * ((MLA_NOPE + MLA_ROPE) ** -0.5 * math.log2(math.e))).astype(BF16)
    blk = kv_ref[...]
    ckvn = _rms(blk[:, :MLA_KV_RANK], kvnw_ref[...]).astype(BF16)
    kr = _rope(blk[:, MLA_KV_RANK:], kc_ref[...], ks1_ref[...], ks2_ref[...], MLA_ROPE // 2)
    k_ref[...] = (_dot(ckvn, wk_ref[...]) + _dot(kr.astype(BF16), place_ref[...])).astype(BF16)
    v = _dot(ckvn, wv_ref[...])
    ones = (lax.broadcasted_iota(jnp.int32, v.shape, 1) % 128) >= HEAD_DIM
    v_ref[...] = jnp.where(ones, 1.0, v).astype(BF16)


def _mla_weights(w_uq, w_ukv):
    qh = w_uq.reshape(MLA_Q_RANK, N_HEADS, MLA_NOPE + MLA_ROPE)
    wq = jnp.concatenate([qh, jnp.zeros((MLA_Q_RANK, N_HEADS, 32), F32)], axis=-1).reshape(MLA_Q_RANK, 512)
    kvh = w_ukv.reshape(MLA_KV_RANK, N_HEADS, MLA_NOPE + HEAD_DIM)
    zk = jnp.zeros((MLA_KV_RANK, N_HEADS, 64), F32)
    wk = jnp.concatenate([kvh[..., :MLA_NOPE], zk], axis=-1).reshape(MLA_KV_RANK, 512)
    wv = jnp.concatenate([kvh[..., MLA_NOPE:], zk], axis=-1).reshape(MLA_KV_RANK, 512)
    src = jnp.arange(128)[:, None]
    dst = jnp.arange(512)[None, :]
    place = ((dst % 128 == src + MLA_NOPE) & (src < MLA_ROPE)).astype(BF16)
    return wq.astype(BF16), wk.astype(BF16), wv.astype(BF16), place


def _mla_flash_kernel(q_ref, k_ref, v_ref, o_ref, m_ref, acc_ref, *, strip):
    ki = pl.program_id(2)

    @pl.when(ki == 0)
    def _():
        m_ref[...] = jnp.full_like(m_ref, -jnp.inf)
        acc_ref[...] = jnp.zeros_like(acc_ref)

    for h in range(N_HEADS):
        sl = slice(h * 128, (h + 1) * 128)
        k = k_ref[:, sl]
        v = v_ref[:, sl]
        for r in range(q_ref.shape[0] // strip):
            rows = slice(r * strip, (r + 1) * strip)
            s = _dot_nt(q_ref[rows, sl], k)
            m_prev = m_ref[h, rows]
            m_new = jnp.maximum(m_prev, jnp.max(s, axis=-1, keepdims=True))
            p = jnp.exp2(s - m_new[:, :1])
            acc_ref[h, rows] = jnp.exp2(m_prev - m_new) * acc_ref[h, rows] + _dot(p.astype(BF16), v)
            m_ref[h, rows] = m_new

    @pl.when(ki == pl.num_programs(2) - 1)
    def _():
        outs = []
        for h in range(N_HEADS):
            a = acc_ref[h]
            outs.append(a[:, :HEAD_DIM] / a[:, HEAD_DIM:])
        o_ref[...] = jnp.concatenate(outs, axis=-1)


def _mla(proj, q_norm_w, kv_norm_w, w_uq, w_ukv, q_tabs, k_tabs):
    b, s, _ = proj.shape
    wq, wk, wv, place = _mla_weights(w_uq, w_ukv)
    ts = 512
    row = lambda n: pl.BlockSpec((ts, n), lambda bi, i: (i, 0))
    full = lambda r, n: pl.BlockSpec((r, n), lambda bi, i: (0, 0))
    out = pl.BlockSpec((None, ts, 512), lambda bi, i: (bi, i, 0))
    q, k, v = pl.pallas_call(
        _mla_prep_kernel,
        grid=(b, s // ts),
        in_specs=[
            pl.BlockSpec((None, ts, BLK), lambda bi, i: (bi, i, COL_MCQ)),
            pl.BlockSpec((None, ts, BLK), lambda bi, i: (bi, i, COL_MKV)),
            full(1, MLA_Q_RANK), full(1, MLA_KV_RANK),
            full(MLA_Q_RANK, 512), full(MLA_KV_RANK, 512), full(MLA_KV_RANK, 512), full(128, 512),
            row(512), row(512), row(512), row(128), row(128), row(128),
        ],
        out_specs=[out, out, out],
        out_shape=[jax.ShapeDtypeStruct((b, s, 512), BF16)] * 3,
        compiler_params=_cparams(("parallel", "parallel")),
    )(proj, proj, q_norm_w[None], kv_norm_w[None], wq, wk, wv, place, *q_tabs, *k_tabs)

    tq = tk = 512
    return pl.pallas_call(
        functools.partial(_mla_flash_kernel, strip=tq),
        grid=(b, s // tq, s // tk),
        in_specs=[
            pl.BlockSpec((None, tq, 512), lambda bi, qi, ki: (bi, qi, 0)),
            pl.BlockSpec((None, tk, 512), lambda bi, qi, ki: (bi, ki, 0)),
            pl.BlockSpec((None, tk, 512), lambda bi, qi, ki: (bi, ki, 0)),
        ],
        out_specs=pl.BlockSpec((None, tq, N_HEADS * HEAD_DIM), lambda bi, qi, ki: (bi, qi, 0)),
        out_shape=jax.ShapeDtypeStruct((b, s, N_HEADS * HEAD_DIM), F32),
        scratch_shapes=[pltpu.VMEM((N_HEADS, tq, 128), F32), pltpu.VMEM((N_HEADS, tq, 128), F32)],
        compiler_params=_cparams(("parallel", "parallel", "arbitrary")),
    )(q, k, v)


def _conv_kernel(prev_ref, cur_ref, next_ref, w_ref, b_ref, o_ref, *, ts):
    i = pl.program_id(1)
    w = w_ref[...]

    def conv(x):
        n = x.shape[0]
        acc = x * w[2:3]
        for s in (-2, -1, 1, 2):
            acc = acc + pltpu.roll(x, (-s) % n, 0) * w[s + 2:s + 3]
        return acc

    act = lambda y: _silu(y + b_ref[...])
    cur = cur_ref[...]
    prev = prev_ref[...] * (i > 0).astype(F32)
    nxt = next_ref[...] * (i < pl.num_programs(1) - 1).astype(F32)
    o_ref[...] = act(conv(cur))
    top = conv(jnp.concatenate([prev, cur[:16]], axis=0))
    o_ref[0:8, :] = act(top[8:16])
    bot = conv(jnp.concatenate([cur[ts - 16:], nxt], axis=0))
    o_ref[ts - 8:ts, :] = act(bot[8:16])


def _ssd_conv(proj, conv_w, conv_b):
    b, s, _ = proj.shape
    ts = 512
    width = 3 * BLK
    cb = COL_XBC * BLK // width
    nb8 = ts // 8
    w8 = jnp.concatenate([conv_w, jnp.zeros((8 - SSM_CONV, width), F32)], axis=0)
    return pl.pallas_call(
        functools.partial(_conv_kernel, ts=ts),
        grid=(b, s // ts),
        in_specs=[
            pl.BlockSpec((None, 8, width), lambda bi, i: (bi, jnp.maximum(i * nb8 - 1, 0), cb)),
            pl.BlockSpec((None, ts, width), lambda bi, i: (bi, i, cb)),
            pl.BlockSpec((None, 8, width), lambda bi, i: (bi, jnp.minimum((i + 1) * nb8, s // 8 - 1), cb)),
            pl.BlockSpec((8, width), lambda bi, i: (0, 0)),
            pl.BlockSpec((1, width), lambda bi, i: (0, 0)),
        ],
        out_specs=pl.BlockSpec((None, ts, width), lambda bi, i: (bi, i, 0)),
        out_shape=jax.ShapeDtypeStruct((b, s, width), F32),
        compiler_params=_cparams(("parallel", "parallel")),
    )(proj, proj, proj, w8, conv_b[None])


def _ssd_tables():
    tril = _tri(CHUNK, lambda i, j: j <= i)
    triu = _tri(CHUNK, lambda i, j: j >= i)
    h = jnp.arange(128)[:, None]
    lane = jnp.arange(256)[None, :]
    e_f = ((lane // HEAD_DIM == h) & (h < N_HEADS)).astype(BF16)
    e_b = ((lane // HEAD_DIM == h - N_HEADS) & (h >= N_HEADS) & (h < 2 * N_HEADS)).astype(BF16)
    return tril, triu, e_f, e_b


def _ssd_kernel(xs_ref, bm_ref, cm_ref, z_ref, dt_ref, dtt_ref, bias_ref, a_ref, biasc_ref, ac_ref,
                dskip_ref, nw_ref, tril_ref, triu_ref, ef_ref, eb_ref, o_ref,
                sf_ref, sb_ref, pb_ref, *, nsteps, g):
    t = pl.program_id(1)
    tril, triu = tril_ref[...], triu_ref[...]

    @pl.when(t == 0)
    def _():
        sf_ref[...] = jnp.zeros_like(sf_ref)
        sb_ref[...] = jnp.zeros_like(sb_ref)

    def step_sizes(rows):
        dt = _softplus(dt_ref[rows, :128] + bias_ref[...])
        return dt, dt * a_ref[...]

    def backward_terms(xs, dt, dta):
        rcs_b = _dot_x_exact(_dot_exact_x(triu, dta), eb_ref[...])
        return rcs_b, rcs_b[0:1, :], xs * _dot_x_exact(dt, eb_ref[...])

    def state_update(s_ref, bmf, weighted_x, total):
        for gi in range(SSM_GROUPS):
            sl = slice(gi * 128, (gi + 1) * 128)
            upd = _dot(bmf[:, sl].T.astype(BF16), weighted_x[:, sl].astype(BF16))
            s_ref[gi] = s_ref[gi] * jnp.exp(total[:, sl]) + upd

    @pl.when(t < nsteps)
    def _():
        base = (nsteps - 1 - t) * g
        for ci in reversed(range(g)):
            rows = slice(ci * CHUNK, (ci + 1) * CHUNK)
            xs = xs_ref[rows, :]
            dt, dta = step_sizes(rows)
            rcs_b, tot_b, xdt_b = backward_terms(xs, dt, dta)
            pb_ref[base + ci] = sb_ref[...].astype(BF16)
            state_update(sb_ref, bm_ref[rows, :], jnp.exp(tot_b - rcs_b) * xdt_b, tot_b)

    @pl.when(t >= nsteps)
    def _():
        base = (t - nsteps) * g
        row = lax.broadcasted_iota(jnp.int32, (CHUNK, CHUNK), 0)
        colm = lax.broadcasted_iota(jnp.int32, (CHUNK, CHUNK), 1)
        head = _lane_head((CHUNK, 128), HEAD_DIM)
        for ci in range(g):
            rows = slice(ci * CHUNK, (ci + 1) * CHUNK)
            xs = xs_ref[rows, :]
            bmf = bm_ref[rows, :]
            bm = bmf.astype(BF16)
            cm = cm_ref[rows, :].astype(BF16)
            dt, dta = step_sizes(rows)
            rcs_b, tot_b, xdt_b = backward_terms(xs, dt, dta)
            cs_f = _dot_x_exact(_dot_exact_x(tril, dta), ef_ref[...])
            tot_f = cs_f[CHUNK - 1:CHUNK, :]
            xdt_f = xs * _dot_x_exact(dt, ef_ref[...])
            dtat = _softplus(dtt_ref[:, rows] + biasc_ref[...]) * ac_ref[...]
            cst = _dot_x_exact(dtat, triu)
            rcst = _dot_x_exact(dtat, tril)
            pb = pb_ref[base + ci]
            ys = []
            for gi in range(SSM_GROUPS):
                sl = slice(gi * 128, (gi + 1) * 128)
                cb = _dot_nt(cm[:, sl], bm[:, sl])
                xcat = jnp.concatenate([xdt_f[:, sl], xdt_b[:, sl]], axis=0).astype(BF16)
                yg = None
                for hh in range(2):
                    h = 2 * gi + hh
                    c0 = h * HEAD_DIM
                    seg_f = cs_f[:, c0:c0 + 1] - cst[h:h + 1, :]
                    seg_b = rcs_b[:, c0:c0 + 1] - rcst[N_HEADS + h:N_HEADS + h + 1, :]
                    dec_f = jnp.where(row >= colm, jnp.exp(jnp.minimum(seg_f, 0.0)), 0.0)
                    dec_b = jnp.where(row < colm, jnp.exp(jnp.minimum(seg_b, 0.0)), 0.0)
                    wcat = jnp.concatenate([cb * dec_f, cb * dec_b], axis=1).astype(BF16)
                    yh = _dot(wcat, xcat)
                    yg = yh if yg is None else jnp.where(head == hh, yh, yg)
                off_f = jnp.exp(cs_f[:, sl]) * _dot(cm[:, sl], sf_ref[gi].astype(BF16))
                off_b = jnp.exp(rcs_b[:, sl]) * _dot(cm[:, sl], pb[gi])
                ys.append(yg + off_f + off_b)
            y = jnp.concatenate(ys, axis=1) + dskip_ref[...] * xs
            y = y * _silu(z_ref[rows, :])
            o_ref[rows, :] = _rms(y, nw_ref[...])
            state_update(sf_ref, bmf, jnp.exp(tot_f - cs_f) * xdt_f, tot_f)


def _ssd(proj, xbc, a_log, dt_bias, d_skip, norm_w, tables):
    b, s, _ = proj.shape
    nc = s // CHUNK
    g = math.gcd(nc, 2)
    nsteps = nc // g
    rows = g * CHUNK
    dt_t = jnp.swapaxes(proj[:, :, COL_DT * BLK:COL_DT * BLK + 8], 1, 2)
    pad_row = lambda v: jnp.concatenate([v.reshape(1, 8), jnp.zeros((1, 120), F32)], axis=1)
    a = -jnp.exp(a_log.astype(F32))
    both = lambda t: jnp.where(t < nsteps, nsteps - 1 - t, t - nsteps)
    late = lambda t: jnp.where(t < nsteps, 0, t - nsteps)
    full = lambda r, n: pl.BlockSpec((r, n), lambda bi, t: (0, 0))
    blk = lambda idx, c: pl.BlockSpec((None, rows, BLK), lambda bi, t: (bi, idx(t), c))
    return pl.pallas_call(
        functools.partial(_ssd_kernel, nsteps=nsteps, g=g),
        grid=(b, 2 * nsteps),
        in_specs=[
            blk(both, 0), blk(both, 1), blk(late, 2), blk(late, COL_Z), blk(both, COL_DT),
            pl.BlockSpec((None, 8, rows), lambda bi, t: (bi, 0, late(t))),
            full(1, 128), full(1, 128), full(8, 1), full(8, 1),
            full(1, SSM_INNER), full(1, SSM_INNER),
            full(CHUNK, CHUNK), full(CHUNK, CHUNK), full(128, 256), full(128, 256),
        ],
        out_specs=blk(late, 0),
        out_shape=jax.ShapeDtypeStruct((b, s, SSM_INNER), F32),
        scratch_shapes=[pltpu.VMEM((SSM_GROUPS, 128, 128), F32), pltpu.VMEM((SSM_GROUPS, 128, 128), F32),
                        pltpu.VMEM((nc, SSM_GROUPS, 128, 128), BF16)],
        compiler_params=_cparams(("parallel", "arbitrary")),
    )(xbc, xbc, xbc, proj, proj, dt_t, pad_row(dt_bias), pad_row(a), dt_bias.reshape(8, 1), a.reshape(8, 1),
      jnp.repeat(d_skip, HEAD_DIM)[None], norm_w[None], *tables)


def _dil_prep_kernel(q_ref, k_ref, c_ref, s1_ref, s2_ref, qo_ref, ko_ref):
    c, s1, s2 = c_ref[...], s1_ref[...], s2_ref[...]
    qo_ref[...] = _rope(q_ref[...], c, s1, s2, ROPE_DIM // 2) * (HEAD_DIM ** -0.5)
    ko_ref[...] = _rope(k_ref[...], c, s1, s2, ROPE_DIM // 2)


def _dil_kernel(q_ref, k_ref, v_ref, o_ref, m_ref, l_ref, *, s):
    head = _lane_head((128, 128), HEAD_DIM)
    kw = 2 * 128
    for pi, (win, d) in enumerate(DIL_PATTERNS):
        half = win // (2 * d)
        seg = s // d
        per_seg = seg // 128

        def body(i, carry, d=d, pi=pi, half=half, seg=seg, per_seg=per_seg):
            r = i // per_seg
            m0 = (i % per_seg) * 128
            ks = jnp.clip(m0 - half, 0, seg - kw)
            if d == 1:
                qrows = pl.ds(pl.multiple_of(m0, 128), 128)
                krows = pl.ds(pl.multiple_of(ks, 64), kw)
            else:
                qrows = pl.ds(r + d * m0, 128, stride=d)
                krows = pl.ds(r + d * ks, kw, stride=d)
            q = q_ref[qrows, :]
            kb = k_ref[krows, :].astype(BF16)
            vb = v_ref[krows, :].astype(BF16)
            qpos = m0 + lax.broadcasted_iota(jnp.int32, (128, kw), 0)
            kpos = ks + lax.broadcasted_iota(jnp.int32, (128, kw), 1)
            valid = jnp.abs(qpos - kpos) <= half
            mb = lb = pv = None
            for hh in range(2):
                sc = _dot_nt(jnp.where(head == hh, q, 0.0).astype(BF16), kb)
                sc = jnp.where(valid, sc, -jnp.inf)
                mh = jnp.max(sc, axis=-1, keepdims=True)
                p = jnp.exp(sc - mh)
                lh = jnp.sum(p, axis=-1, keepdims=True)
                ph = _dot(p.astype(BF16), vb)
                if hh == 0:
                    mb = jnp.broadcast_to(mh, (128, 128))
                    lb = jnp.broadcast_to(lh, (128, 128))
                    pv = ph
                else:
                    mb = jnp.where(head == hh, mh, mb)
                    lb = jnp.where(head == hh, lh, lb)
                    pv = jnp.where(head == hh, ph, pv)
            if pi == 0:
                m_ref[qrows, :] = mb
                l_ref[qrows, :] = lb
                o_ref[qrows, :] = pv
            else:
                mo = m_ref[qrows, :]
                mn = jnp.maximum(mo, mb)
                wa = jnp.exp(mo - mn)
                wb = jnp.exp(mb - mn)
                m_ref[qrows, :] = mn
                l_ref[qrows, :] = wa * l_ref[qrows, :] + wb * lb
                o_ref[qrows, :] = wa * o_ref[qrows, :] + wb * pv
            return carry

        lax.fori_loop(0, s // 128, body, 0)
    o_ref[...] = o_ref[...] / l_ref[...]


def _dilated(proj, tabs):
    b, s, _ = proj.shape
    assert all(s // d >= 256 for _, d in DIL_PATTERNS)
    ts = 512
    rowt = pl.BlockSpec((ts, BLK), lambda bi, i: (i, 0))
    outb = pl.BlockSpec((None, ts, BLK), lambda bi, i: (bi, i, 0))
    qr, kr = pl.pallas_call(
        _dil_prep_kernel,
        grid=(b, s // ts),
        in_specs=[
            pl.BlockSpec((None, ts, BLK), lambda bi, i: (bi, i, COL_DQ)),
            pl.BlockSpec((None, ts, BLK), lambda bi, i: (bi, i, COL_DK)),
            rowt, rowt, rowt,
        ],
        out_specs=[outb, outb],
        out_shape=[jax.ShapeDtypeStruct((b, s, BLK), F32)] * 2,
        compiler_params=_cparams(("parallel", "parallel")),
    )(proj, proj, *tabs)
    seqb = lambda c: pl.BlockSpec((None, s, 128), lambda bi, p: (bi, 0, 2 * c + p))
    return pl.pallas_call(
        functools.partial(_dil_kernel, s=s),
        grid=(b, 2),
        in_specs=[seqb(0), seqb(0), seqb(COL_DV)],
        out_specs=seqb(0),
        out_shape=jax.ShapeDtypeStruct((b, s, BLK), F32),
        scratch_shapes=[pltpu.VMEM((s, 128), F32), pltpu.VMEM((s, 128), F32)],
        compiler_params=_cparams(("parallel", "parallel")),
    )(qr, kr, proj)


def _outproj_kernel(*refs, n_res):
    ya_ref, yb_ref, yc_ref, yd_ref, w_ref, lnw_ref, rw_ref, xo_ref, h_ref, aff_ref = refs[n_res:]
    acc = refs[0][...]
    for r in refs[1:n_res]:
        acc = acc + r[...]
    for i, y_ref in enumerate((ya_ref, yb_ref, yc_ref, yd_ref)):
        acc = acc + _dot(y_ref[...].astype(BF16), w_ref[i * BLK:(i + 1) * BLK, :])
    xo_ref[...] = acc
    h = _rms(acc, lnw_ref[...])
    h_hi = h.astype(BF16)
    h_ref[...] = h_hi
    h_lo = (h - h_hi.astype(F32)).astype(BF16)
    rw = rw_ref[...]
    w_hi = rw.astype(BF16)
    w_lo = (rw - w_hi.astype(F32)).astype(BF16)
    logits = _dot(h_hi, w_hi) + _dot(h_hi, w_lo) + _dot(h_lo, w_hi) + _dot(h_lo, w_lo)
    lane = lax.broadcasted_iota(jnp.int32, logits.shape, 1)
    logits = jnp.where(lane < N_EXPERTS, logits, -jnp.inf)
    e = jnp.exp(logits - jnp.max(logits, axis=-1, keepdims=True))
    aff_ref[...] = e / jnp.sum(e, axis=-1, keepdims=True)


def _outproj(res, ys, w_out, ln2_w, router_w):
    t = res[0].shape[0]
    tm = 256
    rw = jnp.concatenate([router_w, jnp.zeros((D_MODEL, 128 - N_EXPERTS), F32)], axis=1)
    row = lambda n: pl.BlockSpec((tm, n), lambda i: (i, 0))
    full = lambda r, n: pl.BlockSpec((r, n), lambda i: (0, 0))
    return pl.pallas_call(
        functools.partial(_outproj_kernel, n_res=len(res)),
        grid=(t // tm,),
        in_specs=[row(D_MODEL)] * len(res) + [row(BLK)] * 4 + [
            full(D_MODEL, D_MODEL), full(1, D_MODEL), full(D_MODEL, 128)],
        out_specs=[row(D_MODEL), row(D_MODEL), row(128)],
        out_shape=[jax.ShapeDtypeStruct((t, D_MODEL), F32),
                   jax.ShapeDtypeStruct((t, D_MODEL), BF16),
                   jax.ShapeDtypeStruct((t, 128), F32)],
        compiler_params=_cparams(("parallel",)),
    )(*res, *ys, w_out.astype(BF16), ln2_w[None], rw)


ROWS_PER_ITER = 8


def _gather_kernel(idx_ref, h_ref, o_ref, hf_ref, buf_ref, *, cap, n_exp):
    e = pl.program_id(1)
    base = (pl.program_id(0) * n_exp + e) * cap

    @pl.when(e == 0)
    def _():
        hf_ref[...] = h_ref[...].astype(F32)

    def body(jb, carry):
        j0 = jb * ROWS_PER_ITER
        vals = [hf_ref[pl.ds(idx_ref[base + j0 + u], 1), :] for u in range(ROWS_PER_ITER)]
        for u in range(ROWS_PER_ITER):
            buf_ref[pl.ds(j0 + u, 1), :] = vals[u]
        return carry

    lax.fori_loop(0, cap // ROWS_PER_ITER, body, 0)
    o_ref[...] = buf_ref[...].astype(BF16)


def _moe_gather(idx_flat, h, cap):
    b, s, d = h.shape
    return pl.pallas_call(
        functools.partial(_gather_kernel, cap=cap, n_exp=N_EXPERTS),
        grid_spec=pltpu.PrefetchScalarGridSpec(
            num_scalar_prefetch=1,
            grid=(b, N_EXPERTS),
            in_specs=[pl.BlockSpec((None, s, d), lambda bi, e, idx: (bi, 0, 0))],
            out_specs=pl.BlockSpec((None, None, cap, d), lambda bi, e, idx: (e, bi, 0, 0)),
            scratch_shapes=[pltpu.VMEM((s, d), F32), pltpu.VMEM((cap, d), F32)],
        ),
        out_shape=jax.ShapeDtypeStruct((N_EXPERTS, b, cap, d), BF16),
        compiler_params=_cparams(("parallel", "arbitrary")),
    )(idx_flat, h)


def _ffn_kernel(x_ref, wg_ref, wu_ref, wd_ref, o_ref, *, rows, tr):
    f = pl.program_id(1)

    @pl.when(f == 0)
    def _():
        o_ref[...] = jnp.zeros_like(o_ref)

    wg = wg_ref[...].astype(BF16)
    wu = wu_ref[...].astype(BF16)
    wd = wd_ref[...].astype(BF16)

    def body(r, carry):
        sl = pl.ds(pl.multiple_of(r * tr, tr), tr)
        x = x_ref[sl, :]
        hid = (_silu(_dot(x, wg)) * _dot(x, wu)).astype(BF16)
        o_ref[sl, :] += _dot(hid, wd)
        return carry

    lax.fori_loop(0, rows // tr, body, 0)


def _moe_ffn(xin, w_gate, w_up, w_down, layer):
    n_exp, rows, d = xin.shape
    tf = 512
    return pl.pallas_call(
        functools.partial(_ffn_kernel, rows=rows, tr=math.gcd(rows, 256)),
        grid=(n_exp, EXPERT_FF // tf),
        in_specs=[
            pl.BlockSpec((None, rows, d), lambda e, f: (e, 0, 0)),
            pl.BlockSpec((None, None, d, tf), lambda e, f: (layer, e, 0, f)),
            pl.BlockSpec((None, None, d, tf), lambda e, f: (layer, e, 0, f)),
            pl.BlockSpec((None, None, tf, d), lambda e, f: (layer, e, f, 0)),
        ],
        out_specs=pl.BlockSpec((None, rows, d), lambda e, f: (e, 0, 0)),
        out_shape=jax.ShapeDtypeStruct((n_exp, rows, d), F32),
        compiler_params=_cparams(("parallel", "arbitrary")),
    )(xin, w_gate, w_up, w_down)


def _combine_kernel(idx_ref, gate_ref, y_ref, o_ref, *, cap, n_exp):
    e = pl.program_id(1)
    base = (pl.program_id(0) * n_exp + e) * cap

    @pl.when(e == 0)
    def _():
        o_ref[...] = jnp.zeros_like(o_ref)

    def body(jb, carry):
        j0 = jb * ROWS_PER_ITER
        rows = [idx_ref[base + j0 + u] for u in range(ROWS_PER_ITER)]
        vals = [o_ref[pl.ds(rows[u], 1), :] + gate_ref[base + j0 + u] * y_ref[pl.ds(j0 + u, 1), :]
                for u in range(ROWS_PER_ITER)]
        for u in range(ROWS_PER_ITER):
            o_ref[pl.ds(rows[u], 1), :] = vals[u]
        return carry

    lax.fori_loop(0, cap // ROWS_PER_ITER, body, 0)


def _moe_combine(idx_flat, gate_flat, y, b, s, cap):
    d = y.shape[-1]
    return pl.pallas_call(
        functools.partial(_combine_kernel, cap=cap, n_exp=N_EXPERTS),
        grid_spec=pltpu.PrefetchScalarGridSpec(
            num_scalar_prefetch=2,
            grid=(b, N_EXPERTS),
            in_specs=[pl.BlockSpec((None, None, cap, d), lambda bi, e, idx, gate: (e, bi, 0, 0))],
            out_specs=pl.BlockSpec((None, s, d), lambda bi, e, idx, gate: (bi, 0, 0)),
        ),
        out_shape=jax.ShapeDtypeStruct((b, s, d), F32),
        compiler_params=_cparams(("parallel", "arbitrary")),
    )(idx_flat, gate_flat, y)


def _final_norm_kernel(xa_ref, xb_ref, w_ref, o_ref):
    o_ref[...] = _rms(xa_ref[...] + xb_ref[...], w_ref[...])


def _final_norm(xa, xb, w):
    t = xa.shape[0]
    tm = 1024
    row = pl.BlockSpec((tm, D_MODEL), lambda i: (i, 0))
    return pl.pallas_call(
        _final_norm_kernel,
        grid=(t // tm,),
        in_specs=[row, row, pl.BlockSpec((1, D_MODEL), lambda i: (0, 0))],
        out_specs=row,
        out_shape=jax.ShapeDtypeStruct((t, D_MODEL), F32),
        compiler_params=_cparams(("parallel",)),
    )(xa, xb, w[None])


def _moe(h, aff, w_gate, w_up, w_down, layer, b, s):
    d = h.shape[-1]
    cap = EC_CAPACITY_FACTOR * s // N_EXPERTS
    affinity = jnp.swapaxes(aff.reshape(b, s, 128)[:, :, :N_EXPERTS], 1, 2)
    gate, token_idx = lax.top_k(affinity, cap)
    idx_flat = token_idx.reshape(-1).astype(jnp.int32)
    xin = _moe_gather(idx_flat, h.reshape(b, s, d), cap)
    y = _moe_ffn(xin.reshape(N_EXPERTS, b * cap, d), w_gate, w_up, w_down, layer)
    return _moe_combine(idx_flat, gate.reshape(-1), y.reshape(N_EXPERTS, b, cap, d), b, s, cap)


def kernel(x, ln1_w, w_in, mla_q_norm_w, mla_kv_norm_w, mla_w_uq, mla_w_ukv, ssm_conv_w, ssm_conv_b,
           ssm_a_log, ssm_dt_bias, ssm_d, ssm_norm_w, w_out, ln2_w, router_w, exp_w_gate, exp_w_up,
           exp_w_down, final_norm_w):
    b, s, d = x.shape
    depth = w_in.shape[0]
    ret_tabs = _ret_tables()
    ret_rope = _rope_tables(s, HEAD_DIM, RET_THETA, HEAD_DIM, 0, N_HEADS)
    mla_q_rope = _rope_tables(s, MLA_ROPE, ROPE_THETA, 128, MLA_NOPE, N_HEADS)
    mla_k_rope = _rope_tables(s, MLA_ROPE, ROPE_THETA, 128, 0, 1)
    dil_rope = _rope_tables(s, ROPE_DIM, ROPE_THETA, HEAD_DIM, 0, N_HEADS)
    ssd_tabs = _ssd_tables()
    res = [x.reshape(b * s, d)]
    for i in range(depth):
        proj = _inproj(res, ln1_w[i][None], _pad_cols(w_in[i]).astype(BF16)).reshape(b, s, PROJ_PAD)
        y_a = _retention(proj, ret_tabs, ret_rope)
        y_b = _mla(proj, mla_q_norm_w[i], mla_kv_norm_w[i], mla_w_uq[i], mla_w_ukv[i], mla_q_rope, mla_k_rope)
        xbc = _ssd_conv(proj, ssm_conv_w[i], ssm_conv_b[i])
        y_c = _ssd(proj, xbc, ssm_a_log[i], ssm_dt_bias[i], ssm_d[i], ssm_norm_w[i], ssd_tabs)
        y_d = _dilated(proj, dil_rope)
        ys = [y.reshape(b * s, BLK) for y in (y_a, y_b, y_c, y_d)]
        x_mid, h, aff = _outproj(res, ys, w_out[i], ln2_w[i], router_w[i])
        moe = _moe(h, aff, exp_w_gate, exp_w_up, exp_w_down, i, b, s)
        res = [x_mid, moe.reshape(b * s, d)]
    return _final_norm(res[0], res[1], final_norm_w).reshape(b, s, d)
```

```python
import functools
import math

import jax
import jax.numpy as jnp
from jax import lax
from jax.experimental import pallas as pl
from jax.experimental.pallas import tpu as pltpu

F32 = jnp.float32
BF16 = jnp.bfloat16

D_MODEL = 1024
RMS_EPS = 1e-6
GN_EPS = 1e-5
CHUNK = 128
HEAD_DIM = 64
N_HEADS = 4

RET_THETA = 10000.0
ROPE_THETA = 500000.0
ROPE_DIM = 16

MLA_Q_RANK = 256
MLA_KV_RANK = 128
MLA_NOPE = 64
MLA_ROPE = 32

SSM_GROUPS = 2
SSM_STATE = 128
SSM_CONV = 5
SSM_INNER = 256

DIL_PATTERNS = ((128, 1), (512, 4), (2048, 16))

N_EXPERTS = 16
EXPERT_FF = 2048
EC_CAPACITY_FACTOR = 2

BLK = 256
COL_RQ, COL_RK, COL_RV, COL_RG = 0, 1, 2, 3
COL_MCQ, COL_MKV = 4, 5
COL_XBC = 6
COL_Z = 9
COL_DT = 10
COL_DQ, COL_DK, COL_DV = 11, 12, 13
PROJ_PAD = 14 * BLK

VMEM_LIMIT = 56 * 1024 * 1024


def _cparams(sem):
    return pltpu.CompilerParams(dimension_semantics=sem, vmem_limit_bytes=VMEM_LIMIT)


def _split3(x):
    hi = x.astype(BF16)
    r = x - hi.astype(F32)
    mid = r.astype(BF16)
    lo = (r - mid.astype(F32)).astype(BF16)
    return hi, mid, lo


def _dot(a, b):
    return jnp.dot(a, b, preferred_element_type=F32)


def _dot_nt(a, b):
    return lax.dot_general(a, b, (((1,), (1,)), ((), ())), preferred_element_type=F32)


def _dot_x_exact(x, m):
    hi, mid, lo = _split3(x)
    return _dot(hi, m) + _dot(mid, m) + _dot(lo, m)


def _dot_exact_x(m, x):
    hi, mid, lo = _split3(x)
    return _dot(m, hi) + _dot(m, mid) + _dot(m, lo)


def _rope(x, c, s1, s2, half):
    w = x.shape[-1]
    return x * c + pltpu.roll(x, w - half, 1) * s1 + pltpu.roll(x, half, 1) * s2


def _silu(x):
    return x * (1.0 / (1.0 + jnp.exp(-x)))


def _softplus(x):
    return jnp.maximum(x, 0.0) + jnp.log(1.0 + jnp.exp(-jnp.abs(x)))


def _lane_head(shape, width):
    return lax.broadcasted_iota(jnp.int32, shape, 1) // width


def _rms(x, w):
    return x * lax.rsqrt(jnp.mean(x * x, axis=-1, keepdims=True) + RMS_EPS) * w


def _rope_tables(s, rot_dim, theta, head_dim, offset, n_heads):
    half = rot_dim // 2
    inv = 1.0 / (theta ** (jnp.arange(0, rot_dim, 2, dtype=F32) / rot_dim))
    ang = jnp.arange(s, dtype=F32)[:, None] * inv[None, :]
    cos, sin = jnp.cos(ang), jnp.sin(ang)
    pre0 = jnp.zeros((s, offset), F32)
    pre1 = jnp.ones((s, offset), F32)
    post0 = jnp.zeros((s, head_dim - offset - rot_dim), F32)
    post1 = jnp.ones((s, head_dim - offset - rot_dim), F32)
    zh = jnp.zeros((s, half), F32)
    c = jnp.concatenate([pre1, cos, cos, post1], axis=-1)
    s1 = jnp.concatenate([pre0, -sin, zh, post0], axis=-1)
    s2 = jnp.concatenate([pre0, zh, sin, post0], axis=-1)
    tile = lambda t: jnp.tile(t, (1, n_heads))
    return tile(c), tile(s1), tile(s2)


def _tri(n, fn):
    i = jnp.arange(n)
    return fn(i[:, None], i[None, :]).astype(BF16)


def _inproj_kernel(*refs, n_res, col_chunk):
    lnw_ref, w_ref, o_ref = refs[n_res:]
    x = refs[0][...]
    for r in refs[1:n_res]:
        x = x + r[...]
    yb = _rms(x, lnw_ref[...]).astype(BF16)
    for j in range(PROJ_PAD // col_chunk):
        sl = slice(j * col_chunk, (j + 1) * col_chunk)
        o_ref[:, sl] = _dot(yb, w_ref[:, sl])


def _inproj(res, lnw, w_pad):
    t = res[0].shape[0]
    tm = 512
    return pl.pallas_call(
        functools.partial(_inproj_kernel, n_res=len(res), col_chunk=512),
        grid=(t // tm,),
        in_specs=[pl.BlockSpec((tm, D_MODEL), lambda i: (i, 0))] * len(res) + [
            pl.BlockSpec((1, D_MODEL), lambda i: (0, 0)),
            pl.BlockSpec((D_MODEL, PROJ_PAD), lambda i: (0, 0)),
        ],
        out_specs=pl.BlockSpec((tm, PROJ_PAD), lambda i: (i, 0)),
        out_shape=jax.ShapeDtypeStruct((t, PROJ_PAD), F32),
        compiler_params=_cparams(("parallel",)),
    )(*res, lnw, w_pad)


def _pad_cols(w):
    sizes = (256, 256, 256, 256, 256, 128, 32, 256, 768, 8, 256, 256, 256)
    pts, acc = [], 0
    for sz in sizes:
        pts.append((acc, acc + sz))
        acc += sz
    seg = lambda i: w[:, pts[i][0]:pts[i][1]]
    z = lambda n: jnp.zeros((w.shape[0], n), w.dtype)
    cols = [seg(0), seg(1), seg(2), seg(3), seg(4),
            seg(5), seg(6), z(BLK - 128 - 32),
            seg(8), seg(7), seg(9), z(BLK - 8),
            seg(10), seg(11), seg(12)]
    return jnp.concatenate(cols, axis=1)


def _ret_tables():
    pos = jnp.arange(CHUNK, dtype=F32)
    hh = jnp.arange(N_HEADS, dtype=F32)
    lg_f = jnp.log1p(-jnp.exp2(-5.0 - hh))
    lg_b = jnp.log1p(-jnp.exp2(-5.5 - hh))
    diff = pos[:, None] - pos[None, :]
    d_f = jnp.where(diff >= 0, jnp.exp(lg_f[:, None, None] * jnp.maximum(diff, 0.0)), 0.0)
    d_b = jnp.where(diff < 0, jnp.exp(lg_b[:, None, None] * jnp.maximum(-diff, 0.0)), 0.0)
    dmat = (d_f + d_b).reshape(2, 2, CHUNK, CHUNK)

    def lanes(per_head):
        t = jnp.repeat(per_head[:, :, None], HEAD_DIM, axis=2)
        return t.reshape(2, 2, CHUNK, HEAD_DIM).transpose(0, 2, 1, 3).reshape(2, CHUNK, 2 * HEAD_DIM)

    xi_f = lanes(jnp.exp(lg_f[:, None] * (pos + 1.0)))
    zeta_f = lanes(jnp.exp(lg_f[:, None] * (CHUNK - 1.0 - pos)))
    xi_b = lanes(jnp.exp(lg_b[:, None] * (CHUNK - pos)))
    zeta_b = lanes(jnp.exp(lg_b[:, None] * pos))
    blk = (jnp.arange(128)[:, None] // HEAD_DIM) == (jnp.arange(128)[None, :] // HEAD_DIM)

    def cdec(lg):
        g = jnp.exp(lg * CHUNK).reshape(2, 2)
        rows = jnp.repeat(g, HEAD_DIM, axis=1)
        return jnp.where(blk[None], rows[:, :, None], 0.0)

    avg = jnp.where(blk, 1.0 / HEAD_DIM, 0.0).astype(BF16)
    return dmat, xi_f, zeta_f, xi_b, zeta_b, cdec(lg_f), cdec(lg_b), avg


def _ret_kernel(q_ref, k_ref, v_ref, g_ref, c_ref, s1_ref, s2_ref, dmat_ref, xif_ref, zf_ref,
                xib_ref, zb_ref, cdf_ref, cdb_ref, avg_ref, o_ref, sf_ref, sb_ref, pb_ref, *, nsteps, g):
    t = pl.program_id(1)

    @pl.when(t == 0)
    def _():
        sf_ref[...] = jnp.zeros_like(sf_ref)
        sb_ref[...] = jnp.zeros_like(sb_ref)

    def rotated(ref, rows):
        return _rope(ref[rows, :], c_ref[rows, :], s1_ref[rows, :], s2_ref[rows, :], HEAD_DIM // 2)

    def state_update(s_ref, p, k, vb, zeta, cdec):
        kz = (k * zeta).T.astype(BF16)
        blockmask = jnp.where(cdec > 0.0, 1.0, 0.0)
        s_ref[p] = s_ref[p] * cdec + _dot(kz, vb) * blockmask

    @pl.when(t < nsteps)
    def _():
        base = (nsteps - 1 - t) * g
        for ci in reversed(range(g)):
            rows = slice(ci * CHUNK, (ci + 1) * CHUNK)
            k = rotated(k_ref, rows)
            vb = v_ref[rows, :].astype(BF16)
            for p in range(2):
                sl = slice(p * 128, (p + 1) * 128)
                pb_ref[base + ci, p] = sb_ref[p].astype(BF16)
                state_update(sb_ref, p, k[:, sl], vb[:, sl], zb_ref[p], cdb_ref[p])

    @pl.when(t >= nsteps)
    def _():
        base = (t - nsteps) * g
        head = _lane_head((CHUNK, 128), HEAD_DIM)
        avg = avg_ref[...]
        for ci in range(g):
            rows = slice(ci * CHUNK, (ci + 1) * CHUNK)
            k_all = rotated(k_ref, rows)
            q_all = rotated(q_ref, rows) * (HEAD_DIM ** -0.5)
            vb_all = v_ref[rows, :].astype(BF16)
            for p in range(2):
                sl = slice(p * 128, (p + 1) * 128)
                q, k, vb = q_all[:, sl], k_all[:, sl], vb_all[:, sl]
                kb = k.astype(BF16)
                inner = None
                for hh in range(2):
                    qh = jnp.where(head == hh, q, 0.0).astype(BF16)
                    pm = (_dot_nt(qh, kb) * dmat_ref[p, hh]).astype(BF16)
                    oh = _dot(pm, vb)
                    inner = oh if inner is None else jnp.where(head == hh, oh, inner)
                cross_f = _dot((q * xif_ref[p]).astype(BF16), sf_ref[p].astype(BF16))
                cross_b = _dot((q * xib_ref[p]).astype(BF16), pb_ref[base + ci, p])
                o = inner + cross_f + cross_b
                mu = _dot_x_exact(o, avg)
                d = o - mu
                var = _dot_x_exact(d * d, avg)
                o_ref[rows, sl] = _silu(g_ref[rows, sl]) * (d * lax.rsqrt(var + GN_EPS))
                state_update(sf_ref, p, k, vb, zf_ref[p], cdf_ref[p])


def _retention(proj, tables, rope_tabs):
    b, s, _ = proj.shape
    nc = s // CHUNK
    g = math.gcd(nc, 4)
    nsteps = nc // g
    rows = g * CHUNK
    both = lambda t: jnp.where(t < nsteps, nsteps - 1 - t, t - nsteps)
    late = lambda t: jnp.where(t < nsteps, 0, t - nsteps)
    col = lambda base, idx: pl.BlockSpec((None, rows, BLK), lambda bi, t: (bi, idx(t), base))
    tab = pl.BlockSpec((rows, BLK), lambda bi, t: (both(t), 0))
    pair = pl.BlockSpec((2, CHUNK, 128), lambda bi, t: (0, 0, 0))
    return pl.pallas_call(
        functools.partial(_ret_kernel, nsteps=nsteps, g=g),
        grid=(b, 2 * nsteps),
        in_specs=[
            col(COL_RQ, late), col(COL_RK, both), col(COL_RV, both), col(COL_RG, late),
            tab, tab, tab,
            pl.BlockSpec((2, 2, CHUNK, CHUNK), lambda bi, t: (0, 0, 0, 0)),
            pair, pair, pair, pair, pair, pair,
            pl.BlockSpec((128, 128), lambda bi, t: (0, 0)),
        ],
        out_specs=pl.BlockSpec((None, rows, BLK), lambda bi, t: (bi, late(t), 0)),
        out_shape=jax.ShapeDtypeStruct((b, s, BLK), F32),
        scratch_shapes=[pltpu.VMEM((2, 128, 128), F32), pltpu.VMEM((2, 128, 128), F32),
                        pltpu.VMEM((nc, 2, 128, 128), BF16)],
        compiler_params=_cparams(("parallel", "arbitrary")),
    )(proj, proj, proj, proj, *rope_tabs, *tables)


def _mla_prep_kernel(cq_ref, kv_ref, qnw_ref, kvnw_ref, wq_ref, wk_ref, wv_ref, place_ref,
                     qc_ref, qs1_ref, qs2_ref, kc_ref, ks1_ref, ks2_ref, q_ref, k_ref, v_ref):
    cqn = _rms(cq_ref[...], qnw_ref[...])
    q = _dot(cqn.astype(BF16), wq_ref[...])
    q = _rope(q, qc_ref[...], qs1_ref[...], qs2_ref[...], MLA_ROPE // 2)
    q_ref[...] = (q * ((MLA_NOPE + MLA_ROPE) ** -0.5 * math.log2(math.e))).astype(BF16)
    blk = kv_ref[...]
    ckvn = _rms(blk[:, :MLA_KV_RANK], kvnw_ref[...]).astype(BF16)
    kr = _rope(blk[:, MLA_KV_RANK:], kc_ref[...], ks1_ref[...], ks2_ref[...], MLA_ROPE // 2)
    k_ref[...] = (_dot(ckvn, wk_ref[...]) + _dot(kr.astype(BF16), place_ref[...])).astype(BF16)
    v = _dot(ckvn, wv_ref[...])
    ones = (lax.broadcasted_iota(jnp.int32, v.shape, 1) % 128) >= HEAD_DIM
    v_ref[...] = jnp.where(ones, 1.0, v).astype(BF16)


def _mla_weights(w_uq, w_ukv):
    qh = w_uq.reshape(MLA_Q_RANK, N_HEADS, MLA_NOPE + MLA_ROPE)
    wq = jnp.concatenate([qh, jnp.zeros((MLA_Q_RANK, N_HEADS, 32), F32)], axis=-1).reshape(MLA_Q_RANK, 512)
    kvh = w_ukv.reshape(MLA_KV_RANK, N_HEADS, MLA_NOPE + HEAD_DIM)
    zk = jnp.zeros((MLA_KV_RANK, N_HEADS, 64), F32)
    wk = jnp.concatenate([kvh[..., :MLA_NOPE], zk], axis=-1).reshape(MLA_KV_RANK, 512)
    wv = jnp.concatenate([kvh[..., MLA_NOPE:], zk], axis=-1).reshape(MLA_KV_RANK, 512)
    src = jnp.arange(128)[:, None]
    dst = jnp.arange(512)[None, :]
    place = ((dst % 128 == src + MLA_NOPE) & (src < MLA_ROPE)).astype(BF16)
    return wq.astype(BF16), wk.astype(BF16), wv.astype(BF16), place


def _mla_flash_kernel(q_ref, k_ref, v_ref, o_ref, m_ref, acc_ref, *, strip):
    ki = pl.program_id(2)

    @pl.when(ki == 0)
    def _():
        m_ref[...] = jnp.full_like(m_ref, -jnp.inf)
        acc_ref[...] = jnp.zeros_like(acc_ref)

    for h in range(N_HEADS):
        sl = slice(h * 128, (h + 1) * 128)
        k = k_ref[:, sl]
        v = v_ref[:, sl]
        for r in range(q_ref.shape[0] // strip):
            rows = slice(r * strip, (r + 1) * strip)
            s = _dot_nt(q_ref[rows, sl], k)
            m_prev = m_ref[h, rows]
            m_new = jnp.maximum(m_prev, jnp.max(s, axis=-1, keepdims=True))
            p = jnp.exp2(s - m_new[:, :1])
            acc_ref[h, rows] = jnp.exp2(m_prev - m_new) * acc_ref[h, rows] + _dot(p.astype(BF16), v)
            m_ref[h, rows] = m_new

    @pl.when(ki == pl.num_programs(2) - 1)
    def _():
        outs = []
        for h in range(N_HEADS):
            a = acc_ref[h]
            outs.append(a[:, :HEAD_DIM] / a[:, HEAD_DIM:])
        o_ref[...] = jnp.concatenate(outs, axis=-1)


def _mla(proj, q_norm_w, kv_norm_w, w_uq, w_ukv, q_tabs, k_tabs):
    b, s, _ = proj.shape
    wq, wk, wv, place = _mla_weights(w_uq, w_ukv)
    ts = 512
    row = lambda n: pl.BlockSpec((ts, n), lambda bi, i: (i, 0))
    full = lambda r, n: pl.BlockSpec((r, n), lambda bi, i: (0, 0))
    out = pl.BlockSpec((None, ts, 512), lambda bi, i: (bi, i, 0))
    q, k, v = pl.pallas_call(
        _mla_prep_kernel,
        grid=(b, s // ts),
        in_specs=[
            pl.BlockSpec((None, ts, BLK), lambda bi, i: (bi, i, COL_MCQ)),
            pl.BlockSpec((None, ts, BLK), lambda bi, i: (bi, i, COL_MKV)),
            full(1, MLA_Q_RANK), full(1, MLA_KV_RANK),
            full(MLA_Q_RANK, 512), full(MLA_KV_RANK, 512), full(MLA_KV_RANK, 512), full(128, 512),
            row(512), row(512), row(512), row(128), row(128), row(128),
        ],
        out_specs=[out, out, out],
        out_shape=[jax.ShapeDtypeStruct((b, s, 512), BF16)] * 3,
        compiler_params=_cparams(("parallel", "parallel")),
    )(proj, proj, q_norm_w[None], kv_norm_w[None], wq, wk, wv, place, *q_tabs, *k_tabs)

    tq, tk = 2048, 512
    return pl.pallas_call(
        functools.partial(_mla_flash_kernel, strip=tq),
        grid=(b, s // tq, s // tk),
        in_specs=[
            pl.BlockSpec((None, tq, 512), lambda bi, qi, ki: (bi, qi, 0)),
            pl.BlockSpec((None, tk, 512), lambda bi, qi, ki: (bi, ki, 0)),
            pl.BlockSpec((None, tk, 512), lambda bi, qi, ki: (bi, ki, 0)),
        ],
        out_specs=pl.BlockSpec((None, tq, N_HEADS * HEAD_DIM), lambda bi, qi, ki: (bi, qi, 0)),
        out_shape=jax.ShapeDtypeStruct((b, s, N_HEADS * HEAD_DIM), F32),
        scratch_shapes=[pltpu.VMEM((N_HEADS, tq, 128), F32), pltpu.VMEM((N_HEADS, tq, 128), F32)],
        compiler_params=_cparams(("parallel", "parallel", "arbitrary")),
    )(q, k, v)


def _conv_kernel(prev_ref, cur_ref, next_ref, w_ref, b_ref, o_ref, *, ts):
    i = pl.program_id(1)
    w = w_ref[...]

    def conv(x):
        n = x.shape[0]
        acc = x * w[2:3]
        for s in (-2, -1, 1, 2):
            acc = acc + pltpu.roll(x, (-s) % n, 0) * w[s + 2:s + 3]
        return acc

    act = lambda y: _silu(y + b_ref[...])
    cur = cur_ref[...]
    prev = prev_ref[...] * (i > 0).astype(F32)
    nxt = next_ref[...] * (i < pl.num_programs(1) - 1).astype(F32)
    o_ref[...] = act(conv(cur))
    top = conv(jnp.concatenate([prev, cur[:16]], axis=0))
    o_ref[0:8, :] = act(top[8:16])
    bot = conv(jnp.concatenate([cur[ts - 16:], nxt], axis=0))
    o_ref[ts - 8:ts, :] = act(bot[8:16])


def _ssd_conv(proj, conv_w, conv_b):
    b, s, _ = proj.shape
    ts = 512
    width = 3 * BLK
    cb = COL_XBC * BLK // width
    nb8 = ts // 8
    w8 = jnp.concatenate([conv_w, jnp.zeros((8 - SSM_CONV, width), F32)], axis=0)
    return pl.pallas_call(
        functools.partial(_conv_kernel, ts=ts),
        grid=(b, s // ts),
        in_specs=[
            pl.BlockSpec((None, 8, width), lambda bi, i: (bi, jnp.maximum(i * nb8 - 1, 0), cb)),
            pl.BlockSpec((None, ts, width), lambda bi, i: (bi, i, cb)),
            pl.BlockSpec((None, 8, width), lambda bi, i: (bi, jnp.minimum((i + 1) * nb8, s // 8 - 1), cb)),
            pl.BlockSpec((8, width), lambda bi, i: (0, 0)),
            pl.BlockSpec((1, width), lambda bi, i: (0, 0)),
        ],
        out_specs=pl.BlockSpec((None, ts, width), lambda bi, i: (bi, i, 0)),
        out_shape=jax.ShapeDtypeStruct((b, s, width), F32),
        compiler_params=_cparams(("parallel", "parallel")),
    )(proj, proj, proj, w8, conv_b[None])


def _ssd_tables():
    tril = _tri(CHUNK, lambda i, j: j <= i)
    triu = _tri(CHUNK, lambda i, j: j >= i)
    h = jnp.arange(128)[:, None]
    lane = jnp.arange(256)[None, :]
    e_f = ((lane // HEAD_DIM == h) & (h < N_HEADS)).astype(BF16)
    e_b = ((lane // HEAD_DIM == h - N_HEADS) & (h >= N_HEADS) & (h < 2 * N_HEADS)).astype(BF16)
    return tril, triu, e_f, e_b


def _ssd_kernel(xs_ref, bm_ref, cm_ref, z_ref, dt_ref, dtt_ref, bias_ref, a_ref, biasc_ref, ac_ref,
                dskip_ref, nw_ref, tril_ref, triu_ref, ef_ref, eb_ref, o_ref,
                sf_ref, sb_ref, pb_ref, *, nsteps, g):
    t = pl.program_id(1)
    tril, triu = tril_ref[...], triu_ref[...]

    @pl.when(t == 0)
    def _():
        sf_ref[...] = jnp.zeros_like(sf_ref)
        sb_ref[...] = jnp.zeros_like(sb_ref)

    def step_sizes(rows):
        dt = _softplus(dt_ref[rows, :128] + bias_ref[...])
        return dt, dt * a_ref[...]

    def backward_terms(xs, dt, dta):
        rcs_b = _dot_x_exact(_dot_exact_x(triu, dta), eb_ref[...])
        return rcs_b, rcs_b[0:1, :], xs * _dot_x_exact(dt, eb_ref[...])

    def state_update(s_ref, bmf, weighted_x, total):
        for gi in range(SSM_GROUPS):
            sl = slice(gi * 128, (gi + 1) * 128)
            upd = _dot(bmf[:, sl].T.astype(BF16), weighted_x[:, sl].astype(BF16))
            s_ref[gi] = s_ref[gi] * jnp.exp(total[:, sl]) + upd

    @pl.when(t < nsteps)
    def _():
        base = (nsteps - 1 - t) * g
        for ci in reversed(range(g)):
            rows = slice(ci * CHUNK, (ci + 1) * CHUNK)
            xs = xs_ref[rows, :]
            dt, dta = step_sizes(rows)
            rcs_b, tot_b, xdt_b = backward_terms(xs, dt, dta)
            pb_ref[base + ci] = sb_ref[...].astype(BF16)
            state_update(sb_ref, bm_ref[rows, :], jnp.exp(tot_b - rcs_b) * xdt_b, tot_b)

    @pl.when(t >= nsteps)
    def _():
        base = (t - nsteps) * g
        row = lax.broadcasted_iota(jnp.int32, (CHUNK, CHUNK), 0)
        colm = lax.broadcasted_iota(jnp.int32, (CHUNK, CHUNK), 1)
        head = _lane_head((CHUNK, 128), HEAD_DIM)
        for ci in range(g):
            rows = slice(ci * CHUNK, (ci + 1) * CHUNK)
            xs = xs_ref[rows, :]
            bmf = bm_ref[rows, :]
            bm = bmf.astype(BF16)
            cm = cm_ref[rows, :].astype(BF16)
            dt, dta = step_sizes(rows)
            rcs_b, tot_b, xdt_b = backward_terms(xs, dt, dta)
            cs_f = _dot_x_exact(_dot_exact_x(tril, dta), ef_ref[...])
            tot_f = cs_f[CHUNK - 1:CHUNK, :]
            xdt_f = xs * _dot_x_exact(dt, ef_ref[...])
            dtat = _softplus(dtt_ref[:, rows] + biasc_ref[...]) * ac_ref[...]
            cst = _dot_x_exact(dtat, triu)
            rcst = _dot_x_exact(dtat, tril)
            pb = pb_ref[base + ci]
            ys = []
            for gi in range(SSM_GROUPS):
                sl = slice(gi * 128, (gi + 1) * 128)
                cb = _dot_nt(cm[:, sl], bm[:, sl])
                xcat = jnp.concatenate([xdt_f[:, sl], xdt_b[:, sl]], axis=0).astype(BF16)
                yg = None
                for hh in range(2):
                    h = 2 * gi + hh
                    c0 = h * HEAD_DIM
                    seg_f = cs_f[:, c0:c0 + 1] - cst[h:h + 1, :]
                    seg_b = rcs_b[:, c0:c0 + 1] - rcst[N_HEADS + h:N_HEADS + h + 1, :]
                    dec_f = jnp.where(row >= colm, jnp.exp(jnp.minimum(seg_f, 0.0)), 0.0)
                    dec_b = jnp.where(row < colm, jnp.exp(jnp.minimum(seg_b, 0.0)), 0.0)
                    wcat = jnp.concatenate([cb * dec_f, cb * dec_b], axis=1).astype(BF16)
                    yh = _dot(wcat, xcat)
                    yg = yh if yg is None else jnp.where(head == hh, yh, yg)
                off_f = jnp.exp(cs_f[:, sl]) * _dot(cm[:, sl], sf_ref[gi].astype(BF16))
                off_b = jnp.exp(rcs_b[:, sl]) * _dot(cm[:, sl], pb[gi])
                ys.append(yg + off_f + off_b)
            y = jnp.concatenate(ys, axis=1) + dskip_ref[...] * xs
            y = y * _silu(z_ref[rows, :])
            o_ref[rows, :] = _rms(y, nw_ref[...])
            state_update(sf_ref, bmf, jnp.exp(tot_f - cs_f) * xdt_f, tot_f)


def _ssd(proj, xbc, a_log, dt_bias, d_skip, norm_w, tables):
    b, s, _ = proj.shape
    nc = s // CHUNK
    g = math.gcd(nc, 2)
    nsteps = nc // g
    rows = g * CHUNK
    dt_t = jnp.swapaxes(proj[:, :, COL_DT * BLK:COL_DT * BLK + 8], 1, 2)
    pad_row = lambda v: jnp.concatenate([v.reshape(1, 8), jnp.zeros((1, 120), F32)], axis=1)
    a = -jnp.exp(a_log.astype(F32))
    both = lambda t: jnp.where(t < nsteps, nsteps - 1 - t, t - nsteps)
    late = lambda t: jnp.where(t < nsteps, 0, t - nsteps)
    full = lambda r, n: pl.BlockSpec((r, n), lambda bi, t: (0, 0))
    blk = lambda idx, c: pl.BlockSpec((None, rows, BLK), lambda bi, t: (bi, idx(t), c))
    return pl.pallas_call(
        functools.partial(_ssd_kernel, nsteps=nsteps, g=g),
        grid=(b, 2 * nsteps),
        in_specs=[
            blk(both, 0), blk(both, 1), blk(late, 2), blk(late, COL_Z), blk(both, COL_DT),
            pl.BlockSpec((None, 8, rows), lambda bi, t: (bi, 0, late(t))),
            full(1, 128), full(1, 128), full(8, 1), full(8, 1),
            full(1, SSM_INNER), full(1, SSM_INNER),
            full(CHUNK, CHUNK), full(CHUNK, CHUNK), full(128, 256), full(128, 256),
        ],
        out_specs=blk(late, 0),
        out_shape=jax.ShapeDtypeStruct((b, s, SSM_INNER), F32),
        scratch_shapes=[pltpu.VMEM((SSM_GROUPS, 128, 128), F32), pltpu.VMEM((SSM_GROUPS, 128, 128), F32),
                        pltpu.VMEM((nc, SSM_GROUPS, 128, 128), BF16)],
        compiler_params=_cparams(("parallel", "arbitrary")),
    )(xbc, xbc, xbc, proj, proj, dt_t, pad_row(dt_bias), pad_row(a), dt_bias.reshape(8, 1), a.reshape(8, 1),
      jnp.repeat(d_skip, HEAD_DIM)[None], norm_w[None], *tables)


def _dil_prep_kernel(q_ref, k_ref, c_ref, s1_ref, s2_ref, qo_ref, ko_ref):
    c, s1, s2 = c_ref[...], s1_ref[...], s2_ref[...]
    qo_ref[...] = _rope(q_ref[...], c, s1, s2, ROPE_DIM // 2) * (HEAD_DIM ** -0.5)
    ko_ref[...] = _rope(k_ref[...], c, s1, s2, ROPE_DIM // 2)


def _dil_kernel(q_ref, k_ref, v_ref, o_ref, m_ref, l_ref, *, s):
    head = _lane_head((128, 128), HEAD_DIM)
    kw = 2 * 128
    for pi, (win, d) in enumerate(DIL_PATTERNS):
        half = win // (2 * d)
        seg = s // d
        per_seg = seg // 128

        def body(i, carry, d=d, pi=pi, half=half, seg=seg, per_seg=per_seg):
            r = i // per_seg
            m0 = (i % per_seg) * 128
            ks = jnp.clip(m0 - half, 0, seg - kw)
            if d == 1:
                qrows = pl.ds(pl.multiple_of(m0, 128), 128)
                krows = pl.ds(pl.multiple_of(ks, 64), kw)
            else:
                qrows = pl.ds(r + d * m0, 128, stride=d)
                krows = pl.ds(r + d * ks, kw, stride=d)
            q = q_ref[qrows, :]
            kb = k_ref[krows, :].astype(BF16)
            vb = v_ref[krows, :].astype(BF16)
            qpos = m0 + lax.broadcasted_iota(jnp.int32, (128, kw), 0)
            kpos = ks + lax.broadcasted_iota(jnp.int32, (128, kw), 1)
            valid = jnp.abs(qpos - kpos) <= half
            mb = lb = pv = None
            for hh in range(2):
                sc = _dot_nt(jnp.where(head == hh, q, 0.0).astype(BF16), kb)
                sc = jnp.where(valid, sc, -jnp.inf)
                mh = jnp.max(sc, axis=-1, keepdims=True)
                p = jnp.exp(sc - mh)
                lh = jnp.sum(p, axis=-1, keepdims=True)
                ph = _dot(p.astype(BF16), vb)
                if hh == 0:
                    mb = jnp.broadcast_to(mh, (128, 128))
                    lb = jnp.broadcast_to(lh, (128, 128))
                    pv = ph
                else:
                    mb = jnp.where(head == hh, mh, mb)
                    lb = jnp.where(head == hh, lh, lb)
                    pv = jnp.where(head == hh, ph, pv)
            if pi == 0:
                m_ref[qrows, :] = mb
                l_ref[qrows, :] = lb
                o_ref[qrows, :] = pv
            else:
                mo = m_ref[qrows, :]
                mn = jnp.maximum(mo, mb)
                wa = jnp.exp(mo - mn)
                wb = jnp.exp(mb - mn)
                m_ref[qrows, :] = mn
                l_ref[qrows, :] = wa * l_ref[qrows, :] + wb * lb
                o_ref[qrows, :] = wa * o_ref[qrows, :] + wb * pv
            return carry

        lax.fori_loop(0, s // 128, body, 0, unroll=4)
    o_ref[...] = o_ref[...] / l_ref[...]


def _dilated(proj, tabs):
    b, s, _ = proj.shape
    assert all(s // d >= 256 for _, d in DIL_PATTERNS)
    ts = 512
    rowt = pl.BlockSpec((ts, BLK), lambda bi, i: (i, 0))
    outb = pl.BlockSpec((None, ts, BLK), lambda bi, i: (bi, i, 0))
    qr, kr = pl.pallas_call(
        _dil_prep_kernel,
        grid=(b, s // ts),
        in_specs=[
            pl.BlockSpec((None, ts, BLK), lambda bi, i: (bi, i, COL_DQ)),
            pl.BlockSpec((None, ts, BLK), lambda bi, i: (bi, i, COL_DK)),
            rowt, rowt, rowt,
        ],
        out_specs=[outb, outb],
        out_shape=[jax.ShapeDtypeStruct((b, s, BLK), F32)] * 2,
        compiler_params=_cparams(("parallel", "parallel")),
    )(proj, proj, *tabs)
    seqb = lambda c: pl.BlockSpec((None, s, 128), lambda bi, p: (bi, 0, 2 * c + p))
    return pl.pallas_call(
        functools.partial(_dil_kernel, s=s),
        grid=(b, 2),
        in_specs=[seqb(0), seqb(0), seqb(COL_DV)],
        out_specs=seqb(0),
        out_shape=jax.ShapeDtypeStruct((b, s, BLK), F32),
        scratch_shapes=[pltpu.VMEM((s, 128), F32), pltpu.VMEM((s, 128), F32)],
        compiler_params=_cparams(("parallel", "parallel")),
    )(qr, kr, proj)


def _outproj_kernel(*refs, n_res):
    ya_ref, yb_ref, yc_ref, yd_ref, w_ref, lnw_ref, rw_ref, xo_ref, h_ref, aff_ref = refs[n_res:]
    acc = refs[0][...]
    for r in refs[1:n_res]:
        acc = acc + r[...]
    for i, y_ref in enumerate((ya_ref, yb_ref, yc_ref, yd_ref)):
        acc = acc + _dot(y_ref[...].astype(BF16), w_ref[i * BLK:(i + 1) * BLK, :])
    xo_ref[...] = acc
    h = _rms(acc, lnw_ref[...])
    h_hi = h.astype(BF16)
    h_ref[...] = h_hi
    h_lo = (h - h_hi.astype(F32)).astype(BF16)
    rw = rw_ref[...]
    w_hi = rw.astype(BF16)
    w_lo = (rw - w_hi.astype(F32)).astype(BF16)
    logits = _dot(h_hi, w_hi) + _dot(h_hi, w_lo) + _dot(h_lo, w_hi) + _dot(h_lo, w_lo)
    lane = lax.broadcasted_iota(jnp.int32, logits.shape, 1)
    logits = jnp.where(lane < N_EXPERTS, logits, -jnp.inf)
    e = jnp.exp(logits - jnp.max(logits, axis=-1, keepdims=True))
    aff_ref[...] = e / jnp.sum(e, axis=-1, keepdims=True)


def _outproj(res, ys, w_out, ln2_w, router_w):
    t = res[0].shape[0]
    tm = 256
    rw = jnp.concatenate([router_w, jnp.zeros((D_MODEL, 128 - N_EXPERTS), F32)], axis=1)
    row = lambda n: pl.BlockSpec((tm, n), lambda i: (i, 0))
    full = lambda r, n: pl.BlockSpec((r, n), lambda i: (0, 0))
    return pl.pallas_call(
        functools.partial(_outproj_kernel, n_res=len(res)),
        grid=(t // tm,),
        in_specs=[row(D_MODEL)] * len(res) + [row(BLK)] * 4 + [
            full(D_MODEL, D_MODEL), full(1, D_MODEL), full(D_MODEL, 128)],
        out_specs=[row(D_MODEL), row(D_MODEL), row(128)],
        out_shape=[jax.ShapeDtypeStruct((t, D_MODEL), F32),
                   jax.ShapeDtypeStruct((t, D_MODEL), BF16),
                   jax.ShapeDtypeStruct((t, 128), F32)],
        compiler_params=_cparams(("parallel",)),
    )(*res, *ys, w_out.astype(BF16), ln2_w[None], rw)


ROWS_PER_ITER = 8


def _gather_kernel(idx_ref, h_ref, o_ref, hf_ref, buf_ref, *, cap, n_exp):
    e = pl.program_id(1)
    base = (pl.program_id(0) * n_exp + e) * cap

    @pl.when(e == 0)
    def _():
        hf_ref[...] = h_ref[...].astype(F32)

    def body(jb, carry):
        j0 = jb * ROWS_PER_ITER
        vals = [hf_ref[pl.ds(idx_ref[base + j0 + u], 1), :] for u in range(ROWS_PER_ITER)]
        for u in range(ROWS_PER_ITER):
            buf_ref[pl.ds(j0 + u, 1), :] = vals[u]
        return carry

    lax.fori_loop(0, cap // ROWS_PER_ITER, body, 0)
    o_ref[...] = buf_ref[...].astype(BF16)


def _moe_gather(idx_flat, h, cap):
    b, s, d = h.shape
    return pl.pallas_call(
        functools.partial(_gather_kernel, cap=cap, n_exp=N_EXPERTS),
        grid_spec=pltpu.PrefetchScalarGridSpec(
            num_scalar_prefetch=1,
            grid=(b, N_EXPERTS),
            in_specs=[pl.BlockSpec((None, s, d), lambda bi, e, idx: (bi, 0, 0))],
            out_specs=pl.BlockSpec((None, None, cap, d), lambda bi, e, idx: (e, bi, 0, 0)),
            scratch_shapes=[pltpu.VMEM((s, d), F32), pltpu.VMEM((cap, d), F32)],
        ),
        out_shape=jax.ShapeDtypeStruct((N_EXPERTS, b, cap, d), BF16),
        compiler_params=_cparams(("parallel", "arbitrary")),
    )(idx_flat, h)


def _ffn_kernel(x_ref, wg_ref, wu_ref, wd_ref, o_ref, *, rows, tr):
    f = pl.program_id(1)
    wg = wg_ref[...].astype(BF16)
    wu = wu_ref[...].astype(BF16)
    wd = wd_ref[...].astype(BF16)

    def partial_out(sl):
        x = x_ref[sl, :]
        hid = (_silu(_dot(x, wg)) * _dot(x, wu)).astype(BF16)
        return _dot(hid, wd)

    for r in range(rows // tr):
        sl = slice(r * tr, (r + 1) * tr)

        @pl.when(f == 0)
        def _():
            o_ref[sl, :] = partial_out(sl)

        @pl.when(f > 0)
        def _():
            o_ref[sl, :] += partial_out(sl)


def _moe_ffn(xin, w_gate, w_up, w_down, layer):
    n_exp, rows, d = xin.shape
    tf = 512
    return pl.pallas_call(
        functools.partial(_ffn_kernel, rows=rows, tr=math.gcd(rows, 1024)),
        grid=(n_exp, EXPERT_FF // tf),
        in_specs=[
            pl.BlockSpec((None, rows, d), lambda e, f: (e, 0, 0)),
            pl.BlockSpec((None, None, d, tf), lambda e, f: (layer, e, 0, f)),
            pl.BlockSpec((None, None, d, tf), lambda e, f: (layer, e, 0, f)),
            pl.BlockSpec((None, None, tf, d), lambda e, f: (layer, e, f, 0)),
        ],
        out_specs=pl.BlockSpec((None, rows, d), lambda e, f: (e, 0, 0)),
        out_shape=jax.ShapeDtypeStruct((n_exp, rows, d), F32),
        compiler_params=_cparams(("parallel", "arbitrary")),
    )(xin, w_gate, w_up, w_down)


def _combine_kernel(idx_ref, gate_ref, y_ref, o_ref, *, cap, n_exp):
    e = pl.program_id(1)
    base = (pl.program_id(0) * n_exp + e) * cap

    @pl.when(e == 0)
    def _():
        o_ref[...] = jnp.zeros_like(o_ref)

    def body(jb, carry):
        j0 = jb * ROWS_PER_ITER
        rows = [idx_ref[base + j0 + u] for u in range(ROWS_PER_ITER)]
        vals = [o_ref[pl.ds(rows[u], 1), :] + gate_ref[base + j0 + u] * y_ref[pl.ds(j0 + u, 1), :]
                for u in range(ROWS_PER_ITER)]
        for u in range(ROWS_PER_ITER):
            o_ref[pl.ds(rows[u], 1), :] = vals[u]
        return carry

    lax.fori_loop(0, cap // ROWS_PER_ITER, body, 0)


def _moe_combine(idx_flat, gate_flat, y, b, s, cap):
    d = y.shape[-1]
    return pl.pallas_call(
        functools.partial(_combine_kernel, cap=cap, n_exp=N_EXPERTS),
        grid_spec=pltpu.PrefetchScalarGridSpec(
            num_scalar_prefetch=2,
            grid=(b, N_EXPERTS),
            in_specs=[pl.BlockSpec((None, None, cap, d), lambda bi, e, idx, gate: (e, bi, 0, 0))],
            out_specs=pl.BlockSpec((None, s, d), lambda bi, e, idx, gate: (bi, 0, 0)),
        ),
        out_shape=jax.ShapeDtypeStruct((b, s, d), F32),
        compiler_params=_cparams(("parallel", "arbitrary")),
    )(idx_flat, gate_flat, y)


def _final_norm_kernel(xa_ref, xb_ref, w_ref, o_ref):
    o_ref[...] = _rms(xa_ref[...] + xb_ref[...], w_ref[...])


def _final_norm(xa, xb, w):
    t = xa.shape[0]
    tm = 1024
    row = pl.BlockSpec((tm, D_MODEL), lambda i: (i, 0))
    return pl.pallas_call(
        _final_norm_kernel,
        grid=(t // tm,),
        in_specs=[row, row, pl.BlockSpec((1, D_MODEL), lambda i: (0, 0))],
        out_specs=row,
        out_shape=jax.ShapeDtypeStruct((t, D_MODEL), F32),
        compiler_params=_cparams(("parallel",)),
    )(xa, xb, w[None])


def _moe(h, aff, w_gate, w_up, w_down, layer, b, s):
    d = h.shape[-1]
    cap = EC_CAPACITY_FACTOR * s // N_EXPERTS
    affinity = jnp.swapaxes(aff.reshape(b, s, 128)[:, :, :N_EXPERTS], 1, 2)
    gate, token_idx = lax.top_k(affinity, cap)
    idx_flat = token_idx.reshape(-1).astype(jnp.int32)
    xin = _moe_gather(idx_flat, h.reshape(b, s, d), cap)
    y = _moe_ffn(xin.reshape(N_EXPERTS, b * cap, d), w_gate, w_up, w_down, layer)
    return _moe_combine(idx_flat, gate.reshape(-1), y.reshape(N_EXPERTS, b, cap, d), b, s, cap)


def kernel(x, ln1_w, w_in, mla_q_norm_w, mla_kv_norm_w, mla_w_uq, mla_w_ukv, ssm_conv_w, ssm_conv_b,
           ssm_a_log, ssm_dt_bias, ssm_d, ssm_norm_w, w_out, ln2_w, router_w, exp_w_gate, exp_w_up,
           exp_w_down, final_norm_w):
    b, s, d = x.shape
    depth = w_in.shape[0]
    ret_tabs = _ret_tables()
    ret_rope = _rope_tables(s, HEAD_DIM, RET_THETA, HEAD_DIM, 0, N_HEADS)
    mla_q_rope = _rope_tables(s, MLA_ROPE, ROPE_THETA, 128, MLA_NOPE, N_HEADS)
    mla_k_rope = _rope_tables(s, MLA_ROPE, ROPE_THETA, 128, 0, 1)
    dil_rope = _rope_tables(s, ROPE_DIM, ROPE_THETA, HEAD_DIM, 0, N_HEADS)
    ssd_tabs = _ssd_tables()
    res = [x.reshape(b * s, d)]
    for i in range(depth):
        proj = _inproj(res, ln1_w[i][None], _pad_cols(w_in[i]).astype(BF16)).reshape(b, s, PROJ_PAD)
        y_a = _retention(proj, ret_tabs, ret_rope)
        y_b = _mla(proj, mla_q_norm_w[i], mla_kv_norm_w[i], mla_w_uq[i], mla_w_ukv[i], mla_q_rope, mla_k_rope)
        xbc = _ssd_conv(proj, ssm_conv_w[i], ssm_conv_b[i])
        y_c = _ssd(proj, xbc, ssm_a_log[i], ssm_dt_bias[i], ssm_d[i], ssm_norm_w[i], ssd_tabs)
        y_d = _dilated(proj, dil_rope)
        ys = [y.reshape(b * s, BLK) for y in (y_a, y_b, y_c, y_d)]
        x_mid, h, aff = _outproj(res, ys, w_out[i], ln2_w[i], router_w[i])
        moe = _moe(h, aff, exp_w_gate, exp_w_up, exp_w_down, i, b, s)
        res = [x_mid, moe.reshape(b * s, d)]
    return _final_norm(res[0], res[1], final_norm_w).reshape(b, s, d)
```

```python
import functools
import math

import jax
import jax.numpy as jnp
from jax import lax
from jax.experimental import pallas as pl
from jax.experimental.pallas import tpu as pltpu

F32 = jnp.float32
BF16 = jnp.bfloat16

D_MODEL = 1024
RMS_EPS = 1e-6
GN_EPS = 1e-5
CHUNK = 128
HEAD_DIM = 64
N_HEADS = 4

RET_THETA = 10000.0
ROPE_THETA = 500000.0
ROPE_DIM = 16

MLA_Q_RANK = 256
MLA_KV_RANK = 128
MLA_NOPE = 64
MLA_ROPE = 32

SSM_GROUPS = 2
SSM_STATE = 128
SSM_CONV = 5
SSM_INNER = 256

DIL_PATTERNS = ((128, 1), (512, 4), (2048, 16))

N_EXPERTS = 16
EXPERT_FF = 2048
EC_CAPACITY_FACTOR = 2

BLK = 256
COL_RQ, COL_RK, COL_RV, COL_RG = 0, 1, 2, 3
COL_MCQ, COL_MKV = 4, 5
COL_XBC = 6
COL_Z = 9
COL_DT = 10
COL_DQ, COL_DK, COL_DV = 11, 12, 13
PROJ_PAD = 14 * BLK

VMEM_LIMIT = 56 * 1024 * 1024


def _cparams(sem):
    return pltpu.CompilerParams(dimension_semantics=sem, vmem_limit_bytes=VMEM_LIMIT)


def _split2(x):
    hi = x.astype(BF16)
    lo = (x - hi.astype(F32)).astype(BF16)
    return hi, lo


def _dot(a, b):
    return jnp.dot(a, b, preferred_element_type=F32)


def _dot_nt(a, b):
    return lax.dot_general(a, b, (((1,), (1,)), ((), ())), preferred_element_type=F32)


def _dot_x_exact(x, m):
    hi, lo = _split2(x)
    return _dot(jnp.concatenate([hi, lo], axis=1), jnp.concatenate([m, m], axis=0))


def _dot_exact_x(m, x):
    hi, lo = _split2(x)
    n = x.shape[1]
    r = _dot(m, jnp.concatenate([hi, lo], axis=1))
    return r[:, :n] + r[:, n:]


def _rope(x, c, s1, s2, half):
    w = x.shape[-1]
    return x * c + pltpu.roll(x, w - half, 1) * s1 + pltpu.roll(x, half, 1) * s2


def _silu(x):
    return x * (1.0 / (1.0 + jnp.exp(-x)))


def _softplus(x):
    return jnp.maximum(x, 0.0) + jnp.log(1.0 + jnp.exp(-jnp.abs(x)))


def _lane_head(shape, width):
    return lax.broadcasted_iota(jnp.int32, shape, 1) // width


def _rms(x, w):
    return x * lax.rsqrt(jnp.mean(x * x, axis=-1, keepdims=True) + RMS_EPS) * w


def _rope_tables(s, rot_dim, theta, head_dim, offset, n_heads):
    half = rot_dim // 2
    inv = 1.0 / (theta ** (jnp.arange(0, rot_dim, 2, dtype=F32) / rot_dim))
    ang = jnp.arange(s, dtype=F32)[:, None] * inv[None, :]
    cos, sin = jnp.cos(ang), jnp.sin(ang)
    pre0 = jnp.zeros((s, offset), F32)
    pre1 = jnp.ones((s, offset), F32)
    post0 = jnp.zeros((s, head_dim - offset - rot_dim), F32)
    post1 = jnp.ones((s, head_dim - offset - rot_dim), F32)
    zh = jnp.zeros((s, half), F32)
    c = jnp.concatenate([pre1, cos, cos, post1], axis=-1)
    s1 = jnp.concatenate([pre0, -sin, zh, post0], axis=-1)
    s2 = jnp.concatenate([pre0, zh, sin, post0], axis=-1)
    tile = lambda t: jnp.tile(t, (1, n_heads))
    return tile(c), tile(s1), tile(s2)


def _tri(n, fn):
    i = jnp.arange(n)
    return fn(i[:, None], i[None, :]).astype(BF16)


def _inproj_kernel(*refs, n_res, col_chunk):
    lnw_ref, w_ref, o_ref = refs[n_res:]
    x = refs[0][...]
    for r in refs[1:n_res]:
        x = x + r[...]
    yb = _rms(x, lnw_ref[...]).astype(BF16)
    for j in range(PROJ_PAD // col_chunk):
        sl = slice(j * col_chunk, (j + 1) * col_chunk)
        o_ref[:, sl] = _dot(yb, w_ref[:, sl])


def _inproj(res, lnw, w_pad):
    t = res[0].shape[0]
    tm = 512
    return pl.pallas_call(
        functools.partial(_inproj_kernel, n_res=len(res), col_chunk=512),
        grid=(t // tm,),
        in_specs=[pl.BlockSpec((tm, D_MODEL), lambda i: (i, 0))] * len(res) + [
            pl.BlockSpec((1, D_MODEL), lambda i: (0, 0)),
            pl.BlockSpec((D_MODEL, PROJ_PAD), lambda i: (0, 0)),
        ],
        out_specs=pl.BlockSpec((tm, PROJ_PAD), lambda i: (i, 0)),
        out_shape=jax.ShapeDtypeStruct((t, PROJ_PAD), F32),
        compiler_params=_cparams(("parallel",)),
    )(*res, lnw, w_pad)


def _pad_cols(w):
    sizes = (256, 256, 256, 256, 256, 128, 32, 256, 768, 8, 256, 256, 256)
    pts, acc = [], 0
    for sz in sizes:
        pts.append((acc, acc + sz))
        acc += sz
    seg = lambda i: w[:, pts[i][0]:pts[i][1]]
    z = lambda n: jnp.zeros((w.shape[0], n), w.dtype)
    cols = [seg(0), seg(1), seg(2), seg(3), seg(4),
            seg(5), seg(6), z(BLK - 128 - 32),
            seg(8), seg(7), seg(9), z(BLK - 8),
            seg(10), seg(11), seg(12)]
    return jnp.concatenate(cols, axis=1)


def _ret_tables():
    pos = jnp.arange(CHUNK, dtype=F32)
    hh = jnp.arange(N_HEADS, dtype=F32)
    lg_f = jnp.log1p(-jnp.exp2(-5.0 - hh))
    lg_b = jnp.log1p(-jnp.exp2(-5.5 - hh))
    diff = pos[:, None] - pos[None, :]
    d_f = jnp.where(diff >= 0, jnp.exp(lg_f[:, None, None] * jnp.maximum(diff, 0.0)), 0.0)
    d_b = jnp.where(diff < 0, jnp.exp(lg_b[:, None, None] * jnp.maximum(-diff, 0.0)), 0.0)
    dmat = (d_f + d_b).reshape(2, 2, CHUNK, CHUNK)

    def lanes(per_head):
        t = jnp.repeat(per_head[:, :, None], HEAD_DIM, axis=2)
        return t.reshape(2, 2, CHUNK, HEAD_DIM).transpose(0, 2, 1, 3).reshape(2, CHUNK, 2 * HEAD_DIM)

    xi_f = lanes(jnp.exp(lg_f[:, None] * (pos + 1.0)))
    zeta_f = lanes(jnp.exp(lg_f[:, None] * (CHUNK - 1.0 - pos)))
    xi_b = lanes(jnp.exp(lg_b[:, None] * (CHUNK - pos)))
    zeta_b = lanes(jnp.exp(lg_b[:, None] * pos))
    blk = (jnp.arange(128)[:, None] // HEAD_DIM) == (jnp.arange(128)[None, :] // HEAD_DIM)

    def cdec(lg):
        g = jnp.exp(lg * CHUNK).reshape(2, 2)
        rows = jnp.repeat(g, HEAD_DIM, axis=1)
        return jnp.where(blk[None], rows[:, :, None], 0.0)

    avg = jnp.where(blk, 1.0 / HEAD_DIM, 0.0).astype(BF16)
    return dmat, xi_f, zeta_f, xi_b, zeta_b, cdec(lg_f), cdec(lg_b), avg


def _ret_kernel(q_ref, k_ref, v_ref, g_ref, c_ref, s1_ref, s2_ref, dmat_ref, xif_ref, zf_ref,
                xib_ref, zb_ref, cdf_ref, cdb_ref, avg_ref, o_ref, sf_ref, sb_ref, pb_ref, *, nsteps, g):
    t = pl.program_id(1)

    @pl.when(t == 0)
    def _():
        sf_ref[...] = jnp.zeros_like(sf_ref)
        sb_ref[...] = jnp.zeros_like(sb_ref)

    def rotated(ref, rows):
        return _rope(ref[rows, :], c_ref[rows, :], s1_ref[rows, :], s2_ref[rows, :], HEAD_DIM // 2)

    def state_update(s_ref, p, k, vb, zeta, cdec):
        kz = (k * zeta).T.astype(BF16)
        blockmask = jnp.where(cdec > 0.0, 1.0, 0.0)
        s_ref[p] = s_ref[p] * cdec + _dot(kz, vb) * blockmask

    @pl.when(t < nsteps)
    def _():
        base = (nsteps - 1 - t) * g
        for ci in reversed(range(g)):
            rows = slice(ci * CHUNK, (ci + 1) * CHUNK)
            k = rotated(k_ref, rows)
            vb = v_ref[rows, :].astype(BF16)
            for p in range(2):
                sl = slice(p * 128, (p + 1) * 128)
                pb_ref[base + ci, p] = sb_ref[p].astype(BF16)
                state_update(sb_ref, p, k[:, sl], vb[:, sl], zb_ref[p], cdb_ref[p])

    @pl.when(t >= nsteps)
    def _():
        base = (t - nsteps) * g
        head = _lane_head((CHUNK, 128), HEAD_DIM)
        avg = avg_ref[...]

        def group_mean(x):
            hi, lo = _split2(x)
            return _dot(hi, avg) + _dot(lo, avg)

        for ci in range(g):
            rows = slice(ci * CHUNK, (ci + 1) * CHUNK)
            k_all = rotated(k_ref, rows)
            q_all = rotated(q_ref, rows) * (HEAD_DIM ** -0.5)
            vb_all = v_ref[rows, :].astype(BF16)
            for p in range(2):
                sl = slice(p * 128, (p + 1) * 128)
                q, k, vb = q_all[:, sl], k_all[:, sl], vb_all[:, sl]
                kb = k.astype(BF16)
                inner = None
                for hh in range(2):
                    qh = jnp.where(head == hh, q, 0.0).astype(BF16)
                    pm = (_dot_nt(qh, kb) * dmat_ref[p, hh]).astype(BF16)
                    oh = _dot(pm, vb)
                    inner = oh if inner is None else jnp.where(head == hh, oh, inner)
                cross_f = _dot((q * xif_ref[p]).astype(BF16), sf_ref[p].astype(BF16))
                cross_b = _dot((q * xib_ref[p]).astype(BF16), pb_ref[base + ci, p])
                o = inner + cross_f + cross_b
                mu = group_mean(o)
                d = o - mu
                var = group_mean(d * d)
                o_ref[rows, sl] = _silu(g_ref[rows, sl]) * (d * lax.rsqrt(var + GN_EPS))
                state_update(sf_ref, p, k, vb, zf_ref[p], cdf_ref[p])


def _retention(proj, tables, rope_tabs):
    b, s, _ = proj.shape
    nc = s // CHUNK
    g = math.gcd(nc, 4)
    nsteps = nc // g
    rows = g * CHUNK
    both = lambda t: jnp.where(t < nsteps, nsteps - 1 - t, t - nsteps)
    late = lambda t: jnp.where(t < nsteps, 0, t - nsteps)
    col = lambda base, idx: pl.BlockSpec((None, rows, BLK), lambda bi, t: (bi, idx(t), base))
    tab = pl.BlockSpec((rows, BLK), lambda bi, t: (both(t), 0))
    pair = pl.BlockSpec((2, CHUNK, 128), lambda bi, t: (0, 0, 0))
    return pl.pallas_call(
        functools.partial(_ret_kernel, nsteps=nsteps, g=g),
        grid=(b, 2 * nsteps),
        in_specs=[
            col(COL_RQ, late), col(COL_RK, both), col(COL_RV, both), col(COL_RG, late),
            tab, tab, tab,
            pl.BlockSpec((2, 2, CHUNK, CHUNK), lambda bi, t: (0, 0, 0, 0)),
            pair, pair, pair, pair, pair, pair,
            pl.BlockSpec((128, 128), lambda bi, t: (0, 0)),
        ],
        out_specs=pl.BlockSpec((None, rows, BLK), lambda bi, t: (bi, late(t), 0)),
        out_shape=jax.ShapeDtypeStruct((b, s, BLK), F32),
        scratch_shapes=[pltpu.VMEM((2, 128, 128), F32), pltpu.VMEM((2, 128, 128), F32),
                        pltpu.VMEM((nc, 2, 128, 128), BF16)],
        compiler_params=_cparams(("parallel", "arbitrary")),
    )(proj, proj, proj, proj, *rope_tabs, *tables)


def _mla_prep_kernel(cq_ref, kv_ref, qnw_ref, kvnw_ref, wq_ref, wk_ref, wv_ref, place_ref,
                     qc_ref, qs1_ref, qs2_ref, kc_ref, ks1_ref, ks2_ref, q_ref, k_ref, v_ref):
    cqn = _rms(cq_ref[...], qnw_ref[...])
    q = _dot(cqn.astype(BF16), wq_ref[...])
    q = _rope(q, qc_ref[...], qs1_ref[...], qs2_ref[...], MLA_ROPE // 2)
    q_ref[...] = (q * ((MLA_NOPE + MLA_ROPE) ** -0.5 * math.log2(math.e))).astype(BF16)
    blk = kv_ref[...]
    ckvn = _rms(blk[:, :MLA_KV_RANK], kvnw_ref[...]).astype(BF16)
    kr = _rope(blk[:, MLA_KV_RANK:], kc_ref[...], ks1_ref[...], ks2_ref[...], MLA_ROPE // 2)
    k_ref[...] = (_dot(ckvn, wk_ref[...]) + _dot(kr.astype(BF16), place_ref[...])).astype(BF16)
    v = _dot(ckvn, wv_ref[...])
    ones = (lax.broadcasted_iota(jnp.int32, v.shape, 1) % 128) >= HEAD_DIM
    v_ref[...] = jnp.where(ones, 1.0, v).astype(BF16)


def _mla_weights(w_uq, w_ukv):
    qh = w_uq.reshape(MLA_Q_RANK, N_HEADS, MLA_NOPE + MLA_ROPE)
    wq = jnp.concatenate([qh, jnp.zeros((MLA_Q_RANK, N_HEADS, 32), F32)], axis=-1).reshape(MLA_Q_RANK, 512)
    kvh = w_ukv.reshape(MLA_KV_RANK, N_HEADS, MLA_NOPE + HEAD_DIM)
    zk = jnp.zeros((MLA_KV_RANK, N_HEADS, 64), F32)
    wk = jnp.concatenate([kvh[..., :MLA_NOPE], zk], axis=-1).reshape(MLA_KV_RANK, 512)
    wv = jnp.concatenate([kvh[..., MLA_NOPE:], zk], axis=-1).reshape(MLA_KV_RANK, 512)
    src = jnp.arange(128)[:, None]
    dst = jnp.arange(512)[None, :]
    place = ((dst % 128 == src + MLA_NOPE) & (src < MLA_ROPE)).astype(BF16)
    return wq.astype(BF16), wk.astype(BF16), wv.astype(BF16), place


def _mla_flash_kernel(q_ref, k_ref, v_ref, o_ref, m_ref, acc_ref, *, strip):
    ki = pl.program_id(2)

    @pl.when(ki == 0)
    def _():
        m_ref[...] = jnp.full_like(m_ref, -jnp.inf)
        acc_ref[...] = jnp.zeros_like(acc_ref)

    for h in range(N_HEADS):
        sl = slice(h * 128, (h + 1) * 128)
        k = k_ref[:, sl]
        v = v_ref[:, sl]
        for r in range(q_ref.shape[0] // strip):
            rows = slice(r * strip, (r + 1) * strip)
            s = _dot_nt(q_ref[rows, sl], k)
            m_prev = m_ref[h, rows]
            m_new = jnp.maximum(m_prev, jnp.max(s, axis=-1, keepdims=True))
            p = jnp.exp2(s - m_new[:, :1])
            acc_ref[h, rows] = jnp.exp2(m_prev - m_new) * acc_ref[h, rows] + _dot(p.astype(BF16), v)
            m_ref[h, rows] = m_new

    @pl.when(ki == pl.num_programs(2) - 1)
    def _():
        outs = []
        for h in range(N_HEADS):
            a = acc_ref[h]
            outs.append(a[:, :HEAD_DIM] / a[:, HEAD_DIM:])
        o_ref[...] = jnp.concatenate(outs, axis=-1)


def _mla(proj, q_norm_w, kv_norm_w, w_uq, w_ukv, q_tabs, k_tabs):
    b, s, _ = proj.shape
    wq, wk, wv, place = _mla_weights(w_uq, w_ukv)
    ts = 512
    row = lambda n: pl.BlockSpec((ts, n), lambda bi, i: (i, 0))
    full = lambda r, n: pl.BlockSpec((r, n), lambda bi, i: (0, 0))
    out = pl.BlockSpec((None, ts, 512), lambda bi, i: (bi, i, 0))
    q, k, v = pl.pallas_call(
        _mla_prep_kernel,
        grid=(b, s // ts),
        in_specs=[
            pl.BlockSpec((None, ts, BLK), lambda bi, i: (bi, i, COL_MCQ)),
            pl.BlockSpec((None, ts, BLK), lambda bi, i: (bi, i, COL_MKV)),
            full(1, MLA_Q_RANK), full(1, MLA_KV_RANK),
            full(MLA_Q_RANK, 512), full(MLA_KV_RANK, 512), full(MLA_KV_RANK, 512), full(128, 512),
            row(512), row(512), row(512), row(128), row(128), row(128),
        ],
        out_specs=[out, out, out],
        out_shape=[jax.ShapeDtypeStruct((b, s, 512), BF16)] * 3,
        compiler_params=_cparams(("parallel", "parallel")),
    )(proj, proj, q_norm_w[None], kv_norm_w[None], wq, wk, wv, place, *q_tabs, *k_tabs)

    tq, tk = math.gcd(s, 2048), math.gcd(s, 512)
    return pl.pallas_call(
        functools.partial(_mla_flash_kernel, strip=tq),
        grid=(b, s // tq, s // tk),
        in_specs=[
            pl.BlockSpec((None, tq, 512), lambda bi, qi, ki: (bi, qi, 0)),
            pl.BlockSpec((None, tk, 512), lambda bi, qi, ki: (bi, ki, 0)),
            pl.BlockSpec((None, tk, 512), lambda bi, qi, ki: (bi, ki, 0)),
        ],
        out_specs=pl.BlockSpec((None, tq, N_HEADS * HEAD_DIM), lambda bi, qi, ki: (bi, qi, 0)),
        out_shape=jax.ShapeDtypeStruct((b, s, N_HEADS * HEAD_DIM), F32),
        scratch_shapes=[pltpu.VMEM((N_HEADS, tq, 128), F32), pltpu.VMEM((N_HEADS, tq, 128), F32)],
        compiler_params=_cparams(("parallel", "parallel", "arbitrary")),
    )(q, k, v)


def _conv_kernel(prev_ref, cur_ref, next_ref, w_ref, b_ref, o_ref, *, ts):
    i = pl.program_id(1)
    w = w_ref[...]

    def conv(x):
        n = x.shape[0]
        acc = x * w[2:3]
        for s in (-2, -1, 1, 2):
            acc = acc + pltpu.roll(x, (-s) % n, 0) * w[s + 2:s + 3]
        return acc

    act = lambda y: _silu(y + b_ref[...])
    cur = cur_ref[...]
    prev = prev_ref[...] * (i > 0).astype(F32)
    nxt = next_ref[...] * (i < pl.num_programs(1) - 1).astype(F32)
    o_ref[...] = act(conv(cur))
    top = conv(jnp.concatenate([prev, cur[:16]], axis=0))
    o_ref[0:8, :] = act(top[8:16])
    bot = conv(jnp.concatenate([cur[ts - 16:], nxt], axis=0))
    o_ref[ts - 8:ts, :] = act(bot[8:16])


def _ssd_conv(proj, conv_w, conv_b):
    b, s, _ = proj.shape
    ts = 512
    width = 3 * BLK
    cb = COL_XBC * BLK // width
    nb8 = ts // 8
    w8 = jnp.concatenate([conv_w, jnp.zeros((8 - SSM_CONV, width), F32)], axis=0)
    return pl.pallas_call(
        functools.partial(_conv_kernel, ts=ts),
        grid=(b, s // ts),
        in_specs=[
            pl.BlockSpec((None, 8, width), lambda bi, i: (bi, jnp.maximum(i * nb8 - 1, 0), cb)),
            pl.BlockSpec((None, ts, width), lambda bi, i: (bi, i, cb)),
            pl.BlockSpec((None, 8, width), lambda bi, i: (bi, jnp.minimum((i + 1) * nb8, s // 8 - 1), cb)),
            pl.BlockSpec((8, width), lambda bi, i: (0, 0)),
            pl.BlockSpec((1, width), lambda bi, i: (0, 0)),
        ],
        out_specs=pl.BlockSpec((None, ts, width), lambda bi, i: (bi, i, 0)),
        out_shape=jax.ShapeDtypeStruct((b, s, width), F32),
        compiler_params=_cparams(("parallel", "parallel")),
    )(proj, proj, proj, w8, conv_b[None])


def _ssd_tables():
    tril = _tri(CHUNK, lambda i, j: j <= i)
    triu = _tri(CHUNK, lambda i, j: j >= i)
    h = jnp.arange(128)[:, None]
    lane = jnp.arange(256)[None, :]
    e_f = ((lane // HEAD_DIM == h) & (h < N_HEADS)).astype(BF16)
    e_b = ((lane // HEAD_DIM == h - N_HEADS) & (h >= N_HEADS) & (h < 2 * N_HEADS)).astype(BF16)
    return tril, triu, e_f, e_b


def _ssd_kernel(xs_ref, bm_ref, cm_ref, z_ref, dt_ref, dtt_ref, bias_ref, a_ref, biasc_ref, ac_ref,
                dskip_ref, nw_ref, tril_ref, triu_ref, ef_ref, eb_ref, o_ref,
                sf_ref, sb_ref, pb_ref, *, nsteps, g):
    t = pl.program_id(1)
    tril, triu = tril_ref[...], triu_ref[...]

    @pl.when(t == 0)
    def _():
        sf_ref[...] = jnp.zeros_like(sf_ref)
        sb_ref[...] = jnp.zeros_like(sb_ref)

    def step_sizes(rows):
        dt = _softplus(dt_ref[rows, :128] + bias_ref[...])
        return dt, dt * a_ref[...]

    def backward_terms(xs, dt, dta):
        rcs_b = _dot_x_exact(_dot_exact_x(triu, dta), eb_ref[...])
        return rcs_b, rcs_b[0:1, :], xs * _dot_x_exact(dt, eb_ref[...])

    def state_update(s_ref, bmf, weighted_x, total):
        for gi in range(SSM_GROUPS):
            sl = slice(gi * 128, (gi + 1) * 128)
            upd = _dot(bmf[:, sl].T.astype(BF16), weighted_x[:, sl].astype(BF16))
            s_ref[gi] = s_ref[gi] * jnp.exp(total[:, sl]) + upd

    @pl.when(t < nsteps)
    def _():
        base = (nsteps - 1 - t) * g
        for ci in reversed(range(g)):
            rows = slice(ci * CHUNK, (ci + 1) * CHUNK)
            xs = xs_ref[rows, :]
            dt, dta = step_sizes(rows)
            rcs_b, tot_b, xdt_b = backward_terms(xs, dt, dta)
            pb_ref[base + ci] = sb_ref[...].astype(BF16)
            state_update(sb_ref, bm_ref[rows, :], jnp.exp(tot_b - rcs_b) * xdt_b, tot_b)

    @pl.when(t >= nsteps)
    def _():
        base = (t - nsteps) * g
        row = lax.broadcasted_iota(jnp.int32, (CHUNK, CHUNK), 0)
        colm = lax.broadcasted_iota(jnp.int32, (CHUNK, CHUNK), 1)
        head = _lane_head((CHUNK, 128), HEAD_DIM)
        efb = jnp.concatenate([ef_ref[...], eb_ref[...]], axis=1)
        for ci in range(g):
            rows = slice(ci * CHUNK, (ci + 1) * CHUNK)
            xs = xs_ref[rows, :]
            bmf = bm_ref[rows, :]
            bm = bmf.astype(BF16)
            cm = cm_ref[rows, :].astype(BF16)
            dt, dta = step_sizes(rows)
            rcs_b = _dot_x_exact(_dot_exact_x(triu, dta), eb_ref[...])
            cs_f = _dot_x_exact(_dot_exact_x(tril, dta), ef_ref[...])
            tot_f = cs_f[CHUNK - 1:CHUNK, :]
            dt_exp = _dot_x_exact(dt, efb)
            xdt_f = xs * dt_exp[:, :SSM_INNER]
            xdt_b = xs * dt_exp[:, SSM_INNER:]
            dtat = _softplus(dtt_ref[:, rows] + biasc_ref[...]) * ac_ref[...]
            cst = _dot_x_exact(dtat, triu)
            rcst = _dot_x_exact(dtat, tril)
            pb = pb_ref[base + ci]
            ys = []
            for gi in range(SSM_GROUPS):
                sl = slice(gi * 128, (gi + 1) * 128)
                cb = _dot_nt(cm[:, sl], bm[:, sl])
                xcat = jnp.concatenate([xdt_f[:, sl], xdt_b[:, sl]], axis=0).astype(BF16)
                yg = None
                for hh in range(2):
                    h = 2 * gi + hh
                    c0 = h * HEAD_DIM
                    seg_f = cs_f[:, c0:c0 + 1] - cst[h:h + 1, :]
                    seg_b = rcs_b[:, c0:c0 + 1] - rcst[N_HEADS + h:N_HEADS + h + 1, :]
                    dec_f = jnp.where(row >= colm, jnp.exp(jnp.minimum(seg_f, 0.0)), 0.0)
                    dec_b = jnp.where(row < colm, jnp.exp(jnp.minimum(seg_b, 0.0)), 0.0)
                    wcat = jnp.concatenate([cb * dec_f, cb * dec_b], axis=1).astype(BF16)
                    yh = _dot(wcat, xcat)
                    yg = yh if yg is None else jnp.where(head == hh, yh, yg)
                off_f = jnp.exp(cs_f[:, sl]) * _dot(cm[:, sl], sf_ref[gi].astype(BF16))
                off_b = jnp.exp(rcs_b[:, sl]) * _dot(cm[:, sl], pb[gi])
                ys.append(yg + off_f + off_b)
            y = jnp.concatenate(ys, axis=1) + dskip_ref[...] * xs
            y = y * _silu(z_ref[rows, :])
            o_ref[rows, :] = _rms(y, nw_ref[...])
            state_update(sf_ref, bmf, jnp.exp(tot_f - cs_f) * xdt_f, tot_f)


def _ssd(proj, xbc, a_log, dt_bias, d_skip, norm_w, tables):
    b, s, _ = proj.shape
    nc = s // CHUNK
    g = math.gcd(nc, 4)
    nsteps = nc // g
    rows = g * CHUNK
    dt_t = jnp.swapaxes(proj[:, :, COL_DT * BLK:COL_DT * BLK + 8], 1, 2)
    pad_row = lambda v: jnp.concatenate([v.reshape(1, 8), jnp.zeros((1, 120), F32)], axis=1)
    a = -jnp.exp(a_log.astype(F32))
    both = lambda t: jnp.where(t < nsteps, nsteps - 1 - t, t - nsteps)
    late = lambda t: jnp.where(t < nsteps, 0, t - nsteps)
    full = lambda r, n: pl.BlockSpec((r, n), lambda bi, t: (0, 0))
    blk = lambda idx, c: pl.BlockSpec((None, rows, BLK), lambda bi, t: (bi, idx(t), c))
    return pl.pallas_call(
        functools.partial(_ssd_kernel, nsteps=nsteps, g=g),
        grid=(b, 2 * nsteps),
        in_specs=[
            blk(both, 0), blk(both, 1), blk(late, 2), blk(late, COL_Z), blk(both, COL_DT),
            pl.BlockSpec((None, 8, rows), lambda bi, t: (bi, 0, late(t))),
            full(1, 128), full(1, 128), full(8, 1), full(8, 1),
            full(1, SSM_INNER), full(1, SSM_INNER),
            full(CHUNK, CHUNK), full(CHUNK, CHUNK), full(128, 256), full(128, 256),
        ],
        out_specs=blk(late, 0),
        out_shape=jax.ShapeDtypeStruct((b, s, SSM_INNER), F32),
        scratch_shapes=[pltpu.VMEM((SSM_GROUPS, 128, 128), F32), pltpu.VMEM((SSM_GROUPS, 128, 128), F32),
                        pltpu.VMEM((nc, SSM_GROUPS, 128, 128), BF16)],
        compiler_params=_cparams(("parallel", "arbitrary")),
    )(xbc, xbc, xbc, proj, proj, dt_t, pad_row(dt_bias), pad_row(a), dt_bias.reshape(8, 1), a.reshape(8, 1),
      jnp.repeat(d_skip, HEAD_DIM)[None], norm_w[None], *tables)


def _dil_prep_kernel(q_ref, k_ref, c_ref, s1_ref, s2_ref, qo_ref, ko_ref):
    c, s1, s2 = c_ref[...], s1_ref[...], s2_ref[...]
    qo_ref[...] = _rope(q_ref[...], c, s1, s2, ROPE_DIM // 2) * (HEAD_DIM ** -0.5)
    ko_ref[...] = _rope(k_ref[...], c, s1, s2, ROPE_DIM // 2)


def _dil_kernel(q_ref, k_ref, v_ref, o_ref, m_ref, l_ref, *, s):
    head = _lane_head((128, 128), HEAD_DIM)
    kw = 2 * 128
    for pi, (win, d) in enumerate(DIL_PATTERNS):
        half = win // (2 * d)
        seg = s // d
        per_seg = seg // 128

        def body(i, carry, d=d, pi=pi, half=half, seg=seg, per_seg=per_seg):
            r = i // per_seg
            m0 = (i % per_seg) * 128
            ks = jnp.clip(m0 - half, 0, seg - kw)
            if d == 1:
                qrows = pl.ds(pl.multiple_of(m0, 128), 128)
                krows = pl.ds(pl.multiple_of(ks, 64), kw)
            else:
                qrows = pl.ds(r + d * m0, 128, stride=d)
                krows = pl.ds(r + d * ks, kw, stride=d)
            q = q_ref[qrows, :]
            kb = k_ref[krows, :].astype(BF16)
            vb = v_ref[krows, :].astype(BF16)
            qpos = m0 + lax.broadcasted_iota(jnp.int32, (128, kw), 0)
            kpos = ks + lax.broadcasted_iota(jnp.int32, (128, kw), 1)
            valid = jnp.abs(qpos - kpos) <= half
            mb = lb = pv = None
            for hh in range(2):
                sc = _dot_nt(jnp.where(head == hh, q, 0.0).astype(BF16), kb)
                sc = jnp.where(valid, sc, -jnp.inf)
                mh = jnp.max(sc, axis=-1, keepdims=True)
                p = jnp.exp(sc - mh)
                lh = jnp.sum(p, axis=-1, keepdims=True)
                ph = _dot(p.astype(BF16), vb)
                if hh == 0:
                    mb = jnp.broadcast_to(mh, (128, 128))
                    lb = jnp.broadcast_to(lh, (128, 128))
                    pv = ph
                else:
                    mb = jnp.where(head == hh, mh, mb)
                    lb = jnp.where(head == hh, lh, lb)
                    pv = jnp.where(head == hh, ph, pv)
            if pi == 0:
                m_ref[qrows, :] = mb
                l_ref[qrows, :] = lb
                o_ref[qrows, :] = pv
            else:
                mo = m_ref[qrows, :]
                mn = jnp.maximum(mo, mb)
                wa = jnp.exp(mo - mn)
                wb = jnp.exp(mb - mn)
                m_ref[qrows, :] = mn
                l_ref[qrows, :] = wa * l_ref[qrows, :] + wb * lb
                o_ref[qrows, :] = wa * o_ref[qrows, :] + wb * pv
            return carry

        lax.fori_loop(0, s // 128, body, 0, unroll=4)
    o_ref[...] = o_ref[...] / l_ref[...]


def _dilated(proj, tabs):
    b, s, _ = proj.shape
    assert all(s // d >= 256 for _, d in DIL_PATTERNS)
    ts = 512
    rowt = pl.BlockSpec((ts, BLK), lambda bi, i: (i, 0))
    outb = pl.BlockSpec((None, ts, BLK), lambda bi, i: (bi, i, 0))
    qr, kr = pl.pallas_call(
        _dil_prep_kernel,
        grid=(b, s // ts),
        in_specs=[
            pl.BlockSpec((None, ts, BLK), lambda bi, i: (bi, i, COL_DQ)),
            pl.BlockSpec((None, ts, BLK), lambda bi, i: (bi, i, COL_DK)),
            rowt, rowt, rowt,
        ],
        out_specs=[outb, outb],
        out_shape=[jax.ShapeDtypeStruct((b, s, BLK), F32)] * 2,
        compiler_params=_cparams(("parallel", "parallel")),
    )(proj, proj, *tabs)
    seqb = lambda c: pl.BlockSpec((None, s, 128), lambda bi, p: (bi, 0, 2 * c + p))
    return pl.pallas_call(
        functools.partial(_dil_kernel, s=s),
        grid=(b, 2),
        in_specs=[seqb(0), seqb(0), seqb(COL_DV)],
        out_specs=seqb(0),
        out_shape=jax.ShapeDtypeStruct((b, s, BLK), F32),
        scratch_shapes=[pltpu.VMEM((s, 128), F32), pltpu.VMEM((s, 128), F32)],
        compiler_params=_cparams(("parallel", "parallel")),
    )(qr, kr, proj)


def _outproj_kernel(*refs, n_res):
    ya_ref, yb_ref, yc_ref, yd_ref, w_ref, lnw_ref, rw_ref, xo_ref, h_ref, aff_ref = refs[n_res:]
    acc = refs[0][...]
    for r in refs[1:n_res]:
        acc = acc + r[...]
    for i, y_ref in enumerate((ya_ref, yb_ref, yc_ref, yd_ref)):
        acc = acc + _dot(y_ref[...].astype(BF16), w_ref[i * BLK:(i + 1) * BLK, :])
    xo_ref[...] = acc
    h = _rms(acc, lnw_ref[...])
    h_hi = h.astype(BF16)
    h_ref[...] = h_hi
    h_lo = (h - h_hi.astype(F32)).astype(BF16)
    rw = rw_ref[...]
    w_hi = rw.astype(BF16)
    w_lo = (rw - w_hi.astype(F32)).astype(BF16)
    logits = _dot(h_hi, w_hi) + _dot(h_hi, w_lo) + _dot(h_lo, w_hi) + _dot(h_lo, w_lo)
    lane = lax.broadcasted_iota(jnp.int32, logits.shape, 1)
    logits = jnp.where(lane < N_EXPERTS, logits, -jnp.inf)
    e = jnp.exp(logits - jnp.max(logits, axis=-1, keepdims=True))
    aff = e / jnp.sum(e, axis=-1, keepdims=True)
    aff_ref[...] = aff.T[:N_EXPERTS, :]


def _outproj(res, ys, w_out, ln2_w, router_w):
    t = res[0].shape[0]
    tm = 256
    rw = jnp.concatenate([router_w, jnp.zeros((D_MODEL, 128 - N_EXPERTS), F32)], axis=1)
    row = lambda n: pl.BlockSpec((tm, n), lambda i: (i, 0))
    full = lambda r, n: pl.BlockSpec((r, n), lambda i: (0, 0))
    return pl.pallas_call(
        functools.partial(_outproj_kernel, n_res=len(res)),
        grid=(t // tm,),
        in_specs=[row(D_MODEL)] * len(res) + [row(BLK)] * 4 + [
            full(D_MODEL, D_MODEL), full(1, D_MODEL), full(D_MODEL, 128)],
        out_specs=[row(D_MODEL), row(D_MODEL), pl.BlockSpec((N_EXPERTS, tm), lambda i: (0, i))],
        out_shape=[jax.ShapeDtypeStruct((t, D_MODEL), F32),
                   jax.ShapeDtypeStruct((t, D_MODEL), BF16),
                   jax.ShapeDtypeStruct((N_EXPERTS, t), F32)],
        compiler_params=_cparams(("parallel",)),
    )(*res, *ys, w_out.astype(BF16), ln2_w[None], rw)


def _exclusive_prefix(x, tri):
    n = x.shape[1] // 128
    off = jnp.zeros((x.shape[0], 1), F32)
    parts = []
    for i in range(n):
        xt = x[:, i * 128:(i + 1) * 128]
        incl = _dot(xt.astype(BF16), tri)
        parts.append(incl - xt + off)
        off = off + incl[:, 127:128]
    return jnp.concatenate(parts, axis=1)


def _select_kernel(aff_ref, tri_ref, idx_ref, gate_ref, key_ref, lhs_ref, *, cap):
    a = aff_ref[...]
    n_exp, s = a.shape
    thr_bits = jnp.zeros((n_exp, 1), jnp.int32)
    for bit in range(30, -1, -1):
        cand = thr_bits | (1 << bit)
        cnt = jnp.sum(jnp.where(a >= lax.bitcast_convert_type(cand, F32), 1.0, 0.0), axis=1, keepdims=True)
        thr_bits = jnp.where(cnt >= cap, cand, thr_bits)
    thr = lax.bitcast_convert_type(thr_bits, F32)
    above = jnp.where(a > thr, 1.0, 0.0)
    tie = jnp.where(a == thr, 1.0, 0.0)
    need = cap - jnp.sum(above, axis=1, keepdims=True)
    tri = tri_ref[...]
    sel = above + tie * jnp.where(_exclusive_prefix(tie, tri) < need, 1.0, 0.0)
    key_ref[...] = jnp.where(sel > 0.0, _exclusive_prefix(sel, tri), -1.0)

    tok = lax.broadcasted_iota(jnp.int32, (1, s), 1)
    lhs_ref[...] = jnp.zeros_like(lhs_ref)
    lhs_ref[0:1, :] = (tok // 64).astype(F32)
    lhs_ref[1:2, :] = (tok % 64).astype(F32)
    slot = lax.broadcasted_iota(jnp.int32, (128, s), 0).astype(F32)

    def per_expert(e, carry):
        key = key_ref[pl.ds(e, 1), :]
        g = aff_ref[pl.ds(e, 1), :]
        g_hi = g.astype(BF16).astype(F32)
        g_mid = (g - g_hi).astype(BF16).astype(F32)
        lhs_ref[2:3, :] = g_hi
        lhs_ref[3:4, :] = g_mid
        lhs_ref[4:5, :] = (g - g_hi - g_mid).astype(BF16).astype(F32)
        lhs = lhs_ref[...].astype(BF16)
        ids, gates = [], []
        for j in range(cap // 128):
            onehot = jnp.where(slot == key - (128.0 * j), 1.0, 0.0).astype(BF16)
            r = _dot_nt(lhs, onehot)
            ids.append((64.0 * r[0:1] + r[1:2]).astype(jnp.int32))
            gates.append(r[2:3] + r[3:4] + r[4:5])
        idx_ref[pl.ds(e, 1), :] = jnp.concatenate(ids, axis=1)
        gate_ref[pl.ds(e, 1), :] = jnp.concatenate(gates, axis=1)
        return carry

    lax.fori_loop(0, n_exp, per_expert, 0)


def _moe_select(aff_t, b, s, cap):
    tri = _tri(128, lambda i, j: i <= j)
    out = pl.BlockSpec((None, N_EXPERTS, cap), lambda bi: (bi, 0, 0))
    return pl.pallas_call(
        functools.partial(_select_kernel, cap=cap),
        grid=(b,),
        in_specs=[pl.BlockSpec((N_EXPERTS, s), lambda bi: (0, bi)),
                  pl.BlockSpec((128, 128), lambda bi: (0, 0))],
        out_specs=[out, out],
        out_shape=[jax.ShapeDtypeStruct((b, N_EXPERTS, cap), jnp.int32),
                   jax.ShapeDtypeStruct((b, N_EXPERTS, cap), F32)],
        scratch_shapes=[pltpu.VMEM((N_EXPERTS, s), F32), pltpu.VMEM((8, s), F32)],
        compiler_params=_cparams(("parallel",)),
    )(aff_t, tri)


ROWS_PER_ITER = 8


def _gather_kernel(idx_ref, h_ref, o_ref, hf_ref, buf_ref, *, cap, n_exp):
    e = pl.program_id(1)
    base = (pl.program_id(0) * n_exp + e) * cap

    @pl.when(e == 0)
    def _():
        hf_ref[...] = h_ref[...].astype(F32)

    def body(jb, carry):
        j0 = jb * ROWS_PER_ITER
        vals = [hf_ref[pl.ds(idx_ref[base + j0 + u], 1), :] for u in range(ROWS_PER_ITER)]
        for u in range(ROWS_PER_ITER):
            buf_ref[pl.ds(j0 + u, 1), :] = vals[u]
        return carry

    lax.fori_loop(0, cap // ROWS_PER_ITER, body, 0)
    o_ref[...] = buf_ref[...].astype(BF16)


def _moe_gather(idx_flat, h, cap):
    b, s, d = h.shape
    return pl.pallas_call(
        functools.partial(_gather_kernel, cap=cap, n_exp=N_EXPERTS),
        grid_spec=pltpu.PrefetchScalarGridSpec(
            num_scalar_prefetch=1,
            grid=(b, N_EXPERTS),
            in_specs=[pl.BlockSpec((None, s, d), lambda bi, e, idx: (bi, 0, 0))],
            out_specs=pl.BlockSpec((None, None, cap, d), lambda bi, e, idx: (e, bi, 0, 0)),
            scratch_shapes=[pltpu.VMEM((s, d), F32), pltpu.VMEM((cap, d), F32)],
        ),
        out_shape=jax.ShapeDtypeStruct((N_EXPERTS, b, cap, d), BF16),
        compiler_params=_cparams(("parallel", "arbitrary")),
    )(idx_flat, h)


def _ffn_kernel(x_ref, wg_ref, wu_ref, wd_ref, o_ref, *, rows, tr):
    f = pl.program_id(1)
    wg = wg_ref[...].astype(BF16)
    wu = wu_ref[...].astype(BF16)
    wd = wd_ref[...].astype(BF16)

    def partial_out(sl):
        x = x_ref[sl, :]
        hid = (_silu(_dot(x, wg)) * _dot(x, wu)).astype(BF16)
        return _dot(hid, wd)

    for r in range(rows // tr):
        sl = slice(r * tr, (r + 1) * tr)

        @pl.when(f == 0)
        def _():
            o_ref[sl, :] = partial_out(sl)

        @pl.when(f > 0)
        def _():
            o_ref[sl, :] += partial_out(sl)


def _moe_ffn(xin, w_gate, w_up, w_down, layer):
    n_exp, rows, d = xin.shape
    tf = 512
    return pl.pallas_call(
        functools.partial(_ffn_kernel, rows=rows, tr=math.gcd(rows, 1024)),
        grid=(n_exp, EXPERT_FF // tf),
        in_specs=[
            pl.BlockSpec((None, rows, d), lambda e, f: (e, 0, 0)),
            pl.BlockSpec((None, None, d, tf), lambda e, f: (layer, e, 0, f)),
            pl.BlockSpec((None, None, d, tf), lambda e, f: (layer, e, 0, f)),
            pl.BlockSpec((None, None, tf, d), lambda e, f: (layer, e, f, 0)),
        ],
        out_specs=pl.BlockSpec((None, rows, d), lambda e, f: (e, 0, 0)),
        out_shape=jax.ShapeDtypeStruct((n_exp, rows, d), F32),
        compiler_params=_cparams(("parallel", "arbitrary")),
    )(xin, w_gate, w_up, w_down)


def _combine_kernel(idx_ref, gate_ref, y_ref, o_ref, *, cap, n_exp):
    e = pl.program_id(1)
    base = (pl.program_id(0) * n_exp + e) * cap

    @pl.when(e == 0)
    def _():
        o_ref[...] = jnp.zeros_like(o_ref)

    def body(jb, carry):
        j0 = jb * ROWS_PER_ITER
        rows = [idx_ref[base + j0 + u] for u in range(ROWS_PER_ITER)]
        vals = [o_ref[pl.ds(rows[u], 1), :] + gate_ref[base + j0 + u] * y_ref[pl.ds(j0 + u, 1), :]
                for u in range(ROWS_PER_ITER)]
        for u in range(ROWS_PER_ITER):
            o_ref[pl.ds(rows[u], 1), :] = vals[u]
        return carry

    lax.fori_loop(0, cap // ROWS_PER_ITER, body, 0)


def _moe_combine(idx_flat, gate_flat, y, b, s, cap):
    d = y.shape[-1]
    return pl.pallas_call(
        functools.partial(_combine_kernel, cap=cap, n_exp=N_EXPERTS),
        grid_spec=pltpu.PrefetchScalarGridSpec(
            num_scalar_prefetch=2,
            grid=(b, N_EXPERTS),
            in_specs=[pl.BlockSpec((None, None, cap, d), lambda bi, e, idx, gate: (e, bi, 0, 0))],
            out_specs=pl.BlockSpec((None, s, d), lambda bi, e, idx, gate: (bi, 0, 0)),
        ),
        out_shape=jax.ShapeDtypeStruct((b, s, d), F32),
        compiler_params=_cparams(("parallel", "arbitrary")),
    )(idx_flat, gate_flat, y)


def _final_norm_kernel(xa_ref, xb_ref, w_ref, o_ref):
    o_ref[...] = _rms(xa_ref[...] + xb_ref[...], w_ref[...])


def _final_norm(xa, xb, w):
    t = xa.shape[0]
    tm = 1024
    row = pl.BlockSpec((tm, D_MODEL), lambda i: (i, 0))
    return pl.pallas_call(
        _final_norm_kernel,
        grid=(t // tm,),
        in_specs=[row, row, pl.BlockSpec((1, D_MODEL), lambda i: (0, 0))],
        out_specs=row,
        out_shape=jax.ShapeDtypeStruct((t, D_MODEL), F32),
        compiler_params=_cparams(("parallel",)),
    )(xa, xb, w[None])


def _moe(h, aff, w_gate, w_up, w_down, layer, b, s):
    d = h.shape[-1]
    cap = EC_CAPACITY_FACTOR * s // N_EXPERTS
    token_idx, gate = _moe_select(aff, b, s, cap)
    idx_flat = token_idx.reshape(-1)
    xin = _moe_gather(idx_flat, h.reshape(b, s, d), cap)
    y = _moe_ffn(xin.reshape(N_EXPERTS, b * cap, d), w_gate, w_up, w_down, layer)
    return _moe_combine(idx_flat, gate.reshape(-1), y.reshape(N_EXPERTS, b, cap, d), b, s, cap)


def kernel(x, ln1_w, w_in, mla_q_norm_w, mla_kv_norm_w, mla_w_uq, mla_w_ukv, ssm_conv_w, ssm_conv_b,
           ssm_a_log, ssm_dt_bias, ssm_d, ssm_norm_w, w_out, ln2_w, router_w, exp_w_gate, exp_w_up,
           exp_w_down, final_norm_w):
    b, s, d = x.shape
    depth = w_in.shape[0]
    ret_tabs = _ret_tables()
    ret_rope = _rope_tables(s, HEAD_DIM, RET_THETA, HEAD_DIM, 0, N_HEADS)
    mla_q_rope = _rope_tables(s, MLA_ROPE, ROPE_THETA, 128, MLA_NOPE, N_HEADS)
    mla_k_rope = _rope_tables(s, MLA_ROPE, ROPE_THETA, 128, 0, 1)
    dil_rope = _rope_tables(s, ROPE_DIM, ROPE_THETA, HEAD_DIM, 0, N_HEADS)
    ssd_tabs = _ssd_tables()
    res = [x.reshape(b * s, d)]
    for i in range(depth):
        proj = _inproj(res, ln1_w[i][None], _pad_cols(w_in[i]).astype(BF16)).reshape(b, s, PROJ_PAD)
        y_a = _retention(proj, ret_tabs, ret_rope)
        y_b = _mla(proj, mla_q_norm_w[i], mla_kv_norm_w[i], mla_w_uq[i], mla_w_ukv[i], mla_q_rope, mla_k_rope)
        xbc = _ssd_conv(proj, ssm_conv_w[i], ssm_conv_b[i])
        y_c = _ssd(proj, xbc, ssm_a_log[i], ssm_dt_bias[i], ssm_d[i], ssm_norm_w[i], ssd_tabs)
        y_d = _dilated(proj, dil_rope)
        ys = [y.reshape(b * s, BLK) for y in (y_a, y_b, y_c, y_d)]
        x_mid, h, aff = _outproj(res, ys, w_out[i], ln2_w[i], router_w[i])
        moe = _moe(h, aff, exp_w_gate, exp_w_up, exp_w_down, i, b, s)
        res = [x_mid, moe.reshape(b * s, d)]
    return _final_norm(res[0], res[1], final_norm_w).reshape(b, s, d)
```

```python
import functools
import math

import jax
import jax.numpy as jnp
from jax import lax
from jax.experimental import pallas as pl
from jax.experimental.pallas import tpu as pltpu

F32 = jnp.float32
BF16 = jnp.bfloat16

D_MODEL = 1024
RMS_EPS = 1e-6
GN_EPS = 1e-5
CHUNK = 128
HEAD_DIM = 64
N_HEADS = 4

RET_THETA = 10000.0
ROPE_THETA = 500000.0
ROPE_DIM = 16

MLA_Q_RANK = 256
MLA_KV_RANK = 128
MLA_NOPE = 64
MLA_ROPE = 32

SSM_GROUPS = 2
SSM_STATE = 128
SSM_CONV = 5
SSM_INNER = 256

DIL_PATTERNS = ((128, 1), (512, 4), (2048, 16))

N_EXPERTS = 16
EXPERT_FF = 2048
EC_CAPACITY_FACTOR = 2

BLK = 256
COL_RQ, COL_RK, COL_RV, COL_RG = 0, 1, 2, 3
COL_MCQ, COL_MKV = 4, 5
COL_XBC = 6
COL_Z = 9
COL_DT = 10
COL_DQ, COL_DK, COL_DV = 11, 12, 13
PROJ_PAD = 14 * BLK

VMEM_LIMIT = 56 * 1024 * 1024


def _cparams(sem):
    return pltpu.CompilerParams(dimension_semantics=sem, vmem_limit_bytes=VMEM_LIMIT)


def _split2(x):
    hi = x.astype(BF16)
    lo = (x - hi.astype(F32)).astype(BF16)
    return hi, lo


def _dot(a, b):
    return jnp.dot(a, b, preferred_element_type=F32)


def _dot_nt(a, b):
    return lax.dot_general(a, b, (((1,), (1,)), ((), ())), preferred_element_type=F32)


def _dot_x_exact(x, m):
    hi, lo = _split2(x)
    return _dot(jnp.concatenate([hi, lo], axis=1), jnp.concatenate([m, m], axis=0))


def _dot_exact_x(m, x):
    hi, lo = _split2(x)
    n = x.shape[1]
    r = _dot(m, jnp.concatenate([hi, lo], axis=1))
    return r[:, :n] + r[:, n:]


def _rope(x, c, s1, s2, half):
    w = x.shape[-1]
    return x * c + pltpu.roll(x, w - half, 1) * s1 + pltpu.roll(x, half, 1) * s2


def _silu(x):
    return x * (1.0 / (1.0 + jnp.exp(-x)))


def _softplus(x):
    return jnp.maximum(x, 0.0) + jnp.log(1.0 + jnp.exp(-jnp.abs(x)))


def _lane_head(shape, width):
    return lax.broadcasted_iota(jnp.int32, shape, 1) // width


def _rms(x, w):
    return x * lax.rsqrt(jnp.mean(x * x, axis=-1, keepdims=True) + RMS_EPS) * w


def _rope_tables(s, rot_dim, theta, head_dim, offset, n_heads):
    half = rot_dim // 2
    inv = 1.0 / (theta ** (jnp.arange(0, rot_dim, 2, dtype=F32) / rot_dim))
    ang = jnp.arange(s, dtype=F32)[:, None] * inv[None, :]
    cos, sin = jnp.cos(ang), jnp.sin(ang)
    pre0 = jnp.zeros((s, offset), F32)
    pre1 = jnp.ones((s, offset), F32)
    post0 = jnp.zeros((s, head_dim - offset - rot_dim), F32)
    post1 = jnp.ones((s, head_dim - offset - rot_dim), F32)
    zh = jnp.zeros((s, half), F32)
    c = jnp.concatenate([pre1, cos, cos, post1], axis=-1)
    s1 = jnp.concatenate([pre0, -sin, zh, post0], axis=-1)
    s2 = jnp.concatenate([pre0, zh, sin, post0], axis=-1)
    tile = lambda t: jnp.tile(t, (1, n_heads))
    return tile(c), tile(s1), tile(s2)


def _tri(n, fn):
    i = jnp.arange(n)
    return fn(i[:, None], i[None, :]).astype(BF16)


def _inproj_kernel(*refs, n_res, col_chunk):
    lnw_ref, w_ref, o_ref = refs[n_res:]
    x = refs[0][...]
    for r in refs[1:n_res]:
        x = x + r[...]
    yb = _rms(x, lnw_ref[...]).astype(BF16)
    for j in range(PROJ_PAD // col_chunk):
        sl = slice(j * col_chunk, (j + 1) * col_chunk)
        o_ref[:, sl] = _dot(yb, w_ref[:, sl])


def _inproj(res, lnw, w_pad):
    t = res[0].shape[0]
    tm = 512
    return pl.pallas_call(
        functools.partial(_inproj_kernel, n_res=len(res), col_chunk=512),
        grid=(t // tm,),
        in_specs=[pl.BlockSpec((tm, D_MODEL), lambda i: (i, 0))] * len(res) + [
            pl.BlockSpec((1, D_MODEL), lambda i: (0, 0)),
            pl.BlockSpec((D_MODEL, PROJ_PAD), lambda i: (0, 0)),
        ],
        out_specs=pl.BlockSpec((tm, PROJ_PAD), lambda i: (i, 0)),
        out_shape=jax.ShapeDtypeStruct((t, PROJ_PAD), F32),
        compiler_params=_cparams(("parallel",)),
    )(*res, lnw, w_pad)


def _pad_cols(w):
    sizes = (256, 256, 256, 256, 256, 128, 32, 256, 768, 8, 256, 256, 256)
    pts, acc = [], 0
    for sz in sizes:
        pts.append((acc, acc + sz))
        acc += sz
    seg = lambda i: w[:, pts[i][0]:pts[i][1]]
    z = lambda n: jnp.zeros((w.shape[0], n), w.dtype)
    cols = [seg(0), seg(1), seg(2), seg(3), seg(4),
            seg(5), seg(6), z(BLK - 128 - 32),
            seg(8), seg(7), seg(9), z(BLK - 8),
            seg(10), seg(11), seg(12)]
    return jnp.concatenate(cols, axis=1)


def _ret_tables():
    pos = jnp.arange(CHUNK, dtype=F32)
    hh = jnp.arange(N_HEADS, dtype=F32)
    lg_f = jnp.log1p(-jnp.exp2(-5.0 - hh))
    lg_b = jnp.log1p(-jnp.exp2(-5.5 - hh))
    diff = pos[:, None] - pos[None, :]
    d_f = jnp.where(diff >= 0, jnp.exp(lg_f[:, None, None] * jnp.maximum(diff, 0.0)), 0.0)
    d_b = jnp.where(diff < 0, jnp.exp(lg_b[:, None, None] * jnp.maximum(-diff, 0.0)), 0.0)
    dmat = (d_f + d_b).reshape(2, 2, CHUNK, CHUNK)

    def lanes(per_head):
        t = jnp.repeat(per_head[:, :, None], HEAD_DIM, axis=2)
        return t.reshape(2, 2, CHUNK, HEAD_DIM).transpose(0, 2, 1, 3).reshape(2, CHUNK, 2 * HEAD_DIM)

    xi_f = lanes(jnp.exp(lg_f[:, None] * (pos + 1.0)))
    zeta_f = lanes(jnp.exp(lg_f[:, None] * (CHUNK - 1.0 - pos)))
    xi_b = lanes(jnp.exp(lg_b[:, None] * (CHUNK - pos)))
    zeta_b = lanes(jnp.exp(lg_b[:, None] * pos))
    blk = (jnp.arange(128)[:, None] // HEAD_DIM) == (jnp.arange(128)[None, :] // HEAD_DIM)

    def cdec(lg):
        g = jnp.exp(lg * CHUNK).reshape(2, 2)
        rows = jnp.repeat(g, HEAD_DIM, axis=1)
        return jnp.where(blk[None], rows[:, :, None], 0.0)

    avg = jnp.where(blk, 1.0 / HEAD_DIM, 0.0).astype(BF16)
    return dmat, xi_f, zeta_f, xi_b, zeta_b, cdec(lg_f), cdec(lg_b), avg


def _ret_kernel(q_ref, k_ref, v_ref, g_ref, c_ref, s1_ref, s2_ref, dmat_ref, xif_ref, zf_ref,
                xib_ref, zb_ref, cdf_ref, cdb_ref, avg_ref, o_ref, sf_ref, sb_ref, pb_ref, *, nsteps, g):
    t = pl.program_id(1)

    @pl.when(t == 0)
    def _():
        sf_ref[...] = jnp.zeros_like(sf_ref)
        sb_ref[...] = jnp.zeros_like(sb_ref)

    def rotated(ref, rows):
        return _rope(ref[rows, :], c_ref[rows, :], s1_ref[rows, :], s2_ref[rows, :], HEAD_DIM // 2)

    def state_update(s_ref, p, k, vb, zeta, cdec):
        kz = (k * zeta).T.astype(BF16)
        blockmask = jnp.where(cdec > 0.0, 1.0, 0.0)
        s_ref[p] = s_ref[p] * cdec + _dot(kz, vb) * blockmask

    @pl.when(t < nsteps)
    def _():
        base = (nsteps - 1 - t) * g
        for ci in reversed(range(g)):
            rows = slice(ci * CHUNK, (ci + 1) * CHUNK)
            k = rotated(k_ref, rows)
            vb = v_ref[rows, :].astype(BF16)
            for p in range(2):
                sl = slice(p * 128, (p + 1) * 128)
                pb_ref[base + ci, p] = sb_ref[p].astype(BF16)
                state_update(sb_ref, p, k[:, sl], vb[:, sl], zb_ref[p], cdb_ref[p])

    @pl.when(t >= nsteps)
    def _():
        base = (t - nsteps) * g
        head = _lane_head((CHUNK, 128), HEAD_DIM)
        avg = avg_ref[...]

        def group_mean(x):
            hi, lo = _split2(x)
            return _dot(hi, avg) + _dot(lo, avg)

        for ci in range(g):
            rows = slice(ci * CHUNK, (ci + 1) * CHUNK)
            k_all = rotated(k_ref, rows)
            q_all = rotated(q_ref, rows) * (HEAD_DIM ** -0.5)
            vb_all = v_ref[rows, :].astype(BF16)
            for p in range(2):
                sl = slice(p * 128, (p + 1) * 128)
                q, k, vb = q_all[:, sl], k_all[:, sl], vb_all[:, sl]
                kb = k.astype(BF16)
                inner = None
                for hh in range(2):
                    qh = jnp.where(head == hh, q, 0.0).astype(BF16)
                    pm = (_dot_nt(qh, kb) * dmat_ref[p, hh]).astype(BF16)
                    oh = _dot(pm, vb)
                    inner = oh if inner is None else jnp.where(head == hh, oh, inner)
                cross_f = _dot((q * xif_ref[p]).astype(BF16), sf_ref[p].astype(BF16))
                cross_b = _dot((q * xib_ref[p]).astype(BF16), pb_ref[base + ci, p])
                o = inner + cross_f + cross_b
                mu = group_mean(o)
                d = o - mu
                var = group_mean(d * d)
                o_ref[rows, sl] = _silu(g_ref[rows, sl]) * (d * lax.rsqrt(var + GN_EPS))
                state_update(sf_ref, p, k, vb, zf_ref[p], cdf_ref[p])


def _retention(proj, tables, rope_tabs):
    b, s, _ = proj.shape
    nc = s // CHUNK
    g = math.gcd(nc, 4)
    nsteps = nc // g
    rows = g * CHUNK
    both = lambda t: jnp.where(t < nsteps, nsteps - 1 - t, t - nsteps)
    late = lambda t: jnp.where(t < nsteps, 0, t - nsteps)
    col = lambda base, idx: pl.BlockSpec((None, rows, BLK), lambda bi, t: (bi, idx(t), base))
    tab = pl.BlockSpec((rows, BLK), lambda bi, t: (both(t), 0))
    pair = pl.BlockSpec((2, CHUNK, 128), lambda bi, t: (0, 0, 0))
    return pl.pallas_call(
        functools.partial(_ret_kernel, nsteps=nsteps, g=g),
        grid=(b, 2 * nsteps),
        in_specs=[
            col(COL_RQ, late), col(COL_RK, both), col(COL_RV, both), col(COL_RG, late),
            tab, tab, tab,
            pl.BlockSpec((2, 2, CHUNK, CHUNK), lambda bi, t: (0, 0, 0, 0)),
            pair, pair, pair, pair, pair, pair,
            pl.BlockSpec((128, 128), lambda bi, t: (0, 0)),
        ],
        out_specs=pl.BlockSpec((None, rows, BLK), lambda bi, t: (bi, late(t), 0)),
        out_shape=jax.ShapeDtypeStruct((b, s, BLK), F32),
        scratch_shapes=[pltpu.VMEM((2, 128, 128), F32), pltpu.VMEM((2, 128, 128), F32),
                        pltpu.VMEM((nc, 2, 128, 128), BF16)],
        compiler_params=_cparams(("parallel", "arbitrary")),
    )(proj, proj, proj, proj, *rope_tabs, *tables)


def _mla_prep_kernel(cq_ref, kv_ref, qnw_ref, kvnw_ref, wq_ref, wk_ref, wv_ref, place_ref,
                     qc_ref, qs1_ref, qs2_ref, kc_ref, ks1_ref, ks2_ref, q_ref, k_ref, v_ref):
    cqn = _rms(cq_ref[...], qnw_ref[...])
    q = _dot(cqn.astype(BF16), wq_ref[...])
    q = _rope(q, qc_ref[...], qs1_ref[...], qs2_ref[...], MLA_ROPE // 2)
    q_ref[...] = (q * ((MLA_NOPE + MLA_ROPE) ** -0.5 * math.log2(math.e))).astype(BF16)
    blk = kv_ref[...]
    ckvn = _rms(blk[:, :MLA_KV_RANK], kvnw_ref[...]).astype(BF16)
    kr = _rope(blk[:, MLA_KV_RANK:], kc_ref[...], ks1_ref[...], ks2_ref[...], MLA_ROPE // 2)
    k_ref[...] = (_dot(ckvn, wk_ref[...]) + _dot(kr.astype(BF16), place_ref[...])).astype(BF16)
    v = _dot(ckvn, wv_ref[...])
    ones = (lax.broadcasted_iota(jnp.int32, v.shape, 1) % 128) >= HEAD_DIM
    v_ref[...] = jnp.where(ones, 1.0, v).astype(BF16)


def _mla_weights(w_uq, w_ukv):
    qh = w_uq.reshape(MLA_Q_RANK, N_HEADS, MLA_NOPE + MLA_ROPE)
    wq = jnp.concatenate([qh, jnp.zeros((MLA_Q_RANK, N_HEADS, 32), F32)], axis=-1).reshape(MLA_Q_RANK, 512)
    kvh = w_ukv.reshape(MLA_KV_RANK, N_HEADS, MLA_NOPE + HEAD_DIM)
    zk = jnp.zeros((MLA_KV_RANK, N_HEADS, 64), F32)
    wk = jnp.concatenate([kvh[..., :MLA_NOPE], zk], axis=-1).reshape(MLA_KV_RANK, 512)
    wv = jnp.concatenate([kvh[..., MLA_NOPE:], zk], axis=-1).reshape(MLA_KV_RANK, 512)
    src = jnp.arange(128)[:, None]
    dst = jnp.arange(512)[None, :]
    place = ((dst % 128 == src + MLA_NOPE) & (src < MLA_ROPE)).astype(BF16)
    return wq.astype(BF16), wk.astype(BF16), wv.astype(BF16), place


def _mla_flash_kernel(q_ref, k_ref, v_ref, o_ref, m_ref, acc_ref, *, strip):
    ki = pl.program_id(2)

    @pl.when(ki == 0)
    def _():
        m_ref[...] = jnp.full_like(m_ref, -jnp.inf)
        acc_ref[...] = jnp.zeros_like(acc_ref)

    for h in range(N_HEADS):
        sl = slice(h * 128, (h + 1) * 128)
        k = k_ref[:, sl]
        v = v_ref[:, sl]
        for r in range(q_ref.shape[0] // strip):
            rows = slice(r * strip, (r + 1) * strip)
            s = _dot_nt(q_ref[rows, sl], k)
            m_prev = m_ref[h, rows]
            m_new = jnp.maximum(m_prev, jnp.max(s, axis=-1, keepdims=True))
            p = jnp.exp2(s - m_new[:, :1])
            acc_ref[h, rows] = jnp.exp2(m_prev - m_new) * acc_ref[h, rows] + _dot(p.astype(BF16), v)
            m_ref[h, rows] = m_new

    @pl.when(ki == pl.num_programs(2) - 1)
    def _():
        outs = []
        for h in range(N_HEADS):
            a = acc_ref[h]
            outs.append(a[:, :HEAD_DIM] / a[:, HEAD_DIM:])
        o_ref[...] = jnp.concatenate(outs, axis=-1)


def _mla(proj, q_norm_w, kv_norm_w, w_uq, w_ukv, q_tabs, k_tabs):
    b, s, _ = proj.shape
    wq, wk, wv, place = _mla_weights(w_uq, w_ukv)
    ts = 512
    row = lambda n: pl.BlockSpec((ts, n), lambda bi, i: (i, 0))
    full = lambda r, n: pl.BlockSpec((r, n), lambda bi, i: (0, 0))
    out = pl.BlockSpec((None, ts, 512), lambda bi, i: (bi, i, 0))
    q, k, v = pl.pallas_call(
        _mla_prep_kernel,
        grid=(b, s // ts),
        in_specs=[
            pl.BlockSpec((None, ts, BLK), lambda bi, i: (bi, i, COL_MCQ)),
            pl.BlockSpec((None, ts, BLK), lambda bi, i: (bi, i, COL_MKV)),
            full(1, MLA_Q_RANK), full(1, MLA_KV_RANK),
            full(MLA_Q_RANK, 512), full(MLA_KV_RANK, 512), full(MLA_KV_RANK, 512), full(128, 512),
            row(512), row(512), row(512), row(128), row(128), row(128),
        ],
        out_specs=[out, out, out],
        out_shape=[jax.ShapeDtypeStruct((b, s, 512), BF16)] * 3,
        compiler_params=_cparams(("parallel", "parallel")),
    )(proj, proj, q_norm_w[None], kv_norm_w[None], wq, wk, wv, place, *q_tabs, *k_tabs)

    tq, tk = math.gcd(s, 2048), math.gcd(s, 512)
    return pl.pallas_call(
        functools.partial(_mla_flash_kernel, strip=tq),
        grid=(b, s // tq, s // tk),
        in_specs=[
            pl.BlockSpec((None, tq, 512), lambda bi, qi, ki: (bi, qi, 0)),
            pl.BlockSpec((None, tk, 512), lambda bi, qi, ki: (bi, ki, 0)),
            pl.BlockSpec((None, tk, 512), lambda bi, qi, ki: (bi, ki, 0)),
        ],
        out_specs=pl.BlockSpec((None, tq, N_HEADS * HEAD_DIM), lambda bi, qi, ki: (bi, qi, 0)),
        out_shape=jax.ShapeDtypeStruct((b, s, N_HEADS * HEAD_DIM), F32),
        scratch_shapes=[pltpu.VMEM((N_HEADS, tq, 128), F32), pltpu.VMEM((N_HEADS, tq, 128), F32)],
        compiler_params=_cparams(("parallel", "parallel", "arbitrary")),
    )(q, k, v)


def _conv_kernel(prev_ref, cur_ref, next_ref, w_ref, b_ref, o_ref, *, ts):
    i = pl.program_id(1)
    w = w_ref[...]

    def conv(x):
        n = x.shape[0]
        acc = x * w[2:3]
        for s in (-2, -1, 1, 2):
            acc = acc + pltpu.roll(x, (-s) % n, 0) * w[s + 2:s + 3]
        return acc

    act = lambda y: _silu(y + b_ref[...])
    cur = cur_ref[...]
    prev = prev_ref[...] * (i > 0).astype(F32)
    nxt = next_ref[...] * (i < pl.num_programs(1) - 1).astype(F32)
    o_ref[...] = act(conv(cur))
    top = conv(jnp.concatenate([prev, cur[:16]], axis=0))
    o_ref[0:8, :] = act(top[8:16])
    bot = conv(jnp.concatenate([cur[ts - 16:], nxt], axis=0))
    o_ref[ts - 8:ts, :] = act(bot[8:16])


def _ssd_conv(proj, conv_w, conv_b):
    b, s, _ = proj.shape
    ts = 512
    width = 3 * BLK
    cb = COL_XBC * BLK // width
    nb8 = ts // 8
    w8 = jnp.concatenate([conv_w, jnp.zeros((8 - SSM_CONV, width), F32)], axis=0)
    return pl.pallas_call(
        functools.partial(_conv_kernel, ts=ts),
        grid=(b, s // ts),
        in_specs=[
            pl.BlockSpec((None, 8, width), lambda bi, i: (bi, jnp.maximum(i * nb8 - 1, 0), cb)),
            pl.BlockSpec((None, ts, width), lambda bi, i: (bi, i, cb)),
            pl.BlockSpec((None, 8, width), lambda bi, i: (bi, jnp.minimum((i + 1) * nb8, s // 8 - 1), cb)),
            pl.BlockSpec((8, width), lambda bi, i: (0, 0)),
            pl.BlockSpec((1, width), lambda bi, i: (0, 0)),
        ],
        out_specs=pl.BlockSpec((None, ts, width), lambda bi, i: (bi, i, 0)),
        out_shape=jax.ShapeDtypeStruct((b, s, width), F32),
        compiler_params=_cparams(("parallel", "parallel")),
    )(proj, proj, proj, w8, conv_b[None])


def _ssd_tables():
    tril = _tri(CHUNK, lambda i, j: j <= i)
    triu = _tri(CHUNK, lambda i, j: j >= i)
    h = jnp.arange(128)[:, None]
    lane = jnp.arange(256)[None, :]
    e_f = ((lane // HEAD_DIM == h) & (h < N_HEADS)).astype(BF16)
    e_b = ((lane // HEAD_DIM == h - N_HEADS) & (h >= N_HEADS) & (h < 2 * N_HEADS)).astype(BF16)
    return tril, triu, e_f, e_b


def _ssd_kernel(xs_ref, bm_ref, cm_ref, z_ref, dt_ref, dtt_ref, bias_ref, a_ref, biasc_ref, ac_ref,
                dskip_ref, nw_ref, tril_ref, triu_ref, ef_ref, eb_ref, o_ref,
                sf_ref, sb_ref, pb_ref, *, nsteps, g):
    t = pl.program_id(1)
    tril, triu = tril_ref[...], triu_ref[...]

    @pl.when(t == 0)
    def _():
        sf_ref[...] = jnp.zeros_like(sf_ref)
        sb_ref[...] = jnp.zeros_like(sb_ref)

    def step_sizes(rows):
        dt = _softplus(dt_ref[rows, :128] + bias_ref[...])
        return dt, dt * a_ref[...]

    def backward_terms(xs, dt, dta):
        rcs_b = _dot_x_exact(_dot_exact_x(triu, dta), eb_ref[...])
        return rcs_b, rcs_b[0:1, :], xs * _dot_x_exact(dt, eb_ref[...])

    def state_update(s_ref, bmf, weighted_x, total):
        for gi in range(SSM_GROUPS):
            sl = slice(gi * 128, (gi + 1) * 128)
            upd = _dot(bmf[:, sl].T.astype(BF16), weighted_x[:, sl].astype(BF16))
            s_ref[gi] = s_ref[gi] * jnp.exp(total[:, sl]) + upd

    @pl.when(t < nsteps)
    def _():
        base = (nsteps - 1 - t) * g
        for ci in reversed(range(g)):
            rows = slice(ci * CHUNK, (ci + 1) * CHUNK)
            xs = xs_ref[rows, :]
            dt, dta = step_sizes(rows)
            rcs_b, tot_b, xdt_b = backward_terms(xs, dt, dta)
            pb_ref[base + ci] = sb_ref[...].astype(BF16)
            state_update(sb_ref, bm_ref[rows, :], jnp.exp(tot_b - rcs_b) * xdt_b, tot_b)

    @pl.when(t >= nsteps)
    def _():
        base = (t - nsteps) * g
        row = lax.broadcasted_iota(jnp.int32, (CHUNK, CHUNK), 0)
        colm = lax.broadcasted_iota(jnp.int32, (CHUNK, CHUNK), 1)
        head = _lane_head((CHUNK, 128), HEAD_DIM)
        efb = jnp.concatenate([ef_ref[...], eb_ref[...]], axis=1)
        for ci in range(g):
            rows = slice(ci * CHUNK, (ci + 1) * CHUNK)
            xs = xs_ref[rows, :]
            bmf = bm_ref[rows, :]
            bm = bmf.astype(BF16)
            cm = cm_ref[rows, :].astype(BF16)
            dt, dta = step_sizes(rows)
            rcs_b = _dot_x_exact(_dot_exact_x(triu, dta), eb_ref[...])
            cs_f = _dot_x_exact(_dot_exact_x(tril, dta), ef_ref[...])
            tot_f = cs_f[CHUNK - 1:CHUNK, :]
            dt_exp = _dot_x_exact(dt, efb)
            xdt_f = xs * dt_exp[:, :SSM_INNER]
            xdt_b = xs * dt_exp[:, SSM_INNER:]
            dtat = _softplus(dtt_ref[:, rows] + biasc_ref[...]) * ac_ref[...]
            cst = _dot_x_exact(dtat, triu)
            rcst = _dot_x_exact(dtat, tril)
            pb = pb_ref[base + ci]
            ys = []
            for gi in range(SSM_GROUPS):
                sl = slice(gi * 128, (gi + 1) * 128)
                cb = _dot_nt(cm[:, sl], bm[:, sl])
                xcat = jnp.concatenate([xdt_f[:, sl], xdt_b[:, sl]], axis=0).astype(BF16)
                yg = None
                for hh in range(2):
                    h = 2 * gi + hh
                    c0 = h * HEAD_DIM
                    seg_f = cs_f[:, c0:c0 + 1] - cst[h:h + 1, :]
                    seg_b = rcs_b[:, c0:c0 + 1] - rcst[N_HEADS + h:N_HEADS + h + 1, :]
                    dec_f = jnp.where(row >= colm, jnp.exp(jnp.minimum(seg_f, 0.0)), 0.0)
                    dec_b = jnp.where(row < colm, jnp.exp(jnp.minimum(seg_b, 0.0)), 0.0)
                    wcat = jnp.concatenate([cb * dec_f, cb * dec_b], axis=1).astype(BF16)
                    yh = _dot(wcat, xcat)
                    yg = yh if yg is None else jnp.where(head == hh, yh, yg)
                off_f = jnp.exp(cs_f[:, sl]) * _dot(cm[:, sl], sf_ref[gi].astype(BF16))
                off_b = jnp.exp(rcs_b[:, sl]) * _dot(cm[:, sl], pb[gi])
                ys.append(yg + off_f + off_b)
            y = jnp.concatenate(ys, axis=1) + dskip_ref[...] * xs
            y = y * _silu(z_ref[rows, :])
            o_ref[rows, :] = _rms(y, nw_ref[...])
            state_update(sf_ref, bmf, jnp.exp(tot_f - cs_f) * xdt_f, tot_f)


def _ssd(proj, xbc, a_log, dt_bias, d_skip, norm_w, tables):
    b, s, _ = proj.shape
    nc = s // CHUNK
    g = math.gcd(nc, 4)
    nsteps = nc // g
    rows = g * CHUNK
    dt_t = jnp.swapaxes(proj[:, :, COL_DT * BLK:COL_DT * BLK + 8], 1, 2)
    pad_row = lambda v: jnp.concatenate([v.reshape(1, 8), jnp.zeros((1, 120), F32)], axis=1)
    a = -jnp.exp(a_log.astype(F32))
    both = lambda t: jnp.where(t < nsteps, nsteps - 1 - t, t - nsteps)
    late = lambda t: jnp.where(t < nsteps, 0, t - nsteps)
    full = lambda r, n: pl.BlockSpec((r, n), lambda bi, t: (0, 0))
    blk = lambda idx, c: pl.BlockSpec((None, rows, BLK), lambda bi, t: (bi, idx(t), c))
    return pl.pallas_call(
        functools.partial(_ssd_kernel, nsteps=nsteps, g=g),
        grid=(b, 2 * nsteps),
        in_specs=[
            blk(both, 0), blk(both, 1), blk(late, 2), blk(late, COL_Z), blk(both, COL_DT),
            pl.BlockSpec((None, 8, rows), lambda bi, t: (bi, 0, late(t))),
            full(1, 128), full(1, 128), full(8, 1), full(8, 1),
            full(1, SSM_INNER), full(1, SSM_INNER),
            full(CHUNK, CHUNK), full(CHUNK, CHUNK), full(128, 256), full(128, 256),
        ],
        out_specs=blk(late, 0),
        out_shape=jax.ShapeDtypeStruct((b, s, SSM_INNER), F32),
        scratch_shapes=[pltpu.VMEM((SSM_GROUPS, 128, 128), F32), pltpu.VMEM((SSM_GROUPS, 128, 128), F32),
                        pltpu.VMEM((nc, SSM_GROUPS, 128, 128), BF16)],
        compiler_params=_cparams(("parallel", "arbitrary")),
    )(xbc, xbc, xbc, proj, proj, dt_t, pad_row(dt_bias), pad_row(a), dt_bias.reshape(8, 1), a.reshape(8, 1),
      jnp.repeat(d_skip, HEAD_DIM)[None], norm_w[None], *tables)


def _dil_prep_kernel(q_ref, k_ref, c_ref, s1_ref, s2_ref, qo_ref, ko_ref):
    c, s1, s2 = c_ref[...], s1_ref[...], s2_ref[...]
    qo_ref[...] = _rope(q_ref[...], c, s1, s2, ROPE_DIM // 2) * (HEAD_DIM ** -0.5)
    ko_ref[...] = _rope(k_ref[...], c, s1, s2, ROPE_DIM // 2)


def _dil_kernel(q_ref, k_ref, v_ref, o_ref, m_ref, l_ref, *, s):
    head = _lane_head((128, 128), HEAD_DIM)
    kw = 2 * 128
    for pi, (win, d) in enumerate(DIL_PATTERNS):
        half = win // (2 * d)
        seg = s // d
        per_seg = seg // 128

        def body(i, carry, d=d, pi=pi, half=half, seg=seg, per_seg=per_seg):
            r = i // per_seg
            m0 = (i % per_seg) * 128
            ks = jnp.clip(m0 - half, 0, seg - kw)
            if d == 1:
                qrows = pl.ds(pl.multiple_of(m0, 128), 128)
                krows = pl.ds(pl.multiple_of(ks, 64), kw)
            else:
                qrows = pl.ds(r + d * m0, 128, stride=d)
                krows = pl.ds(r + d * ks, kw, stride=d)
            q = q_ref[qrows, :]
            kb = k_ref[krows, :].astype(BF16)
            vb = v_ref[krows, :].astype(BF16)
            qpos = m0 + lax.broadcasted_iota(jnp.int32, (128, kw), 0)
            kpos = ks + lax.broadcasted_iota(jnp.int32, (128, kw), 1)
            valid = jnp.abs(qpos - kpos) <= half
            mb = lb = pv = None
            for hh in range(2):
                sc = _dot_nt(jnp.where(head == hh, q, 0.0).astype(BF16), kb)
                sc = jnp.where(valid, sc, -jnp.inf)
                mh = jnp.max(sc, axis=-1, keepdims=True)
                p = jnp.exp(sc - mh)
                lh = jnp.sum(p, axis=-1, keepdims=True)
                ph = _dot(p.astype(BF16), vb)
                if hh == 0:
                    mb = jnp.broadcast_to(mh, (128, 128))
                    lb = jnp.broadcast_to(lh, (128, 128))
                    pv = ph
                else:
                    mb = jnp.where(head == hh, mh, mb)
                    lb = jnp.where(head == hh, lh, lb)
                    pv = jnp.where(head == hh, ph, pv)
            if pi == 0:
                m_ref[qrows, :] = mb
                l_ref[qrows, :] = lb
                o_ref[qrows, :] = pv
            else:
                mo = m_ref[qrows, :]
                mn = jnp.maximum(mo, mb)
                wa = jnp.exp(mo - mn)
                wb = jnp.exp(mb - mn)
                m_ref[qrows, :] = mn
                l_ref[qrows, :] = wa * l_ref[qrows, :] + wb * lb
                o_ref[qrows, :] = wa * o_ref[qrows, :] + wb * pv
            return carry

        lax.fori_loop(0, s // 128, body, 0, unroll=4)
    o_ref[...] = o_ref[...] / l_ref[...]


def _dilated(proj, tabs):
    b, s, _ = proj.shape
    assert all(s // d >= 256 for _, d in DIL_PATTERNS)
    ts = 512
    rowt = pl.BlockSpec((ts, BLK), lambda bi, i: (i, 0))
    outb = pl.BlockSpec((None, ts, BLK), lambda bi, i: (bi, i, 0))
    qr, kr = pl.pallas_call(
        _dil_prep_kernel,
        grid=(b, s // ts),
        in_specs=[
            pl.BlockSpec((None, ts, BLK), lambda bi, i: (bi, i, COL_DQ)),
            pl.BlockSpec((None, ts, BLK), lambda bi, i: (bi, i, COL_DK)),
            rowt, rowt, rowt,
        ],
        out_specs=[outb, outb],
        out_shape=[jax.ShapeDtypeStruct((b, s, BLK), F32)] * 2,
        compiler_params=_cparams(("parallel", "parallel")),
    )(proj, proj, *tabs)
    seqb = lambda c: pl.BlockSpec((None, s, 128), lambda bi, p: (bi, 0, 2 * c + p))
    return pl.pallas_call(
        functools.partial(_dil_kernel, s=s),
        grid=(b, 2),
        in_specs=[seqb(0), seqb(0), seqb(COL_DV)],
        out_specs=seqb(0),
        out_shape=jax.ShapeDtypeStruct((b, s, BLK), F32),
        scratch_shapes=[pltpu.VMEM((s, 128), F32), pltpu.VMEM((s, 128), F32)],
        compiler_params=_cparams(("parallel", "parallel")),
    )(qr, kr, proj)


def _outproj_kernel(*refs, n_res):
    ya_ref, yb_ref, yc_ref, yd_ref, w_ref, lnw_ref, rw_ref, xo_ref, h_ref, aff_ref = refs[n_res:]
    acc = refs[0][...]
    for r in refs[1:n_res]:
        acc = acc + r[...]
    mixed = jnp.concatenate([r[...].astype(BF16) for r in (ya_ref, yb_ref, yc_ref, yd_ref)], axis=1)
    acc = acc + _dot(mixed, w_ref[...])
    xo_ref[...] = acc
    h = _rms(acc, lnw_ref[...])
    h_hi = h.astype(BF16)
    h_ref[...] = h_hi
    h_lo = (h - h_hi.astype(F32)).astype(BF16)
    rw = rw_ref[...]
    w_hi = rw.astype(BF16)
    w_lo = (rw - w_hi.astype(F32)).astype(BF16)
    logits = _dot(h_hi, w_hi) + _dot(h_hi, w_lo) + _dot(h_lo, w_hi) + _dot(h_lo, w_lo)
    lane = lax.broadcasted_iota(jnp.int32, logits.shape, 1)
    logits = jnp.where(lane < N_EXPERTS, logits, -jnp.inf)
    e = jnp.exp(logits - jnp.max(logits, axis=-1, keepdims=True))
    aff = e / jnp.sum(e, axis=-1, keepdims=True)
    aff_ref[...] = aff.T[:N_EXPERTS, :]


def _outproj(res, ys, w_out, ln2_w, router_w):
    t = res[0].shape[0]
    tm = 256
    rw = jnp.concatenate([router_w, jnp.zeros((D_MODEL, 128 - N_EXPERTS), F32)], axis=1)
    row = lambda n: pl.BlockSpec((tm, n), lambda i: (i, 0))
    full = lambda r, n: pl.BlockSpec((r, n), lambda i: (0, 0))
    return pl.pallas_call(
        functools.partial(_outproj_kernel, n_res=len(res)),
        grid=(t // tm,),
        in_specs=[row(D_MODEL)] * len(res) + [row(BLK)] * 4 + [
            full(D_MODEL, D_MODEL), full(1, D_MODEL), full(D_MODEL, 128)],
        out_specs=[row(D_MODEL), row(D_MODEL), pl.BlockSpec((N_EXPERTS, tm), lambda i: (0, i))],
        out_shape=[jax.ShapeDtypeStruct((t, D_MODEL), F32),
                   jax.ShapeDtypeStruct((t, D_MODEL), BF16),
                   jax.ShapeDtypeStruct((N_EXPERTS, t), F32)],
        compiler_params=_cparams(("parallel",)),
    )(*res, *ys, w_out.astype(BF16), ln2_w[None], rw)


def _exclusive_prefix(x, tri):
    n = x.shape[1] // 128
    off = jnp.zeros((x.shape[0], 1), F32)
    parts = []
    for i in range(n):
        xt = x[:, i * 128:(i + 1) * 128]
        incl = _dot(xt.astype(BF16), tri)
        parts.append(incl - xt + off)
        off = off + incl[:, 127:128]
    return jnp.concatenate(parts, axis=1)


def _select_kernel(aff_ref, tri_ref, key_ref, *, cap):
    a = aff_ref[...]
    n_exp = a.shape[0]
    thr_bits = jnp.zeros((n_exp, 1), jnp.int32)
    for bit in range(30, -1, -1):
        cand = thr_bits | (1 << bit)
        cnt = jnp.sum(jnp.where(a >= lax.bitcast_convert_type(cand, F32), 1.0, 0.0), axis=1, keepdims=True)
        thr_bits = jnp.where(cnt >= cap, cand, thr_bits)
    thr = lax.bitcast_convert_type(thr_bits, F32)
    above = jnp.where(a > thr, 1.0, 0.0)
    tie = jnp.where(a == thr, 1.0, 0.0)
    need = cap - jnp.sum(above, axis=1, keepdims=True)
    tri = tri_ref[...]
    sel = above + tie * jnp.where(_exclusive_prefix(tie, tri) < need, 1.0, 0.0)
    key_ref[...] = jnp.where(sel > 0.0, _exclusive_prefix(sel, tri), -1.0)


def _slot_kernel(key_ref, aff_ref, ltri_ref, idx_ref, gate_ref, *, cap, nt):
    n_exp = key_ref.shape[0]
    pad = jnp.zeros((128 - nt, 128), F32)
    slot = lax.broadcasted_iota(jnp.int32, (128, cap), 1).astype(F32)
    sub = lax.broadcasted_iota(jnp.int32, (128, cap), 0).astype(F32)
    ltri = ltri_ref[...]

    def per_expert(e, carry):
        key = jnp.concatenate([key_ref[e], pad - 1.0], axis=0)
        aff = jnp.concatenate([aff_ref[e], pad], axis=0)
        cnt = jnp.sum(jnp.where(key >= 0.0, 1.0, 0.0), axis=1, keepdims=True)
        cum = _dot(ltri, jnp.broadcast_to(cnt, (128, 128)).astype(BF16))
        in_tile = (jnp.where(slot >= (cum - cnt)[:, :1], 1.0, 0.0)
                   * jnp.where(slot < cum[:, :1], 1.0, 0.0))
        g = in_tile.astype(BF16)
        k_hi, k_lo = _split2(key.T)
        slot_of = _dot(k_hi, g) + _dot(k_lo, g)
        hit = jnp.where(slot_of == slot, 1.0, 0.0)
        aff_t = aff.T
        a_hi = aff_t.astype(BF16)
        r1 = aff_t - a_hi.astype(F32)
        a_mid = r1.astype(BF16)
        a_lo = (r1 - a_mid.astype(F32)).astype(BF16)
        aff_of = _dot(a_hi, g) + _dot(a_mid, g) + _dot(a_lo, g)
        tile = jnp.sum(in_tile * sub, axis=0, keepdims=True)
        within = jnp.sum(hit * sub, axis=0, keepdims=True)
        idx_ref[pl.ds(e, 1), :] = (128.0 * tile + within).astype(jnp.int32)
        gate_ref[pl.ds(e, 1), :] = jnp.sum(hit * aff_of, axis=0, keepdims=True)
        return carry

    lax.fori_loop(0, n_exp, per_expert, 0, unroll=4)


def _moe_select(aff_t, b, s, cap):
    nt = s // 128
    assert nt <= 128 and nt % 8 == 0 and cap % 128 == 0
    key = pl.pallas_call(
        functools.partial(_select_kernel, cap=cap),
        grid=(b,),
        in_specs=[pl.BlockSpec((N_EXPERTS, s), lambda bi: (0, bi)),
                  pl.BlockSpec((128, 128), lambda bi: (0, 0))],
        out_specs=pl.BlockSpec((None, N_EXPERTS, s), lambda bi: (bi, 0, 0)),
        out_shape=jax.ShapeDtypeStruct((b, N_EXPERTS, s), F32),
        compiler_params=_cparams(("parallel",)),
    )(aff_t, _tri(128, lambda i, j: i <= j))
    out = pl.BlockSpec((None, N_EXPERTS, cap), lambda bi: (bi, 0, 0))
    return pl.pallas_call(
        functools.partial(_slot_kernel, cap=cap, nt=nt),
        grid=(b,),
        in_specs=[pl.BlockSpec((None, N_EXPERTS, nt, 128), lambda bi: (bi, 0, 0, 0)),
                  pl.BlockSpec((N_EXPERTS, nt, 128), lambda bi: (0, bi, 0)),
                  pl.BlockSpec((128, 128), lambda bi: (0, 0))],
        out_specs=[out, out],
        out_shape=[jax.ShapeDtypeStruct((b, N_EXPERTS, cap), jnp.int32),
                   jax.ShapeDtypeStruct((b, N_EXPERTS, cap), F32)],
        compiler_params=_cparams(("parallel",)),
    )(key.reshape(b, N_EXPERTS, nt, 128), aff_t.reshape(N_EXPERTS, b * nt, 128), _tri(128, lambda i, j: j <= i))


ROWS_PER_ITER = 8


def _gather_kernel(idx_ref, h_ref, o_ref, hf_ref, buf_ref, *, cap, n_exp):
    e = pl.program_id(1)
    base = (pl.program_id(0) * n_exp + e) * cap

    @pl.when(e == 0)
    def _():
        hf_ref[...] = h_ref[...].astype(F32)

    def body(jb, carry):
        j0 = pl.multiple_of(jb * ROWS_PER_ITER, ROWS_PER_ITER)
        vals = [hf_ref[pl.ds(idx_ref[base + j0 + u], 1), :] for u in range(ROWS_PER_ITER)]
        for u in range(ROWS_PER_ITER):
            buf_ref[pl.ds(j0 + u, 1), :] = vals[u]
        return carry

    lax.fori_loop(0, cap // ROWS_PER_ITER, body, 0)
    o_ref[...] = buf_ref[...].astype(BF16)


def _moe_gather(idx_flat, h, cap):
    b, s, d = h.shape
    return pl.pallas_call(
        functools.partial(_gather_kernel, cap=cap, n_exp=N_EXPERTS),
        grid_spec=pltpu.PrefetchScalarGridSpec(
            num_scalar_prefetch=1,
            grid=(b, N_EXPERTS),
            in_specs=[pl.BlockSpec((None, s, d), lambda bi, e, idx: (bi, 0, 0))],
            out_specs=pl.BlockSpec((None, None, cap, d), lambda bi, e, idx: (e, bi, 0, 0)),
            scratch_shapes=[pltpu.VMEM((s, d), F32), pltpu.VMEM((cap, d), F32)],
        ),
        out_shape=jax.ShapeDtypeStruct((N_EXPERTS, b, cap, d), BF16),
        compiler_params=_cparams(("parallel", "arbitrary")),
    )(idx_flat, h)


def _ffn_kernel(x_ref, wg_ref, wu_ref, wd_ref, o_ref, *, rows, tr):
    f = pl.program_id(1)
    wg = wg_ref[...].astype(BF16)
    wu = wu_ref[...].astype(BF16)
    wd = wd_ref[...].astype(BF16)

    def partial_out(sl):
        x = x_ref[sl, :]
        hid = (_silu(_dot(x, wg)) * _dot(x, wu)).astype(BF16)
        return _dot(hid, wd)

    for r in range(rows // tr):
        sl = slice(r * tr, (r + 1) * tr)

        @pl.when(f == 0)
        def _():
            o_ref[sl, :] = partial_out(sl)

        @pl.when(f > 0)
        def _():
            o_ref[sl, :] += partial_out(sl)


def _moe_ffn(xin, w_gate, w_up, w_down, layer):
    n_exp, rows, d = xin.shape
    tf = 512
    return pl.pallas_call(
        functools.partial(_ffn_kernel, rows=rows, tr=math.gcd(rows, 1024)),
        grid=(n_exp, EXPERT_FF // tf),
        in_specs=[
            pl.BlockSpec((None, rows, d), lambda e, f: (e, 0, 0)),
            pl.BlockSpec((None, None, d, tf), lambda e, f: (layer, e, 0, f)),
            pl.BlockSpec((None, None, d, tf), lambda e, f: (layer, e, 0, f)),
            pl.BlockSpec((None, None, tf, d), lambda e, f: (layer, e, f, 0)),
        ],
        out_specs=pl.BlockSpec((None, rows, d), lambda e, f: (e, 0, 0)),
        out_shape=jax.ShapeDtypeStruct((n_exp, rows, d), F32),
        compiler_params=_cparams(("parallel", "arbitrary")),
    )(xin, w_gate, w_up, w_down)


def _combine_kernel(idx_ref, gate_ref, y_ref, o_ref, *, cap, n_exp):
    e = pl.program_id(1)
    base = (pl.program_id(0) * n_exp + e) * cap

    @pl.when(e == 0)
    def _():
        o_ref[...] = jnp.zeros_like(o_ref)

    def body(jb, carry):
        j0 = pl.multiple_of(jb * ROWS_PER_ITER, ROWS_PER_ITER)
        rows = [idx_ref[base + j0 + u] for u in range(ROWS_PER_ITER)]
        vals = [o_ref[pl.ds(rows[u], 1), :] + gate_ref[base + j0 + u] * y_ref[pl.ds(j0 + u, 1), :]
                for u in range(ROWS_PER_ITER)]
        for u in range(ROWS_PER_ITER):
            o_ref[pl.ds(rows[u], 1), :] = vals[u]
        return carry

    lax.fori_loop(0, cap // ROWS_PER_ITER, body, 0)


def _moe_combine(idx_flat, gate_flat, y, b, s, cap):
    d = y.shape[-1]
    return pl.pallas_call(
        functools.partial(_combine_kernel, cap=cap, n_exp=N_EXPERTS),
        grid_spec=pltpu.PrefetchScalarGridSpec(
            num_scalar_prefetch=2,
            grid=(b, N_EXPERTS),
            in_specs=[pl.BlockSpec((None, None, cap, d), lambda bi, e, idx, gate: (e, bi, 0, 0))],
            out_specs=pl.BlockSpec((None, s, d), lambda bi, e, idx, gate: (bi, 0, 0)),
        ),
        out_shape=jax.ShapeDtypeStruct((b, s, d), F32),
        compiler_params=_cparams(("parallel", "arbitrary")),
    )(idx_flat, gate_flat, y)


def _final_norm_kernel(xa_ref, xb_ref, w_ref, o_ref):
    o_ref[...] = _rms(xa_ref[...] + xb_ref[...], w_ref[...])


def _final_norm(xa, xb, w):
    t = xa.shape[0]
    tm = 1024
    row = pl.BlockSpec((tm, D_MODEL), lambda i: (i, 0))
    return pl.pallas_call(
        _final_norm_kernel,
        grid=(t // tm,),
        in_specs=[row, row, pl.BlockSpec((1, D_MODEL), lambda i: (0, 0))],
        out_specs=row,
        out_shape=jax.ShapeDtypeStruct((t, D_MODEL), F32),
        compiler_params=_cparams(("parallel",)),
    )(xa, xb, w[None])


def _moe(h, aff, w_gate, w_up, w_down, layer, b, s):
    d = h.shape[-1]
    cap = EC_CAPACITY_FACTOR * s // N_EXPERTS
    token_idx, gate = _moe_select(aff, b, s, cap)
    idx_flat = token_idx.reshape(-1)
    xin = _moe_gather(idx_flat, h.reshape(b, s, d), cap)
    y = _moe_ffn(xin.reshape(N_EXPERTS, b * cap, d), w_gate, w_up, w_down, layer)
    return _moe_combine(idx_flat, gate.reshape(-1), y.reshape(N_EXPERTS, b, cap, d), b, s, cap)


def kernel(x, ln1_w, w_in, mla_q_norm_w, mla_kv_norm_w, mla_w_uq, mla_w_ukv, ssm_conv_w, ssm_conv_b,
           ssm_a_log, ssm_dt_bias, ssm_d, ssm_norm_w, w_out, ln2_w, router_w, exp_w_gate, exp_w_up,
           exp_w_down, final_norm_w):
    b, s, d = x.shape
    depth = w_in.shape[0]
    ret_tabs = _ret_tables()
    ret_rope = _rope_tables(s, HEAD_DIM, RET_THETA, HEAD_DIM, 0, N_HEADS)
    mla_q_rope = _rope_tables(s, MLA_ROPE, ROPE_THETA, 128, MLA_NOPE, N_HEADS)
    mla_k_rope = _rope_tables(s, MLA_ROPE, ROPE_THETA, 128, 0, 1)
    dil_rope = _rope_tables(s, ROPE_DIM, ROPE_THETA, HEAD_DIM, 0, N_HEADS)
    ssd_tabs = _ssd_tables()
    res = [x.reshape(b * s, d)]
    for i in range(depth):
        proj = _inproj(res, ln1_w[i][None], _pad_cols(w_in[i]).astype(BF16)).reshape(b, s, PROJ_PAD)
        y_a = _retention(proj, ret_tabs, ret_rope)
        y_b = _mla(proj, mla_q_norm_w[i], mla_kv_norm_w[i], mla_w_uq[i], mla_w_ukv[i], mla_q_rope, mla_k_rope)
        xbc = _ssd_conv(proj, ssm_conv_w[i], ssm_conv_b[i])
        y_c = _ssd(proj, xbc, ssm_a_log[i], ssm_dt_bias[i], ssm_d[i], ssm_norm_w[i], ssd_tabs)
        y_d = _dilated(proj, dil_rope)
        ys = [y.reshape(b * s, BLK) for y in (y_a, y_b, y_c, y_d)]
        x_mid, h, aff = _outproj(res, ys, w_out[i], ln2_w[i], router_w[i])
        moe = _moe(h, aff, exp_w_gate, exp_w_up, exp_w_down, i, b, s)
        res = [x_mid, moe.reshape(b * s, d)]
    return _final_norm(res[0], res[1], final_norm_w).reshape(b, s, d)
```

```python
import functools
import math

import jax
import jax.numpy as jnp
from jax import lax
from jax.experimental import pallas as pl
from jax.experimental.pallas import tpu as pltpu

F32 = jnp.float32
BF16 = jnp.bfloat16

D_MODEL = 1024
RMS_EPS = 1e-6
GN_EPS = 1e-5
CHUNK = 128
HEAD_DIM = 64
N_HEADS = 4

RET_THETA = 10000.0
ROPE_THETA = 500000.0
ROPE_DIM = 16

MLA_Q_RANK = 256
MLA_KV_RANK = 128
MLA_NOPE = 64
MLA_ROPE = 32

SSM_GROUPS = 2
SSM_STATE = 128
SSM_CONV = 5
SSM_INNER = 256

DIL_PATTERNS = ((128, 1), (512, 4), (2048, 16))

N_EXPERTS = 16
EXPERT_FF = 2048
EC_CAPACITY_FACTOR = 2

BLK = 256
COL_RQ, COL_RK, COL_RV, COL_RG = 0, 1, 2, 3
COL_MCQ, COL_MKV = 4, 5
COL_XBC = 6
COL_Z = 9
COL_DT = 10
COL_DQ, COL_DK, COL_DV = 11, 12, 13
PROJ_PAD = 14 * BLK

VMEM_LIMIT = 56 * 1024 * 1024


def _cparams(sem):
    return pltpu.CompilerParams(dimension_semantics=sem, vmem_limit_bytes=VMEM_LIMIT)


def _split2(x):
    hi = x.astype(BF16)
    lo = (x - hi.astype(F32)).astype(BF16)
    return hi, lo


def _dot(a, b):
    return jnp.dot(a, b, preferred_element_type=F32)


def _dot_nt(a, b):
    return lax.dot_general(a, b, (((1,), (1,)), ((), ())), preferred_element_type=F32)


def _dot_x_exact(x, m):
    hi, lo = _split2(x)
    return _dot(jnp.concatenate([hi, lo], axis=1), jnp.concatenate([m, m], axis=0))


def _dot_exact_x(m, x):
    hi, lo = _split2(x)
    n = x.shape[1]
    r = _dot(m, jnp.concatenate([hi, lo], axis=1))
    return r[:, :n] + r[:, n:]


def _rope(x, c, s1, s2, half):
    w = x.shape[-1]
    return x * c + pltpu.roll(x, w - half, 1) * s1 + pltpu.roll(x, half, 1) * s2


def _silu(x):
    return x * (1.0 / (1.0 + jnp.exp(-x)))


def _softplus(x):
    return jnp.maximum(x, 0.0) + jnp.log(1.0 + jnp.exp(-jnp.abs(x)))


def _lane_head(shape, width):
    return lax.broadcasted_iota(jnp.int32, shape, 1) // width


def _rms(x, w):
    return x * lax.rsqrt(jnp.mean(x * x, axis=-1, keepdims=True) + RMS_EPS) * w


def _rope_tables(s, rot_dim, theta, head_dim, offset, n_heads):
    half = rot_dim // 2
    inv = 1.0 / (theta ** (jnp.arange(0, rot_dim, 2, dtype=F32) / rot_dim))
    ang = jnp.arange(s, dtype=F32)[:, None] * inv[None, :]
    cos, sin = jnp.cos(ang), jnp.sin(ang)
    pre0 = jnp.zeros((s, offset), F32)
    pre1 = jnp.ones((s, offset), F32)
    post0 = jnp.zeros((s, head_dim - offset - rot_dim), F32)
    post1 = jnp.ones((s, head_dim - offset - rot_dim), F32)
    zh = jnp.zeros((s, half), F32)
    c = jnp.concatenate([pre1, cos, cos, post1], axis=-1)
    s1 = jnp.concatenate([pre0, -sin, zh, post0], axis=-1)
    s2 = jnp.concatenate([pre0, zh, sin, post0], axis=-1)
    tile = lambda t: jnp.tile(t, (1, n_heads))
    return tile(c), tile(s1), tile(s2)


def _tri(n, fn):
    i = jnp.arange(n)
    return fn(i[:, None], i[None, :]).astype(BF16)


def _inproj_kernel(*refs, n_res, col_chunk):
    lnw_ref, w_ref, o_ref = refs[n_res:]
    x = refs[0][...]
    for r in refs[1:n_res]:
        x = x + r[...]
    yb = _rms(x, lnw_ref[...]).astype(BF16)
    for j in range(PROJ_PAD // col_chunk):
        sl = slice(j * col_chunk, (j + 1) * col_chunk)
        o_ref[:, sl] = _dot(yb, w_ref[:, sl])


def _inproj(res, lnw, w_pad):
    t = res[0].shape[0]
    tm = 512
    return pl.pallas_call(
        functools.partial(_inproj_kernel, n_res=len(res), col_chunk=512),
        grid=(t // tm,),
        in_specs=[pl.BlockSpec((tm, D_MODEL), lambda i: (i, 0))] * len(res) + [
            pl.BlockSpec((1, D_MODEL), lambda i: (0, 0)),
            pl.BlockSpec((D_MODEL, PROJ_PAD), lambda i: (0, 0)),
        ],
        out_specs=pl.BlockSpec((tm, PROJ_PAD), lambda i: (i, 0)),
        out_shape=jax.ShapeDtypeStruct((t, PROJ_PAD), F32),
        compiler_params=_cparams(("parallel",)),
    )(*res, lnw, w_pad)


def _pad_cols(w):
    sizes = (256, 256, 256, 256, 256, 128, 32, 256, 768, 8, 256, 256, 256)
    pts, acc = [], 0
    for sz in sizes:
        pts.append((acc, acc + sz))
        acc += sz
    seg = lambda i: w[:, pts[i][0]:pts[i][1]]
    z = lambda n: jnp.zeros((w.shape[0], n), w.dtype)
    cols = [seg(0), seg(1), seg(2), seg(3), seg(4),
            seg(5), seg(6), z(BLK - 128 - 32),
            seg(8), seg(7), seg(9), z(BLK - 8),
            seg(10), seg(11), seg(12)]
    return jnp.concatenate(cols, axis=1)


def _ret_tables():
    pos = jnp.arange(CHUNK, dtype=F32)
    hh = jnp.arange(N_HEADS, dtype=F32)
    lg_f = jnp.log1p(-jnp.exp2(-5.0 - hh))
    lg_b = jnp.log1p(-jnp.exp2(-5.5 - hh))
    diff = pos[:, None] - pos[None, :]
    d_f = jnp.where(diff >= 0, jnp.exp(lg_f[:, None, None] * jnp.maximum(diff, 0.0)), 0.0)
    d_b = jnp.where(diff < 0, jnp.exp(lg_b[:, None, None] * jnp.maximum(-diff, 0.0)), 0.0)
    dmat = (d_f + d_b).reshape(2, 2, CHUNK, CHUNK)

    def lanes(per_head):
        t = jnp.repeat(per_head[:, :, None], HEAD_DIM, axis=2)
        return t.reshape(2, 2, CHUNK, HEAD_DIM).transpose(0, 2, 1, 3).reshape(2, CHUNK, 2 * HEAD_DIM)

    xi_f = lanes(jnp.exp(lg_f[:, None] * (pos + 1.0)))
    zeta_f = lanes(jnp.exp(lg_f[:, None] * (CHUNK - 1.0 - pos)))
    xi_b = lanes(jnp.exp(lg_b[:, None] * (CHUNK - pos)))
    zeta_b = lanes(jnp.exp(lg_b[:, None] * pos))
    blk = (jnp.arange(128)[:, None] // HEAD_DIM) == (jnp.arange(128)[None, :] // HEAD_DIM)

    def cdec(lg):
        g = jnp.exp(lg * CHUNK).reshape(2, 2)
        rows = jnp.repeat(g, HEAD_DIM, axis=1)
        return jnp.where(blk[None], rows[:, :, None], 0.0)

    avg = jnp.where(blk, 1.0 / HEAD_DIM, 0.0).astype(BF16)
    return dmat, xi_f, zeta_f, xi_b, zeta_b, cdec(lg_f), cdec(lg_b), avg


def _ret_kernel(q_ref, k_ref, v_ref, g_ref, c_ref, s1_ref, s2_ref, dmat_ref, xif_ref, zf_ref,
                xib_ref, zb_ref, cdf_ref, cdb_ref, avg_ref, o_ref, sf_ref, sb_ref, pb_ref, *, nsteps, g):
    t = pl.program_id(1)

    @pl.when(t == 0)
    def _():
        sf_ref[...] = jnp.zeros_like(sf_ref)
        sb_ref[...] = jnp.zeros_like(sb_ref)

    def rotated(ref, rows):
        wide = lambda t: jnp.concatenate([t[rows, :]] * 2, axis=1)
        return _rope(ref[rows, :], wide(c_ref), wide(s1_ref), wide(s2_ref), HEAD_DIM // 2)

    def state_update(s_ref, p, k, vb, zeta, cdec):
        kz = (k * zeta).T.astype(BF16)
        blockmask = jnp.where(cdec > 0.0, 1.0, 0.0)
        s_ref[p] = s_ref[p] * cdec + _dot(kz, vb) * blockmask

    @pl.when(t < nsteps)
    def _():
        base = (nsteps - 1 - t) * g
        for ci in reversed(range(g)):
            rows = slice(ci * CHUNK, (ci + 1) * CHUNK)
            k = rotated(k_ref, rows)
            vb = v_ref[rows, :].astype(BF16)
            for p in range(2):
                sl = slice(p * 128, (p + 1) * 128)
                pb_ref[base + ci, p] = sb_ref[p].astype(BF16)
                state_update(sb_ref, p, k[:, sl], vb[:, sl], zb_ref[p], cdb_ref[p])

    @pl.when(t >= nsteps)
    def _():
        base = (t - nsteps) * g
        head = _lane_head((CHUNK, 128), HEAD_DIM)
        avg = avg_ref[...]

        def group_mean(x):
            hi, lo = _split2(x)
            return _dot(hi, avg) + _dot(lo, avg)

        for ci in range(g):
            rows = slice(ci * CHUNK, (ci + 1) * CHUNK)
            k_all = rotated(k_ref, rows)
            q_all = rotated(q_ref, rows) * (HEAD_DIM ** -0.5)
            vb_all = v_ref[rows, :].astype(BF16)
            for p in range(2):
                sl = slice(p * 128, (p + 1) * 128)
                q, k, vb = q_all[:, sl], k_all[:, sl], vb_all[:, sl]
                kb = k.astype(BF16)
                inner = None
                for hh in range(2):
                    qh = jnp.where(head == hh, q, 0.0).astype(BF16)
                    pm = (_dot_nt(qh, kb) * dmat_ref[p, hh]).astype(BF16)
                    oh = _dot(pm, vb)
                    inner = oh if inner is None else jnp.where(head == hh, oh, inner)
                cross_f = _dot((q * xif_ref[p]).astype(BF16), sf_ref[p].astype(BF16))
                cross_b = _dot((q * xib_ref[p]).astype(BF16), pb_ref[base + ci, p])
                o = inner + cross_f + cross_b
                mu = group_mean(o)
                d = o - mu
                var = group_mean(d * d)
                o_ref[rows, sl] = _silu(g_ref[rows, sl]) * (d * lax.rsqrt(var + GN_EPS))
                state_update(sf_ref, p, k, vb, zf_ref[p], cdf_ref[p])


def _retention(proj, tables, rope_tabs):
    b, s, _ = proj.shape
    nc = s // CHUNK
    g = math.gcd(nc, 4)
    nsteps = nc // g
    rows = g * CHUNK
    both = lambda t: jnp.where(t < nsteps, nsteps - 1 - t, t - nsteps)
    late = lambda t: jnp.where(t < nsteps, 0, t - nsteps)
    col = lambda base, idx: pl.BlockSpec((None, rows, BLK), lambda bi, t: (bi, idx(t), base))
    tab = pl.BlockSpec((rows, 128), lambda bi, t: (both(t), 0))
    pair = pl.BlockSpec((2, CHUNK, 128), lambda bi, t: (0, 0, 0))
    return pl.pallas_call(
        functools.partial(_ret_kernel, nsteps=nsteps, g=g),
        grid=(b, 2 * nsteps),
        in_specs=[
            col(COL_RQ, late), col(COL_RK, both), col(COL_RV, both), col(COL_RG, late),
            tab, tab, tab,
            pl.BlockSpec((2, 2, CHUNK, CHUNK), lambda bi, t: (0, 0, 0, 0)),
            pair, pair, pair, pair, pair, pair,
            pl.BlockSpec((128, 128), lambda bi, t: (0, 0)),
        ],
        out_specs=pl.BlockSpec((None, rows, BLK), lambda bi, t: (bi, late(t), 0)),
        out_shape=jax.ShapeDtypeStruct((b, s, BLK), F32),
        scratch_shapes=[pltpu.VMEM((2, 128, 128), F32), pltpu.VMEM((2, 128, 128), F32),
                        pltpu.VMEM((nc, 2, 128, 128), BF16)],
        compiler_params=_cparams(("parallel", "arbitrary")),
    )(proj, proj, proj, proj, *rope_tabs, *tables)


def _mla_prep_kernel(cq_ref, kv_ref, qnw_ref, kvnw_ref, wq_ref, wk_ref, wv_ref, place_ref,
                     qc_ref, qs1_ref, qs2_ref, kc_ref, ks1_ref, ks2_ref, q_ref, k_ref, v_ref):
    cqn = _rms(cq_ref[...], qnw_ref[...])
    q = _dot(cqn.astype(BF16), wq_ref[...])
    per_head = lambda r: jnp.concatenate([r[...]] * N_HEADS, axis=1)
    q = _rope(q, per_head(qc_ref), per_head(qs1_ref), per_head(qs2_ref), MLA_ROPE // 2)
    q_ref[...] = (q * ((MLA_NOPE + MLA_ROPE) ** -0.5 * math.log2(math.e))).astype(BF16)
    blk = kv_ref[...]
    ckvn = _rms(blk[:, :MLA_KV_RANK], kvnw_ref[...]).astype(BF16)
    kr = _rope(blk[:, MLA_KV_RANK:], kc_ref[...], ks1_ref[...], ks2_ref[...], MLA_ROPE // 2)
    k_ref[...] = (_dot(ckvn, wk_ref[...]) + _dot(kr.astype(BF16), place_ref[...])).astype(BF16)
    v = _dot(ckvn, wv_ref[...])
    ones = (lax.broadcasted_iota(jnp.int32, v.shape, 1) % 128) >= HEAD_DIM
    v_ref[...] = jnp.where(ones, 1.0, v).astype(BF16)


def _mla_weights(w_uq, w_ukv):
    qh = w_uq.reshape(MLA_Q_RANK, N_HEADS, MLA_NOPE + MLA_ROPE)
    wq = jnp.concatenate([qh, jnp.zeros((MLA_Q_RANK, N_HEADS, 32), F32)], axis=-1).reshape(MLA_Q_RANK, 512)
    kvh = w_ukv.reshape(MLA_KV_RANK, N_HEADS, MLA_NOPE + HEAD_DIM)
    zk = jnp.zeros((MLA_KV_RANK, N_HEADS, 64), F32)
    wk = jnp.concatenate([kvh[..., :MLA_NOPE], zk], axis=-1).reshape(MLA_KV_RANK, 512)
    wv = jnp.concatenate([kvh[..., MLA_NOPE:], zk], axis=-1).reshape(MLA_KV_RANK, 512)
    src = jnp.arange(128)[:, None]
    dst = jnp.arange(512)[None, :]
    place = ((dst % 128 == src + MLA_NOPE) & (src < MLA_ROPE)).astype(BF16)
    return wq.astype(BF16), wk.astype(BF16), wv.astype(BF16), place


def _mla_flash_kernel(q_ref, k_ref, v_ref, o_ref, m_ref, acc_ref, *, strip):
    ki = pl.program_id(2)

    @pl.when(ki == 0)
    def _():
        m_ref[...] = jnp.full_like(m_ref, -jnp.inf)
        acc_ref[...] = jnp.zeros_like(acc_ref)

    for h in range(N_HEADS):
        sl = slice(h * 128, (h + 1) * 128)
        k = k_ref[:, sl]
        v = v_ref[:, sl]
        for r in range(q_ref.shape[0] // strip):
            rows = slice(r * strip, (r + 1) * strip)
            s = _dot_nt(q_ref[rows, sl], k)
            m_prev = m_ref[h, rows]
            m_new = jnp.maximum(m_prev, jnp.max(s, axis=-1, keepdims=True))
            p = jnp.exp2(s - m_new[:, :1])
            acc_ref[h, rows] = jnp.exp2(m_prev - m_new) * acc_ref[h, rows] + _dot(p.astype(BF16), v)
            m_ref[h, rows] = m_new

    @pl.when(ki == pl.num_programs(2) - 1)
    def _():
        outs = []
        for h in range(N_HEADS):
            a = acc_ref[h]
            outs.append(a[:, :HEAD_DIM] / a[:, HEAD_DIM:])
        o_ref[...] = jnp.concatenate(outs, axis=-1)


def _mla(proj, q_norm_w, kv_norm_w, w_uq, w_ukv, q_tabs, k_tabs):
    b, s, _ = proj.shape
    wq, wk, wv, place = _mla_weights(w_uq, w_ukv)
    ts = 512
    row = lambda n: pl.BlockSpec((ts, n), lambda bi, i: (i, 0))
    full = lambda r, n: pl.BlockSpec((r, n), lambda bi, i: (0, 0))
    out = pl.BlockSpec((None, ts, 512), lambda bi, i: (bi, i, 0))
    q, k, v = pl.pallas_call(
        _mla_prep_kernel,
        grid=(b, s // ts),
        in_specs=[
            pl.BlockSpec((None, ts, BLK), lambda bi, i: (bi, i, COL_MCQ)),
            pl.BlockSpec((None, ts, BLK), lambda bi, i: (bi, i, COL_MKV)),
            full(1, MLA_Q_RANK), full(1, MLA_KV_RANK),
            full(MLA_Q_RANK, 512), full(MLA_KV_RANK, 512), full(MLA_KV_RANK, 512), full(128, 512),
            row(128), row(128), row(128), row(128), row(128), row(128),
        ],
        out_specs=[out, out, out],
        out_shape=[jax.ShapeDtypeStruct((b, s, 512), BF16)] * 3,
        compiler_params=_cparams(("parallel", "parallel")),
    )(proj, proj, q_norm_w[None], kv_norm_w[None], wq, wk, wv, place, *q_tabs, *k_tabs)

    tq, tk = math.gcd(s, 2048), math.gcd(s, 512)
    return pl.pallas_call(
        functools.partial(_mla_flash_kernel, strip=tq),
        grid=(b, s // tq, s // tk),
        in_specs=[
            pl.BlockSpec((None, tq, 512), lambda bi, qi, ki: (bi, qi, 0)),
            pl.BlockSpec((None, tk, 512), lambda bi, qi, ki: (bi, ki, 0)),
            pl.BlockSpec((None, tk, 512), lambda bi, qi, ki: (bi, ki, 0)),
        ],
        out_specs=pl.BlockSpec((None, tq, N_HEADS * HEAD_DIM), lambda bi, qi, ki: (bi, qi, 0)),
        out_shape=jax.ShapeDtypeStruct((b, s, N_HEADS * HEAD_DIM), F32),
        scratch_shapes=[pltpu.VMEM((N_HEADS, tq, 128), F32), pltpu.VMEM((N_HEADS, tq, 128), F32)],
        compiler_params=_cparams(("parallel", "parallel", "arbitrary")),
    )(q, k, v)


def _conv_kernel(prev_ref, cur_ref, next_ref, w_ref, b_ref, o_ref, *, ts):
    i = pl.program_id(1)
    w = w_ref[...]

    def conv(x):
        n = x.shape[0]
        acc = x * w[2:3]
        for s in (-2, -1, 1, 2):
            acc = acc + pltpu.roll(x, (-s) % n, 0) * w[s + 2:s + 3]
        return acc

    act = lambda y: _silu(y + b_ref[...])
    cur = cur_ref[...]
    prev = prev_ref[...] * (i > 0).astype(F32)
    nxt = next_ref[...] * (i < pl.num_programs(1) - 1).astype(F32)
    o_ref[...] = act(conv(cur))
    top = conv(jnp.concatenate([prev, cur[:16]], axis=0))
    o_ref[0:8, :] = act(top[8:16])
    bot = conv(jnp.concatenate([cur[ts - 16:], nxt], axis=0))
    o_ref[ts - 8:ts, :] = act(bot[8:16])


def _ssd_conv(proj, conv_w, conv_b):
    b, s, _ = proj.shape
    ts = 512
    width = 3 * BLK
    cb = COL_XBC * BLK // width
    nb8 = ts // 8
    w8 = jnp.concatenate([conv_w, jnp.zeros((8 - SSM_CONV, width), F32)], axis=0)
    return pl.pallas_call(
        functools.partial(_conv_kernel, ts=ts),
        grid=(b, s // ts),
        in_specs=[
            pl.BlockSpec((None, 8, width), lambda bi, i: (bi, jnp.maximum(i * nb8 - 1, 0), cb)),
            pl.BlockSpec((None, ts, width), lambda bi, i: (bi, i, cb)),
            pl.BlockSpec((None, 8, width), lambda bi, i: (bi, jnp.minimum((i + 1) * nb8, s // 8 - 1), cb)),
            pl.BlockSpec((8, width), lambda bi, i: (0, 0)),
            pl.BlockSpec((1, width), lambda bi, i: (0, 0)),
        ],
        out_specs=pl.BlockSpec((None, ts, width), lambda bi, i: (bi, i, 0)),
        out_shape=jax.ShapeDtypeStruct((b, s, width), F32),
        compiler_params=_cparams(("parallel", "parallel")),
    )(proj, proj, proj, w8, conv_b[None])


def _ssd_tables():
    tril = _tri(CHUNK, lambda i, j: j <= i)
    triu = _tri(CHUNK, lambda i, j: j >= i)
    h = jnp.arange(128)[:, None]
    lane = jnp.arange(256)[None, :]
    e_f = ((lane // HEAD_DIM == h) & (h < N_HEADS)).astype(BF16)
    e_b = ((lane // HEAD_DIM == h - N_HEADS) & (h >= N_HEADS) & (h < 2 * N_HEADS)).astype(BF16)
    return tril, triu, e_f, e_b


def _ssd_kernel(xs_ref, bm_ref, cm_ref, z_ref, dt_ref, dtt_ref, bias_ref, a_ref, biasc_ref, ac_ref,
                dskip_ref, nw_ref, tril_ref, triu_ref, ef_ref, eb_ref, o_ref,
                sf_ref, sb_ref, pb_ref, *, nsteps, g):
    t = pl.program_id(1)
    tril, triu = tril_ref[...], triu_ref[...]

    @pl.when(t == 0)
    def _():
        sf_ref[...] = jnp.zeros_like(sf_ref)
        sb_ref[...] = jnp.zeros_like(sb_ref)

    def step_sizes(rows):
        dt = _softplus(dt_ref[rows, :128] + bias_ref[...])
        return dt, dt * a_ref[...]

    def backward_terms(xs, dt, dta):
        rcs_b = _dot_x_exact(_dot_exact_x(triu, dta), eb_ref[...])
        return rcs_b, rcs_b[0:1, :], xs * _dot_x_exact(dt, eb_ref[...])

    def state_update(s_ref, bmf, weighted_x, total):
        for gi in range(SSM_GROUPS):
            sl = slice(gi * 128, (gi + 1) * 128)
            upd = _dot(bmf[:, sl].T.astype(BF16), weighted_x[:, sl].astype(BF16))
            s_ref[gi] = s_ref[gi] * jnp.exp(total[:, sl]) + upd

    @pl.when(t < nsteps)
    def _():
        base = (nsteps - 1 - t) * g
        for ci in reversed(range(g)):
            rows = slice(ci * CHUNK, (ci + 1) * CHUNK)
            xs = xs_ref[rows, :]
            dt, dta = step_sizes(rows)
            rcs_b, tot_b, xdt_b = backward_terms(xs, dt, dta)
            pb_ref[base + ci] = sb_ref[...].astype(BF16)
            state_update(sb_ref, bm_ref[rows, :], jnp.exp(tot_b - rcs_b) * xdt_b, tot_b)

    @pl.when(t >= nsteps)
    def _():
        base = (t - nsteps) * g
        row = lax.broadcasted_iota(jnp.int32, (CHUNK, CHUNK), 0)
        colm = lax.broadcasted_iota(jnp.int32, (CHUNK, CHUNK), 1)
        head = _lane_head((CHUNK, 128), HEAD_DIM)
        efb = jnp.concatenate([ef_ref[...], eb_ref[...]], axis=1)
        for ci in range(g):
            rows = slice(ci * CHUNK, (ci + 1) * CHUNK)
            xs = xs_ref[rows, :]
            bmf = bm_ref[rows, :]
            bm = bmf.astype(BF16)
            cm = cm_ref[rows, :].astype(BF16)
            dt, dta = step_sizes(rows)
            rcs_b = _dot_x_exact(_dot_exact_x(triu, dta), eb_ref[...])
            cs_f = _dot_x_exact(_dot_exact_x(tril, dta), ef_ref[...])
            tot_f = cs_f[CHUNK - 1:CHUNK, :]
            dt_exp = _dot_x_exact(dt, efb)
            xdt_f = xs * dt_exp[:, :SSM_INNER]
            xdt_b = xs * dt_exp[:, SSM_INNER:]
            dtat = _softplus(dtt_ref[:, rows] + biasc_ref[...]) * ac_ref[...]
            cst = _dot_x_exact(dtat, triu)
            rcst = _dot_x_exact(dtat, tril)
            pb = pb_ref[base + ci]
            ys = []
            for gi in range(SSM_GROUPS):
                sl = slice(gi * 128, (gi + 1) * 128)
                cb = _dot_nt(cm[:, sl], bm[:, sl])
                xcat = jnp.concatenate([xdt_f[:, sl], xdt_b[:, sl]], axis=0).astype(BF16)
                yg = None
                for hh in range(2):
                    h = 2 * gi + hh
                    c0 = h * HEAD_DIM
                    seg_f = cs_f[:, c0:c0 + 1] - cst[h:h + 1, :]
                    seg_b = rcs_b[:, c0:c0 + 1] - rcst[N_HEADS + h:N_HEADS + h + 1, :]
                    dec_f = jnp.where(row >= colm, jnp.exp(jnp.minimum(seg_f, 0.0)), 0.0)
                    dec_b = jnp.where(row < colm, jnp.exp(jnp.minimum(seg_b, 0.0)), 0.0)
                    wcat = jnp.concatenate([cb * dec_f, cb * dec_b], axis=1).astype(BF16)
                    yh = _dot(wcat, xcat)
                    yg = yh if yg is None else jnp.where(head == hh, yh, yg)
                off_f = jnp.exp(cs_f[:, sl]) * _dot(cm[:, sl], sf_ref[gi].astype(BF16))
                off_b = jnp.exp(rcs_b[:, sl]) * _dot(cm[:, sl], pb[gi])
                ys.append(yg + off_f + off_b)
            y = jnp.concatenate(ys, axis=1) + dskip_ref[...] * xs
            y = y * _silu(z_ref[rows, :])
            o_ref[rows, :] = _rms(y, nw_ref[...])
            state_update(sf_ref, bmf, jnp.exp(tot_f - cs_f) * xdt_f, tot_f)


def _ssd(proj, xbc, a_log, dt_bias, d_skip, norm_w, tables):
    b, s, _ = proj.shape
    nc = s // CHUNK
    g = math.gcd(nc, 4)
    nsteps = nc // g
    rows = g * CHUNK
    dt_t = jnp.swapaxes(proj[:, :, COL_DT * BLK:COL_DT * BLK + 8], 1, 2)
    pad_row = lambda v: jnp.concatenate([v.reshape(1, 8), jnp.zeros((1, 120), F32)], axis=1)
    a = -jnp.exp(a_log.astype(F32))
    both = lambda t: jnp.where(t < nsteps, nsteps - 1 - t, t - nsteps)
    late = lambda t: jnp.where(t < nsteps, 0, t - nsteps)
    full = lambda r, n: pl.BlockSpec((r, n), lambda bi, t: (0, 0))
    blk = lambda idx, c: pl.BlockSpec((None, rows, BLK), lambda bi, t: (bi, idx(t), c))
    return pl.pallas_call(
        functools.partial(_ssd_kernel, nsteps=nsteps, g=g),
        grid=(b, 2 * nsteps),
        in_specs=[
            blk(both, 0), blk(both, 1), blk(late, 2), blk(late, COL_Z), blk(both, COL_DT),
            pl.BlockSpec((None, 8, rows), lambda bi, t: (bi, 0, late(t))),
            full(1, 128), full(1, 128), full(8, 1), full(8, 1),
            full(1, SSM_INNER), full(1, SSM_INNER),
            full(CHUNK, CHUNK), full(CHUNK, CHUNK), full(128, 256), full(128, 256),
        ],
        out_specs=blk(late, 0),
        out_shape=jax.ShapeDtypeStruct((b, s, SSM_INNER), F32),
        scratch_shapes=[pltpu.VMEM((SSM_GROUPS, 128, 128), F32), pltpu.VMEM((SSM_GROUPS, 128, 128), F32),
                        pltpu.VMEM((nc, SSM_GROUPS, 128, 128), BF16)],
        compiler_params=_cparams(("parallel", "arbitrary")),
    )(xbc, xbc, xbc, proj, proj, dt_t, pad_row(dt_bias), pad_row(a), dt_bias.reshape(8, 1), a.reshape(8, 1),
      jnp.repeat(d_skip, HEAD_DIM)[None], norm_w[None], *tables)


def _dil_prep_kernel(q_ref, k_ref, c_ref, s1_ref, s2_ref, qo_ref, ko_ref):
    c, s1, s2 = (jnp.concatenate([r[...]] * 2, axis=1) for r in (c_ref, s1_ref, s2_ref))
    qo_ref[...] = _rope(q_ref[...], c, s1, s2, ROPE_DIM // 2) * (HEAD_DIM ** -0.5)
    ko_ref[...] = _rope(k_ref[...], c, s1, s2, ROPE_DIM // 2)


def _dil_kernel(q_ref, k_ref, v_ref, o_ref, m_ref, l_ref, a_ref, *, s):
    head = _lane_head((128, 128), HEAD_DIM)
    kw = 2 * 128
    for pi, (win, d) in enumerate(DIL_PATTERNS):
        half = win // (2 * d)
        seg = s // d
        per_seg = seg // 128

        def body(i, carry, d=d, pi=pi, half=half, seg=seg, per_seg=per_seg):
            r = i // per_seg
            m0 = (i % per_seg) * 128
            ks = jnp.clip(m0 - half, 0, seg - kw)
            if d == 1:
                qrows = pl.ds(pl.multiple_of(m0, 128), 128)
                krows = pl.ds(pl.multiple_of(ks, 64), kw)
            else:
                qrows = pl.ds(r + d * m0, 128, stride=d)
                krows = pl.ds(r + d * ks, kw, stride=d)
            q = q_ref[qrows, :]
            kb = k_ref[krows, :].astype(BF16)
            vb = v_ref[krows, :].astype(BF16)
            qpos = m0 + lax.broadcasted_iota(jnp.int32, (128, kw), 0)
            kpos = ks + lax.broadcasted_iota(jnp.int32, (128, kw), 1)
            valid = jnp.abs(qpos - kpos) <= half
            mb = lb = pv = None
            for hh in range(2):
                sc = _dot_nt(jnp.where(head == hh, q, 0.0).astype(BF16), kb)
                sc = jnp.where(valid, sc, -jnp.inf)
                mh = jnp.max(sc, axis=-1, keepdims=True)
                p = jnp.exp(sc - mh)
                lh = jnp.sum(p, axis=-1, keepdims=True)
                ph = _dot(p.astype(BF16), vb)
                if hh == 0:
                    mb = jnp.broadcast_to(mh, (128, 128))
                    lb = jnp.broadcast_to(lh, (128, 128))
                    pv = ph
                else:
                    mb = jnp.where(head == hh, mh, mb)
                    lb = jnp.where(head == hh, lh, lb)
                    pv = jnp.where(head == hh, ph, pv)
            m_ref.at[pi][qrows, :] = mb
            l_ref.at[pi][qrows, :] = lb
            a_ref.at[pi][qrows, :] = pv
            return carry

        lax.fori_loop(0, s // 128, body, 0, unroll=4)

    def merge(i, carry):
        rows = pl.ds(pl.multiple_of(i * 512, 512), 512)
        ms = [m_ref[pi, rows, :] for pi in range(len(DIL_PATTERNS))]
        top = functools.reduce(jnp.maximum, ms)
        ws = [jnp.exp(m - top) for m in ms]
        den = sum(w * l_ref[pi, rows, :] for pi, w in enumerate(ws))
        num = sum(w * a_ref[pi, rows, :] for pi, w in enumerate(ws))
        o_ref[rows, :] = num / den
        return carry

    lax.fori_loop(0, s // 512, merge, 0)


def _dilated(proj, tabs):
    b, s, _ = proj.shape
    assert all(s // d >= 256 for _, d in DIL_PATTERNS)
    ts = 512
    rowt = pl.BlockSpec((ts, 128), lambda bi, i: (i, 0))
    outb = pl.BlockSpec((None, ts, BLK), lambda bi, i: (bi, i, 0))
    qr, kr = pl.pallas_call(
        _dil_prep_kernel,
        grid=(b, s // ts),
        in_specs=[
            pl.BlockSpec((None, ts, BLK), lambda bi, i: (bi, i, COL_DQ)),
            pl.BlockSpec((None, ts, BLK), lambda bi, i: (bi, i, COL_DK)),
            rowt, rowt, rowt,
        ],
        out_specs=[outb, outb],
        out_shape=[jax.ShapeDtypeStruct((b, s, BLK), F32)] * 2,
        compiler_params=_cparams(("parallel", "parallel")),
    )(proj, proj, *tabs)
    seqb = lambda c: pl.BlockSpec((None, s, 128), lambda bi, p: (bi, 0, 2 * c + p))
    return pl.pallas_call(
        functools.partial(_dil_kernel, s=s),
        grid=(b, 2),
        in_specs=[seqb(0), seqb(0), seqb(COL_DV)],
        out_specs=seqb(0),
        out_shape=jax.ShapeDtypeStruct((b, s, BLK), F32),
        scratch_shapes=[pltpu.VMEM((len(DIL_PATTERNS), s, 128), F32)] * 3,
        compiler_params=_cparams(("parallel", "parallel")),
    )(qr, kr, proj)


def _outproj_kernel(*refs, n_res):
    ya_ref, yb_ref, yc_ref, yd_ref, w_ref, lnw_ref, rw_ref, xo_ref, h_ref, aff_ref = refs[n_res:]
    acc = refs[0][...]
    for r in refs[1:n_res]:
        acc = acc + r[...]
    mixed = jnp.concatenate([r[...].astype(BF16) for r in (ya_ref, yb_ref, yc_ref, yd_ref)], axis=1)
    acc = acc + _dot(mixed, w_ref[...])
    xo_ref[...] = acc
    h = _rms(acc, lnw_ref[...])
    h_hi = h.astype(BF16)
    h_ref[...] = h_hi
    h_lo = (h - h_hi.astype(F32)).astype(BF16)
    rw = rw_ref[...]
    w_hi = rw.astype(BF16)
    w_lo = (rw - w_hi.astype(F32)).astype(BF16)
    logits = _dot(h_hi, w_hi) + _dot(h_hi, w_lo) + _dot(h_lo, w_hi) + _dot(h_lo, w_lo)
    lane = lax.broadcasted_iota(jnp.int32, logits.shape, 1)
    logits = jnp.where(lane < N_EXPERTS, logits, -jnp.inf)
    e = jnp.exp(logits - jnp.max(logits, axis=-1, keepdims=True))
    aff = e / jnp.sum(e, axis=-1, keepdims=True)
    aff_ref[...] = aff.T[:N_EXPERTS, :]


def _outproj(res, ys, w_out, ln2_w, router_w):
    t = res[0].shape[0]
    tm = 256
    rw = jnp.concatenate([router_w, jnp.zeros((D_MODEL, 128 - N_EXPERTS), F32)], axis=1)
    row = lambda n: pl.BlockSpec((tm, n), lambda i: (i, 0))
    full = lambda r, n: pl.BlockSpec((r, n), lambda i: (0, 0))
    return pl.pallas_call(
        functools.partial(_outproj_kernel, n_res=len(res)),
        grid=(t // tm,),
        in_specs=[row(D_MODEL)] * len(res) + [row(BLK)] * 4 + [
            full(D_MODEL, D_MODEL), full(1, D_MODEL), full(D_MODEL, 128)],
        out_specs=[row(D_MODEL), row(D_MODEL), pl.BlockSpec((N_EXPERTS, tm), lambda i: (0, i))],
        out_shape=[jax.ShapeDtypeStruct((t, D_MODEL), F32),
                   jax.ShapeDtypeStruct((t, D_MODEL), BF16),
                   jax.ShapeDtypeStruct((N_EXPERTS, t), F32)],
        compiler_params=_cparams(("parallel",)),
    )(*res, *ys, w_out.astype(BF16), ln2_w[None], rw)


def _exclusive_prefix(x, tri):
    n = x.shape[1] // 128
    off = jnp.zeros((x.shape[0], 1), F32)
    parts = []
    for i in range(n):
        xt = x[:, i * 128:(i + 1) * 128]
        incl = _dot(xt.astype(BF16), tri)
        parts.append(incl - xt + off)
        off = off + incl[:, 127:128]
    return jnp.concatenate(parts, axis=1)


def _select_kernel(aff_ref, tri_ref, key_ref, *, cap):
    a = aff_ref[...]
    n_exp = a.shape[0]
    thr_bits = jnp.zeros((n_exp, 1), jnp.int32)
    for bit in range(30, -1, -1):
        cand = thr_bits | (1 << bit)
        cnt = jnp.sum(jnp.where(a >= lax.bitcast_convert_type(cand, F32), 1.0, 0.0), axis=1, keepdims=True)
        thr_bits = jnp.where(cnt >= cap, cand, thr_bits)
    thr = lax.bitcast_convert_type(thr_bits, F32)
    above = jnp.where(a > thr, 1.0, 0.0)
    tie = jnp.where(a == thr, 1.0, 0.0)
    need = cap - jnp.sum(above, axis=1, keepdims=True)
    tri = tri_ref[...]
    sel = above + tie * jnp.where(_exclusive_prefix(tie, tri) < need, 1.0, 0.0)
    key_ref[...] = jnp.where(sel > 0.0, _exclusive_prefix(sel, tri), -1.0)


def _slot_kernel(key_ref, aff_ref, ltri_ref, idx_ref, gate_ref, *, cap, nt):
    n_exp = key_ref.shape[0]
    pad = jnp.zeros((128 - nt, 128), F32)
    slot = lax.broadcasted_iota(jnp.int32, (128, cap), 1).astype(F32)
    sub = lax.broadcasted_iota(jnp.int32, (128, cap), 0).astype(F32)
    ltri = ltri_ref[...]

    def per_expert(e, carry):
        key = jnp.concatenate([key_ref[e], pad - 1.0], axis=0)
        aff = jnp.concatenate([aff_ref[e], pad], axis=0)
        cnt = jnp.sum(jnp.where(key >= 0.0, 1.0, 0.0), axis=1, keepdims=True)
        cum = _dot(ltri, jnp.broadcast_to(cnt, (128, 128)).astype(BF16))
        in_tile = (jnp.where(slot >= (cum - cnt)[:, :1], 1.0, 0.0)
                   * jnp.where(slot < cum[:, :1], 1.0, 0.0))
        g = in_tile.astype(BF16)
        k_hi, k_lo = _split2(key.T)
        slot_of = _dot(k_hi, g) + _dot(k_lo, g)
        hit = jnp.where(slot_of == slot, 1.0, 0.0)
        aff_t = aff.T
        a_hi = aff_t.astype(BF16)
        r1 = aff_t - a_hi.astype(F32)
        a_mid = r1.astype(BF16)
        a_lo = (r1 - a_mid.astype(F32)).astype(BF16)
        aff_of = _dot(a_hi, g) + _dot(a_mid, g) + _dot(a_lo, g)
        tile = jnp.sum(in_tile * sub, axis=0, keepdims=True)
        within = jnp.sum(hit * sub, axis=0, keepdims=True)
        idx_ref[pl.ds(e, 1), :] = (128.0 * tile + within).astype(jnp.int32)
        gate_ref[pl.ds(e, 1), :] = jnp.sum(hit * aff_of, axis=0, keepdims=True)
        return carry

    lax.fori_loop(0, n_exp, per_expert, 0, unroll=4)


def _moe_select(aff_t, b, s, cap):
    nt = s // 128
    assert nt <= 128 and nt % 8 == 0 and cap % 128 == 0
    key = pl.pallas_call(
        functools.partial(_select_kernel, cap=cap),
        grid=(b,),
        in_specs=[pl.BlockSpec((N_EXPERTS, s), lambda bi: (0, bi)),
                  pl.BlockSpec((128, 128), lambda bi: (0, 0))],
        out_specs=pl.BlockSpec((None, N_EXPERTS, s), lambda bi: (bi, 0, 0)),
        out_shape=jax.ShapeDtypeStruct((b, N_EXPERTS, s), F32),
        compiler_params=_cparams(("parallel",)),
    )(aff_t, _tri(128, lambda i, j: i <= j))
    out = pl.BlockSpec((None, N_EXPERTS, cap), lambda bi: (bi, 0, 0))
    return pl.pallas_call(
        functools.partial(_slot_kernel, cap=cap, nt=nt),
        grid=(b,),
        in_specs=[pl.BlockSpec((None, N_EXPERTS, nt, 128), lambda bi: (bi, 0, 0, 0)),
                  pl.BlockSpec((N_EXPERTS, nt, 128), lambda bi: (0, bi, 0)),
                  pl.BlockSpec((128, 128), lambda bi: (0, 0))],
        out_specs=[out, out],
        out_shape=[jax.ShapeDtypeStruct((b, N_EXPERTS, cap), jnp.int32),
                   jax.ShapeDtypeStruct((b, N_EXPERTS, cap), F32)],
        compiler_params=_cparams(("parallel",)),
    )(key.reshape(b, N_EXPERTS, nt, 128), aff_t.reshape(N_EXPERTS, b * nt, 128), _tri(128, lambda i, j: j <= i))


ROWS_PER_ITER = 8


def _gather_kernel(idx_ref, h_ref, o_ref, hf_ref, buf_ref, *, cap, n_exp):
    e = pl.program_id(1)
    base = (pl.program_id(0) * n_exp + e) * cap

    @pl.when(e == 0)
    def _():
        hf_ref[...] = h_ref[...].astype(F32)

    def body(jb, carry):
        j0 = pl.multiple_of(jb * ROWS_PER_ITER, ROWS_PER_ITER)
        vals = [hf_ref[pl.ds(idx_ref[base + j0 + u], 1), :] for u in range(ROWS_PER_ITER)]
        for u in range(ROWS_PER_ITER):
            buf_ref[pl.ds(j0 + u, 1), :] = vals[u]
        return carry

    lax.fori_loop(0, cap // ROWS_PER_ITER, body, 0)
    o_ref[...] = buf_ref[...].astype(BF16)


def _moe_gather(idx_flat, h, cap):
    b, s, d = h.shape
    return pl.pallas_call(
        functools.partial(_gather_kernel, cap=cap, n_exp=N_EXPERTS),
        grid_spec=pltpu.PrefetchScalarGridSpec(
            num_scalar_prefetch=1,
            grid=(b, N_EXPERTS),
            in_specs=[pl.BlockSpec((None, s, d), lambda bi, e, idx: (bi, 0, 0))],
            out_specs=pl.BlockSpec((None, None, cap, d), lambda bi, e, idx: (e, bi, 0, 0)),
            scratch_shapes=[pltpu.VMEM((s, d), F32), pltpu.VMEM((cap, d), F32)],
        ),
        out_shape=jax.ShapeDtypeStruct((N_EXPERTS, b, cap, d), BF16),
        compiler_params=_cparams(("parallel", "arbitrary")),
    )(idx_flat, h)


def _ffn_kernel(x_ref, wg_ref, wu_ref, wd_ref, o_ref, *, rows, tr):
    f = pl.program_id(1)
    wg = wg_ref[...].astype(BF16)
    wu = wu_ref[...].astype(BF16)
    wd = wd_ref[...].astype(BF16)

    def partial_out(sl):
        x = x_ref[sl, :]
        hid = (_silu(_dot(x, wg)) * _dot(x, wu)).astype(BF16)
        return _dot(hid, wd)

    for r in range(rows // tr):
        sl = slice(r * tr, (r + 1) * tr)

        @pl.when(f == 0)
        def _():
            o_ref[sl, :] = partial_out(sl)

        @pl.when(f > 0)
        def _():
            o_ref[sl, :] += partial_out(sl)


def _moe_ffn(xin, w_gate, w_up, w_down, layer):
    n_exp, rows, d = xin.shape
    tf = 512
    return pl.pallas_call(
        functools.partial(_ffn_kernel, rows=rows, tr=math.gcd(rows, 1024)),
        grid=(n_exp, EXPERT_FF // tf),
        in_specs=[
            pl.BlockSpec((None, rows, d), lambda e, f: (e, 0, 0)),
            pl.BlockSpec((None, None, d, tf), lambda e, f: (layer, e, 0, f)),
            pl.BlockSpec((None, None, d, tf), lambda e, f: (layer, e, 0, f)),
            pl.BlockSpec((None, None, tf, d), lambda e, f: (layer, e, f, 0)),
        ],
        out_specs=pl.BlockSpec((None, rows, d), lambda e, f: (e, 0, 0)),
        out_shape=jax.ShapeDtypeStruct((n_exp, rows, d), F32),
        compiler_params=_cparams(("parallel", "arbitrary")),
    )(xin, w_gate, w_up, w_down)


def _combine_kernel(idx_ref, gate_ref, y_ref, o_ref, *, cap, n_exp):
    e = pl.program_id(1)
    base = (pl.program_id(0) * n_exp + e) * cap

    @pl.when(e == 0)
    def _():
        o_ref[...] = jnp.zeros_like(o_ref)

    def body(jb, carry):
        j0 = pl.multiple_of(jb * ROWS_PER_ITER, ROWS_PER_ITER)
        rows = [idx_ref[base + j0 + u] for u in range(ROWS_PER_ITER)]
        vals = [o_ref[pl.ds(rows[u], 1), :] + gate_ref[base + j0 + u] * y_ref[pl.ds(j0 + u, 1), :]
                for u in range(ROWS_PER_ITER)]
        for u in range(ROWS_PER_ITER):
            o_ref[pl.ds(rows[u], 1), :] = vals[u]
        return carry

    lax.fori_loop(0, cap // ROWS_PER_ITER, body, 0)


def _moe_combine(idx_flat, gate_flat, y, b, s, cap):
    d = y.shape[-1]
    return pl.pallas_call(
        functools.partial(_combine_kernel, cap=cap, n_exp=N_EXPERTS),
        grid_spec=pltpu.PrefetchScalarGridSpec(
            num_scalar_prefetch=2,
            grid=(b, N_EXPERTS),
            in_specs=[pl.BlockSpec((None, None, cap, d), lambda bi, e, idx, gate: (e, bi, 0, 0))],
            out_specs=pl.BlockSpec((None, s, d), lambda bi, e, idx, gate: (bi, 0, 0)),
        ),
        out_shape=jax.ShapeDtypeStruct((b, s, d), F32),
        compiler_params=_cparams(("parallel", "arbitrary")),
    )(idx_flat, gate_flat, y)


def _final_norm_kernel(xa_ref, xb_ref, w_ref, o_ref):
    o_ref[...] = _rms(xa_ref[...] + xb_ref[...], w_ref[...])


def _final_norm(xa, xb, w):
    t = xa.shape[0]
    tm = 1024
    row = pl.BlockSpec((tm, D_MODEL), lambda i: (i, 0))
    return pl.pallas_call(
        _final_norm_kernel,
        grid=(t // tm,),
        in_specs=[row, row, pl.BlockSpec((1, D_MODEL), lambda i: (0, 0))],
        out_specs=row,
        out_shape=jax.ShapeDtypeStruct((t, D_MODEL), F32),
        compiler_params=_cparams(("parallel",)),
    )(xa, xb, w[None])


def _moe(h, aff, w_gate, w_up, w_down, layer, b, s):
    d = h.shape[-1]
    cap = EC_CAPACITY_FACTOR * s // N_EXPERTS
    token_idx, gate = _moe_select(aff, b, s, cap)
    idx_flat = token_idx.reshape(-1)
    xin = _moe_gather(idx_flat, h.reshape(b, s, d), cap)
    y = _moe_ffn(xin.reshape(N_EXPERTS, b * cap, d), w_gate, w_up, w_down, layer)
    return _moe_combine(idx_flat, gate.reshape(-1), y.reshape(N_EXPERTS, b, cap, d), b, s, cap)


def kernel(x, ln1_w, w_in, mla_q_norm_w, mla_kv_norm_w, mla_w_uq, mla_w_ukv, ssm_conv_w, ssm_conv_b,
           ssm_a_log, ssm_dt_bias, ssm_d, ssm_norm_w, w_out, ln2_w, router_w, exp_w_gate, exp_w_up,
           exp_w_down, final_norm_w):
    b, s, d = x.shape
    depth = w_in.shape[0]
    ret_tabs = _ret_tables()
    ret_rope = _rope_tables(s, HEAD_DIM, RET_THETA, HEAD_DIM, 0, 2)
    mla_q_rope = _rope_tables(s, MLA_ROPE, ROPE_THETA, 128, MLA_NOPE, 1)
    mla_k_rope = _rope_tables(s, MLA_ROPE, ROPE_THETA, 128, 0, 1)
    dil_rope = _rope_tables(s, ROPE_DIM, ROPE_THETA, HEAD_DIM, 0, 2)
    ssd_tabs = _ssd_tables()
    res = [x.reshape(b * s, d)]
    for i in range(depth):
        proj = _inproj(res, ln1_w[i][None], _pad_cols(w_in[i]).astype(BF16)).reshape(b, s, PROJ_PAD)
        y_a = _retention(proj, ret_tabs, ret_rope)
        y_b = _mla(proj, mla_q_norm_w[i], mla_kv_norm_w[i], mla_w_uq[i], mla_w_ukv[i], mla_q_rope, mla_k_rope)
        xbc = _ssd_conv(proj, ssm_conv_w[i], ssm_conv_b[i])
        y_c = _ssd(proj, xbc, ssm_a_log[i], ssm_dt_bias[i], ssm_d[i], ssm_norm_w[i], ssd_tabs)
        y_d = _dilated(proj, dil_rope)
        ys = [y.reshape(b * s, BLK) for y in (y_a, y_b, y_c, y_d)]
        x_mid, h, aff = _outproj(res, ys, w_out[i], ln2_w[i], router_w[i])
        moe = _moe(h, aff, exp_w_gate, exp_w_up, exp_w_down, i, b, s)
        res = [x_mid, moe.reshape(b * s, d)]
    return _final_norm(res[0], res[1], final_norm_w).reshape(b, s, d)
```

```python
import functools
import math

import jax
import jax.numpy as jnp
from jax import lax
from jax.experimental import pallas as pl
from jax.experimental.pallas import tpu as pltpu

F32 = jnp.float32
BF16 = jnp.bfloat16

D_MODEL = 1024
RMS_EPS = 1e-6
GN_EPS = 1e-5
CHUNK = 128
HEAD_DIM = 64
N_HEADS = 4

RET_THETA = 10000.0
ROPE_THETA = 500000.0
ROPE_DIM = 16

MLA_Q_RANK = 256
MLA_KV_RANK = 128
MLA_NOPE = 64
MLA_ROPE = 32

SSM_GROUPS = 2
SSM_STATE = 128
SSM_CONV = 5
SSM_INNER = 256

DIL_PATTERNS = ((128, 1), (512, 4), (2048, 16))

N_EXPERTS = 16
EXPERT_FF = 2048
EC_CAPACITY_FACTOR = 2

BLK = 256
COL_RQ, COL_RK, COL_RV, COL_RG = 0, 1, 2, 3
COL_MCQ, COL_MKV = 4, 5
COL_XBC = 6
COL_Z = 9
COL_DT = 10
COL_DQ, COL_DK, COL_DV = 11, 12, 13
PROJ_PAD = 14 * BLK

VMEM_LIMIT = 56 * 1024 * 1024


def _cparams(sem):
    return pltpu.CompilerParams(dimension_semantics=sem, vmem_limit_bytes=VMEM_LIMIT)


def _split2(x):
    hi = x.astype(BF16)
    lo = (x - hi.astype(F32)).astype(BF16)
    return hi, lo


def _dot(a, b):
    return jnp.dot(a, b, preferred_element_type=F32)


def _dot_nt(a, b):
    return lax.dot_general(a, b, (((1,), (1,)), ((), ())), preferred_element_type=F32)


def _dot_x_exact(x, m):
    hi, lo = _split2(x)
    return _dot(jnp.concatenate([hi, lo], axis=1), jnp.concatenate([m, m], axis=0))


def _dot_exact_x(m, x):
    hi, lo = _split2(x)
    n = x.shape[1]
    r = _dot(m, jnp.concatenate([hi, lo], axis=1))
    return r[:, :n] + r[:, n:]


def _rope(x, c, s1, s2, half):
    w = x.shape[-1]
    return x * c + pltpu.roll(x, w - half, 1) * s1 + pltpu.roll(x, half, 1) * s2


def _silu(x):
    return x * (1.0 / (1.0 + jnp.exp(-x)))


def _softplus(x):
    return jnp.maximum(x, 0.0) + jnp.log(1.0 + jnp.exp(-jnp.abs(x)))


def _lane_head(shape, width):
    return lax.broadcasted_iota(jnp.int32, shape, 1) // width


def _rms(x, w):
    return x * lax.rsqrt(jnp.mean(x * x, axis=-1, keepdims=True) + RMS_EPS) * w


def _rope_tables(s, rot_dim, theta, head_dim, offset, n_heads):
    half = rot_dim // 2
    inv = 1.0 / (theta ** (jnp.arange(0, rot_dim, 2, dtype=F32) / rot_dim))
    ang = jnp.arange(s, dtype=F32)[:, None] * inv[None, :]
    cos, sin = jnp.cos(ang), jnp.sin(ang)
    pre0 = jnp.zeros((s, offset), F32)
    pre1 = jnp.ones((s, offset), F32)
    post0 = jnp.zeros((s, head_dim - offset - rot_dim), F32)
    post1 = jnp.ones((s, head_dim - offset - rot_dim), F32)
    zh = jnp.zeros((s, half), F32)
    c = jnp.concatenate([pre1, cos, cos, post1], axis=-1)
    s1 = jnp.concatenate([pre0, -sin, zh, post0], axis=-1)
    s2 = jnp.concatenate([pre0, zh, sin, post0], axis=-1)
    tile = lambda t: jnp.tile(t, (1, n_heads))
    return tile(c), tile(s1), tile(s2)


def _tri(n, fn):
    i = jnp.arange(n)
    return fn(i[:, None], i[None, :]).astype(BF16)


def _inproj_kernel(*refs, n_res, col_chunk):
    lnw_ref, w_ref, o_ref = refs[n_res:]
    x = refs[0][...]
    for r in refs[1:n_res]:
        x = x + r[...]
    yb = _rms(x, lnw_ref[...]).astype(BF16)
    for j in range(PROJ_PAD // col_chunk):
        sl = slice(j * col_chunk, (j + 1) * col_chunk)
        o_ref[:, sl] = _dot(yb, w_ref[:, sl])


def _inproj(res, lnw, w_pad):
    t = res[0].shape[0]
    tm = 512
    return pl.pallas_call(
        functools.partial(_inproj_kernel, n_res=len(res), col_chunk=512),
        grid=(t // tm,),
        in_specs=[pl.BlockSpec((tm, D_MODEL), lambda i: (i, 0))] * len(res) + [
            pl.BlockSpec((1, D_MODEL), lambda i: (0, 0)),
            pl.BlockSpec((D_MODEL, PROJ_PAD), lambda i: (0, 0)),
        ],
        out_specs=pl.BlockSpec((tm, PROJ_PAD), lambda i: (i, 0)),
        out_shape=jax.ShapeDtypeStruct((t, PROJ_PAD), F32),
        compiler_params=_cparams(("parallel",)),
    )(*res, lnw, w_pad)


def _pad_cols(w):
    sizes = (256, 256, 256, 256, 256, 128, 32, 256, 768, 8, 256, 256, 256)
    pts, acc = [], 0
    for sz in sizes:
        pts.append((acc, acc + sz))
        acc += sz
    seg = lambda i: w[:, pts[i][0]:pts[i][1]]
    z = lambda n: jnp.zeros((w.shape[0], n), w.dtype)
    cols = [seg(0), seg(1), seg(2), seg(3), seg(4),
            seg(5), seg(6), z(BLK - 128 - 32),
            seg(8), seg(7), seg(9), z(BLK - 8),
            seg(10), seg(11), seg(12)]
    return jnp.concatenate(cols, axis=1)


def _ret_tables():
    pos = jnp.arange(CHUNK, dtype=F32)
    hh = jnp.arange(N_HEADS, dtype=F32)
    lg_f = jnp.log1p(-jnp.exp2(-5.0 - hh))
    lg_b = jnp.log1p(-jnp.exp2(-5.5 - hh))
    diff = pos[:, None] - pos[None, :]
    d_f = jnp.where(diff >= 0, jnp.exp(lg_f[:, None, None] * jnp.maximum(diff, 0.0)), 0.0)
    d_b = jnp.where(diff < 0, jnp.exp(lg_b[:, None, None] * jnp.maximum(-diff, 0.0)), 0.0)
    dmat = (d_f + d_b).reshape(2, 2 * CHUNK, CHUNK)

    def lanes(per_head):
        t = jnp.repeat(per_head[:, :, None], HEAD_DIM, axis=2)
        return t.reshape(2, 2, CHUNK, HEAD_DIM).transpose(0, 2, 1, 3).reshape(2, CHUNK, 2 * HEAD_DIM)

    xi_f = lanes(jnp.exp(lg_f[:, None] * (pos + 1.0)))
    zeta_f = lanes(jnp.exp(lg_f[:, None] * (CHUNK - 1.0 - pos)))
    xi_b = lanes(jnp.exp(lg_b[:, None] * (CHUNK - pos)))
    zeta_b = lanes(jnp.exp(lg_b[:, None] * pos))
    blk = (jnp.arange(128)[:, None] // HEAD_DIM) == (jnp.arange(128)[None, :] // HEAD_DIM)

    def cdec(lg):
        g = jnp.exp(lg * CHUNK).reshape(2, 2)
        rows = jnp.repeat(g, HEAD_DIM, axis=1)
        return jnp.where(blk[None], rows[:, :, None], 0.0)

    lane = jnp.arange(BLK) // HEAD_DIM
    avg = jnp.where(lane[:, None] == lane[None, :], 1.0 / HEAD_DIM, 0.0).astype(BF16)
    return dmat, xi_f, zeta_f, xi_b, zeta_b, cdec(lg_f), cdec(lg_b), avg


def _ret_kernel(q_ref, k_ref, v_ref, g_ref, c_ref, s1_ref, s2_ref, dmat_ref, xif_ref, zf_ref,
                xib_ref, zb_ref, cdf_ref, cdb_ref, avg_ref, o_ref, sf_ref, sb_ref, pb_ref, *, nsteps, g):
    t = pl.program_id(1)

    @pl.when(t == 0)
    def _():
        sf_ref[...] = jnp.zeros_like(sf_ref)
        sb_ref[...] = jnp.zeros_like(sb_ref)

    def rotated(ref, rows):
        wide = lambda t: jnp.concatenate([t[rows, :]] * 2, axis=1)
        return _rope(ref[rows, :], wide(c_ref), wide(s1_ref), wide(s2_ref), HEAD_DIM // 2)

    def state_update(s_ref, p, k, vb, zeta, cdec):
        kz = (k * zeta).T.astype(BF16)
        blockmask = jnp.where(cdec > 0.0, 1.0, 0.0)
        s_ref[p] = s_ref[p] * cdec + _dot(kz, vb) * blockmask

    @pl.when(t < nsteps)
    def _():
        base = (nsteps - 1 - t) * g
        for ci in reversed(range(g)):
            rows = slice(ci * CHUNK, (ci + 1) * CHUNK)
            k = rotated(k_ref, rows)
            vb = v_ref[rows, :].astype(BF16)
            for p in range(2):
                sl = slice(p * 128, (p + 1) * 128)
                pb_ref[base + ci, p] = sb_ref[p].astype(BF16)
                state_update(sb_ref, p, k[:, sl], vb[:, sl], zb_ref[p], cdb_ref[p])

    @pl.when(t >= nsteps)
    def _():
        base = (t - nsteps) * g
        head = _lane_head((CHUNK, 128), HEAD_DIM)
        for ci in range(g):
            rows = slice(ci * CHUNK, (ci + 1) * CHUNK)
            k_all = rotated(k_ref, rows)
            q_all = rotated(q_ref, rows) * (HEAD_DIM ** -0.5)
            vb_all = v_ref[rows, :].astype(BF16)
            for p in range(2):
                sl = slice(p * 128, (p + 1) * 128)
                q, k, vb = q_all[:, sl], k_all[:, sl], vb_all[:, sl]
                kb = k.astype(BF16)
                q2 = jnp.concatenate([jnp.where(head == hh, q, 0.0) for hh in range(2)], axis=0).astype(BF16)
                pm = (_dot_nt(q2, kb) * dmat_ref[p]).astype(BF16)
                o2 = _dot(pm, vb)
                inner = jnp.where(head == 0, o2[:CHUNK], o2[CHUNK:])
                cross_f = _dot((q * xif_ref[p]).astype(BF16), sf_ref[p].astype(BF16))
                cross_b = _dot((q * xib_ref[p]).astype(BF16), pb_ref[base + ci, p])
                o_ref[rows, sl] = inner + cross_f + cross_b
                state_update(sf_ref, p, k, vb, zf_ref[p], cdf_ref[p])

        avg = avg_ref[...]

        def group_mean(x):
            hi, lo = _split2(x)
            return _dot(hi, avg) + _dot(lo, avg)

        o = o_ref[...]
        d = o - group_mean(o)
        var = group_mean(d * d)
        o_ref[...] = _silu(g_ref[...]) * (d * lax.rsqrt(var + GN_EPS))


def _retention(proj, tables, rope_tabs):
    b, s, _ = proj.shape
    nc = s // CHUNK
    g = math.gcd(nc, 4)
    nsteps = nc // g
    rows = g * CHUNK
    both = lambda t: jnp.where(t < nsteps, nsteps - 1 - t, t - nsteps)
    late = lambda t: jnp.where(t < nsteps, 0, t - nsteps)
    col = lambda base, idx: pl.BlockSpec((None, rows, BLK), lambda bi, t: (bi, idx(t), base))
    tab = pl.BlockSpec((rows, 128), lambda bi, t: (both(t), 0))
    pair = pl.BlockSpec((2, CHUNK, 128), lambda bi, t: (0, 0, 0))
    return pl.pallas_call(
        functools.partial(_ret_kernel, nsteps=nsteps, g=g),
        grid=(b, 2 * nsteps),
        in_specs=[
            col(COL_RQ, late), col(COL_RK, both), col(COL_RV, both), col(COL_RG, late),
            tab, tab, tab,
            pl.BlockSpec((2, 2 * CHUNK, CHUNK), lambda bi, t: (0, 0, 0)),
            pair, pair, pair, pair, pair, pair,
            pl.BlockSpec((BLK, BLK), lambda bi, t: (0, 0)),
        ],
        out_specs=pl.BlockSpec((None, rows, BLK), lambda bi, t: (bi, late(t), 0)),
        out_shape=jax.ShapeDtypeStruct((b, s, BLK), F32),
        scratch_shapes=[pltpu.VMEM((2, 128, 128), F32), pltpu.VMEM((2, 128, 128), F32),
                        pltpu.VMEM((nc, 2, 128, 128), BF16)],
        compiler_params=_cparams(("parallel", "arbitrary")),
    )(proj, proj, proj, proj, *rope_tabs, *tables)


def _mla_prep_kernel(cq_ref, kv_ref, qnw_ref, kvnw_ref, wq_ref, wk_ref, wv_ref, place_ref,
                     qc_ref, qs1_ref, qs2_ref, kc_ref, ks1_ref, ks2_ref, q_ref, k_ref, v_ref):
    cqn = _rms(cq_ref[...], qnw_ref[...])
    q = _dot(cqn.astype(BF16), wq_ref[...])
    per_head = lambda r: jnp.concatenate([r[...]] * N_HEADS, axis=1)
    q = _rope(q, per_head(qc_ref), per_head(qs1_ref), per_head(qs2_ref), MLA_ROPE // 2)
    q_ref[...] = (q * ((MLA_NOPE + MLA_ROPE) ** -0.5 * math.log2(math.e))).astype(BF16)
    blk = kv_ref[...]
    ckvn = _rms(blk[:, :MLA_KV_RANK], kvnw_ref[...]).astype(BF16)
    kr = _rope(blk[:, MLA_KV_RANK:], kc_ref[...], ks1_ref[...], ks2_ref[...], MLA_ROPE // 2)
    k_ref[...] = (_dot(ckvn, wk_ref[...]) + _dot(kr.astype(BF16), place_ref[...])).astype(BF16)
    v = _dot(ckvn, wv_ref[...])
    ones = (lax.broadcasted_iota(jnp.int32, v.shape, 1) % 128) >= HEAD_DIM
    v_ref[...] = jnp.where(ones, 1.0, v).astype(BF16)


def _mla_weights(w_uq, w_ukv):
    qh = w_uq.reshape(MLA_Q_RANK, N_HEADS, MLA_NOPE + MLA_ROPE)
    wq = jnp.concatenate([qh, jnp.zeros((MLA_Q_RANK, N_HEADS, 32), F32)], axis=-1).reshape(MLA_Q_RANK, 512)
    kvh = w_ukv.reshape(MLA_KV_RANK, N_HEADS, MLA_NOPE + HEAD_DIM)
    zk = jnp.zeros((MLA_KV_RANK, N_HEADS, 64), F32)
    wk = jnp.concatenate([kvh[..., :MLA_NOPE], zk], axis=-1).reshape(MLA_KV_RANK, 512)
    wv = jnp.concatenate([kvh[..., MLA_NOPE:], zk], axis=-1).reshape(MLA_KV_RANK, 512)
    src = jnp.arange(128)[:, None]
    dst = jnp.arange(512)[None, :]
    place = ((dst % 128 == src + MLA_NOPE) & (src < MLA_ROPE)).astype(BF16)
    return wq.astype(BF16), wk.astype(BF16), wv.astype(BF16), place


def _mla_flash_kernel(q_ref, k_ref, v_ref, o_ref, m_ref, acc_ref, *, strip):
    ki = pl.program_id(2)

    @pl.when(ki == 0)
    def _():
        m_ref[...] = jnp.full_like(m_ref, -jnp.inf)
        acc_ref[...] = jnp.zeros_like(acc_ref)

    for h in range(N_HEADS):
        sl = slice(h * 128, (h + 1) * 128)
        k = k_ref[:, sl]
        v = v_ref[:, sl]
        for r in range(q_ref.shape[0] // strip):
            rows = slice(r * strip, (r + 1) * strip)
            s = _dot_nt(q_ref[rows, sl], k)
            m_prev = m_ref[h, rows]
            m_new = jnp.maximum(m_prev, jnp.max(s, axis=-1, keepdims=True))
            p = jnp.exp2(s - m_new[:, :1])
            acc_ref[h, rows] = jnp.exp2(m_prev - m_new) * acc_ref[h, rows] + _dot(p.astype(BF16), v)
            m_ref[h, rows] = m_new

    @pl.when(ki == pl.num_programs(2) - 1)
    def _():
        outs = []
        for h in range(N_HEADS):
            a = acc_ref[h]
            outs.append(a[:, :HEAD_DIM] / a[:, HEAD_DIM:])
        o_ref[...] = jnp.concatenate(outs, axis=-1)


def _mla(proj, q_norm_w, kv_norm_w, w_uq, w_ukv, q_tabs, k_tabs):
    b, s, _ = proj.shape
    wq, wk, wv, place = _mla_weights(w_uq, w_ukv)
    ts = 512
    row = lambda n: pl.BlockSpec((ts, n), lambda bi, i: (i, 0))
    full = lambda r, n: pl.BlockSpec((r, n), lambda bi, i: (0, 0))
    out = pl.BlockSpec((None, ts, 512), lambda bi, i: (bi, i, 0))
    q, k, v = pl.pallas_call(
        _mla_prep_kernel,
        grid=(b, s // ts),
        in_specs=[
            pl.BlockSpec((None, ts, BLK), lambda bi, i: (bi, i, COL_MCQ)),
            pl.BlockSpec((None, ts, BLK), lambda bi, i: (bi, i, COL_MKV)),
            full(1, MLA_Q_RANK), full(1, MLA_KV_RANK),
            full(MLA_Q_RANK, 512), full(MLA_KV_RANK, 512), full(MLA_KV_RANK, 512), full(128, 512),
            row(128), row(128), row(128), row(128), row(128), row(128),
        ],
        out_specs=[out, out, out],
        out_shape=[jax.ShapeDtypeStruct((b, s, 512), BF16)] * 3,
        compiler_params=_cparams(("parallel", "parallel")),
    )(proj, proj, q_norm_w[None], kv_norm_w[None], wq, wk, wv, place, *q_tabs, *k_tabs)

    tq, tk = math.gcd(s, 2048), math.gcd(s, 512)
    return pl.pallas_call(
        functools.partial(_mla_flash_kernel, strip=tq),
        grid=(b, s // tq, s // tk),
        in_specs=[
            pl.BlockSpec((None, tq, 512), lambda bi, qi, ki: (bi, qi, 0)),
            pl.BlockSpec((None, tk, 512), lambda bi, qi, ki: (bi, ki, 0)),
            pl.BlockSpec((None, tk, 512), lambda bi, qi, ki: (bi, ki, 0)),
        ],
        out_specs=pl.BlockSpec((None, tq, N_HEADS * HEAD_DIM), lambda bi, qi, ki: (bi, qi, 0)),
        out_shape=jax.ShapeDtypeStruct((b, s, N_HEADS * HEAD_DIM), F32),
        scratch_shapes=[pltpu.VMEM((N_HEADS, tq, 128), F32), pltpu.VMEM((N_HEADS, tq, 128), F32)],
        compiler_params=_cparams(("parallel", "parallel", "arbitrary")),
    )(q, k, v)


def _conv_kernel(prev_ref, cur_ref, next_ref, w_ref, b_ref, o_ref, *, ts):
    i = pl.program_id(1)
    w = w_ref[...]

    def conv(x):
        n = x.shape[0]
        acc = x * w[2:3]
        for s in (-2, -1, 1, 2):
            acc = acc + pltpu.roll(x, (-s) % n, 0) * w[s + 2:s + 3]
        return acc

    act = lambda y: _silu(y + b_ref[...])
    cur = cur_ref[...]
    prev = prev_ref[...] * (i > 0).astype(F32)
    nxt = next_ref[...] * (i < pl.num_programs(1) - 1).astype(F32)
    o_ref[...] = act(conv(cur))
    top = conv(jnp.concatenate([prev, cur[:16]], axis=0))
    o_ref[0:8, :] = act(top[8:16])
    bot = conv(jnp.concatenate([cur[ts - 16:], nxt], axis=0))
    o_ref[ts - 8:ts, :] = act(bot[8:16])


def _ssd_conv(proj, conv_w, conv_b):
    b, s, _ = proj.shape
    ts = 512
    width = 3 * BLK
    cb = COL_XBC * BLK // width
    nb8 = ts // 8
    w8 = jnp.concatenate([conv_w, jnp.zeros((8 - SSM_CONV, width), F32)], axis=0)
    return pl.pallas_call(
        functools.partial(_conv_kernel, ts=ts),
        grid=(b, s // ts),
        in_specs=[
            pl.BlockSpec((None, 8, width), lambda bi, i: (bi, jnp.maximum(i * nb8 - 1, 0), cb)),
            pl.BlockSpec((None, ts, width), lambda bi, i: (bi, i, cb)),
            pl.BlockSpec((None, 8, width), lambda bi, i: (bi, jnp.minimum((i + 1) * nb8, s // 8 - 1), cb)),
            pl.BlockSpec((8, width), lambda bi, i: (0, 0)),
            pl.BlockSpec((1, width), lambda bi, i: (0, 0)),
        ],
        out_specs=pl.BlockSpec((None, ts, width), lambda bi, i: (bi, i, 0)),
        out_shape=jax.ShapeDtypeStruct((b, s, width), F32),
        compiler_params=_cparams(("parallel", "parallel")),
    )(proj, proj, proj, w8, conv_b[None])


def _ssd_tables():
    tril = _tri(CHUNK, lambda i, j: j <= i)
    triu = _tri(CHUNK, lambda i, j: j >= i)
    h = jnp.arange(128)[:, None]
    lane = jnp.arange(256)[None, :]
    e_f = ((lane // HEAD_DIM == h) & (h < N_HEADS)).astype(BF16)
    e_b = ((lane // HEAD_DIM == h - N_HEADS) & (h >= N_HEADS) & (h < 2 * N_HEADS)).astype(BF16)
    return tril, triu, e_f, e_b


def _ssd_kernel(xs_ref, bm_ref, cm_ref, z_ref, dt_ref, dtt_ref, bias_ref, a_ref, biasc_ref, ac_ref,
                dskip_ref, nw_ref, tril_ref, triu_ref, ef_ref, eb_ref, o_ref,
                sf_ref, sb_ref, pb_ref, *, nsteps, g):
    t = pl.program_id(1)
    tril, triu = tril_ref[...], triu_ref[...]

    @pl.when(t == 0)
    def _():
        sf_ref[...] = jnp.zeros_like(sf_ref)
        sb_ref[...] = jnp.zeros_like(sb_ref)

    chunks = [slice(ci * CHUNK, (ci + 1) * CHUNK) for ci in range(g)]

    def step_sizes():
        dt = _softplus(dt_ref[:, :128] + bias_ref[...])
        return dt, dt * a_ref[...]

    def cumulative(tri, dta, expand):
        return _dot_x_exact(jnp.concatenate([_dot_exact_x(tri, dta[r]) for r in chunks], axis=0), expand)

    def state_update(s_ref, bmf, weighted_x, total):
        for gi in range(SSM_GROUPS):
            sl = slice(gi * 128, (gi + 1) * 128)
            upd = _dot(bmf[:, sl].T.astype(BF16), weighted_x[:, sl].astype(BF16))
            s_ref[gi] = s_ref[gi] * jnp.exp(total[:, sl]) + upd

    @pl.when(t < nsteps)
    def _():
        base = (nsteps - 1 - t) * g
        dt, dta = step_sizes()
        rcs_all = cumulative(triu, dta, eb_ref[...])
        xdt_all = xs_ref[...] * _dot_x_exact(dt, eb_ref[...])
        for ci in reversed(range(g)):
            rows = chunks[ci]
            rcs_b = rcs_all[rows]
            tot_b = rcs_b[0:1, :]
            pb_ref[base + ci] = sb_ref[...].astype(BF16)
            state_update(sb_ref, bm_ref[rows, :], jnp.exp(tot_b - rcs_b) * xdt_all[rows], tot_b)

    @pl.when(t >= nsteps)
    def _():
        base = (t - nsteps) * g
        row = lax.broadcasted_iota(jnp.int32, (CHUNK, CHUNK), 0)
        colm = lax.broadcasted_iota(jnp.int32, (CHUNK, CHUNK), 1)
        head = _lane_head((CHUNK, 128), HEAD_DIM)
        efb = jnp.concatenate([ef_ref[...], eb_ref[...]], axis=1)
        dt, dta = step_sizes()
        rcs_all = cumulative(triu, dta, eb_ref[...])
        cs_all = cumulative(tril, dta, ef_ref[...])
        dt_exp = _dot_x_exact(dt, efb)
        dtat = jnp.concatenate([dtt_ref[:, r] for r in chunks], axis=0)
        per_chunk = lambda col: jnp.concatenate([col] * g, axis=0)
        dtat = _softplus(dtat + per_chunk(biasc_ref[...])) * per_chunk(ac_ref[...])
        cst_all = _dot_x_exact(dtat, triu)
        rcst_all = _dot_x_exact(dtat, tril)
        for ci in range(g):
            rows = chunks[ci]
            xs = xs_ref[rows, :]
            bmf = bm_ref[rows, :]
            bm = bmf.astype(BF16)
            cm = cm_ref[rows, :].astype(BF16)
            rcs_b = rcs_all[rows]
            cs_f = cs_all[rows]
            tot_f = cs_f[CHUNK - 1:CHUNK, :]
            xdt_f = xs * dt_exp[rows, :SSM_INNER]
            xdt_b = xs * dt_exp[rows, SSM_INNER:]
            cst = cst_all[8 * ci:8 * ci + 8]
            rcst = rcst_all[8 * ci:8 * ci + 8]
            pb = pb_ref[base + ci]
            ys = []
            for gi in range(SSM_GROUPS):
                sl = slice(gi * 128, (gi + 1) * 128)
                cb = _dot_nt(cm[:, sl], bm[:, sl])
                xcat = jnp.concatenate([xdt_f[:, sl], xdt_b[:, sl]], axis=0).astype(BF16)
                wcats = []
                for hh in range(2):
                    h = 2 * gi + hh
                    c0 = h * HEAD_DIM
                    seg_f = cs_f[:, c0:c0 + 1] - cst[h:h + 1, :]
                    seg_b = rcs_b[:, c0:c0 + 1] - rcst[N_HEADS + h:N_HEADS + h + 1, :]
                    dec_f = jnp.where(row >= colm, jnp.exp(jnp.minimum(seg_f, 0.0)), 0.0)
                    dec_b = jnp.where(row < colm, jnp.exp(jnp.minimum(seg_b, 0.0)), 0.0)
                    wcats.append(jnp.concatenate([cb * dec_f, cb * dec_b], axis=1))
                y2 = _dot(jnp.concatenate(wcats, axis=0).astype(BF16), xcat)
                yg = jnp.where(head == 0, y2[:CHUNK], y2[CHUNK:])
                states = jnp.concatenate([sf_ref[gi].astype(BF16), pb[gi]], axis=1)
                off = _dot(cm[:, sl], states)
                ys.append(yg + jnp.exp(cs_f[:, sl]) * off[:, :128] + jnp.exp(rcs_b[:, sl]) * off[:, 128:])
            y = jnp.concatenate(ys, axis=1) + dskip_ref[...] * xs
            y = y * _silu(z_ref[rows, :])
            o_ref[rows, :] = _rms(y, nw_ref[...])
            state_update(sf_ref, bmf, jnp.exp(tot_f - cs_f) * xdt_f, tot_f)


def _ssd(proj, xbc, a_log, dt_bias, d_skip, norm_w, tables):
    b, s, _ = proj.shape
    nc = s // CHUNK
    g = math.gcd(nc, 4)
    nsteps = nc // g
    rows = g * CHUNK
    dt_t = jnp.swapaxes(proj[:, :, COL_DT * BLK:COL_DT * BLK + 8], 1, 2)
    pad_row = lambda v: jnp.concatenate([v.reshape(1, 8), jnp.zeros((1, 120), F32)], axis=1)
    a = -jnp.exp(a_log.astype(F32))
    both = lambda t: jnp.where(t < nsteps, nsteps - 1 - t, t - nsteps)
    late = lambda t: jnp.where(t < nsteps, 0, t - nsteps)
    full = lambda r, n: pl.BlockSpec((r, n), lambda bi, t: (0, 0))
    blk = lambda idx, c: pl.BlockSpec((None, rows, BLK), lambda bi, t: (bi, idx(t), c))
    return pl.pallas_call(
        functools.partial(_ssd_kernel, nsteps=nsteps, g=g),
        grid=(b, 2 * nsteps),
        in_specs=[
            blk(both, 0), blk(both, 1), blk(late, 2), blk(late, COL_Z), blk(both, COL_DT),
            pl.BlockSpec((None, 8, rows), lambda bi, t: (bi, 0, late(t))),
            full(1, 128), full(1, 128), full(8, 1), full(8, 1),
            full(1, SSM_INNER), full(1, SSM_INNER),
            full(CHUNK, CHUNK), full(CHUNK, CHUNK), full(128, 256), full(128, 256),
        ],
        out_specs=blk(late, 0),
        out_shape=jax.ShapeDtypeStruct((b, s, SSM_INNER), F32),
        scratch_shapes=[pltpu.VMEM((SSM_GROUPS, 128, 128), F32), pltpu.VMEM((SSM_GROUPS, 128, 128), F32),
                        pltpu.VMEM((nc, SSM_GROUPS, 128, 128), BF16)],
        compiler_params=_cparams(("parallel", "arbitrary")),
    )(xbc, xbc, xbc, proj, proj, dt_t, pad_row(dt_bias), pad_row(a), dt_bias.reshape(8, 1), a.reshape(8, 1),
      jnp.repeat(d_skip, HEAD_DIM)[None], norm_w[None], *tables)


def _dil_prep_kernel(q_ref, k_ref, c_ref, s1_ref, s2_ref, qo_ref, ko_ref):
    c, s1, s2 = (jnp.concatenate([r[...]] * 2, axis=1) for r in (c_ref, s1_ref, s2_ref))
    qo_ref[...] = _rope(q_ref[...], c, s1, s2, ROPE_DIM // 2) * (HEAD_DIM ** -0.5)
    ko_ref[...] = _rope(k_ref[...], c, s1, s2, ROPE_DIM // 2)


def _dil_kernel(q_ref, k_ref, v_ref, o_ref, m_ref, l_ref, a_ref, *, s):
    head = _lane_head((128, 128), HEAD_DIM)
    kw = 2 * 128
    for pi, (win, d) in enumerate(DIL_PATTERNS):
        half = win // (2 * d)
        seg = s // d
        per_seg = seg // 128

        def body(i, carry, d=d, pi=pi, half=half, seg=seg, per_seg=per_seg):
            r = i // per_seg
            m0 = (i % per_seg) * 128
            ks = jnp.clip(m0 - half, 0, seg - kw)
            if d == 1:
                qrows = pl.ds(pl.multiple_of(m0, 128), 128)
                krows = pl.ds(pl.multiple_of(ks, 64), kw)
            else:
                qrows = pl.ds(r + d * m0, 128, stride=d)
                krows = pl.ds(r + d * ks, kw, stride=d)
            q = q_ref[qrows, :]
            kb = k_ref[krows, :].astype(BF16)
            vb = v_ref[krows, :].astype(BF16)
            qpos = m0 + lax.broadcasted_iota(jnp.int32, (256, kw), 0) % 128
            kpos = ks + lax.broadcasted_iota(jnp.int32, (256, kw), 1)
            valid = jnp.abs(qpos - kpos) <= half
            q2 = jnp.concatenate([jnp.where(head == hh, q, 0.0) for hh in range(2)], axis=0).astype(BF16)
            sc = jnp.where(valid, _dot_nt(q2, kb), -jnp.inf)
            m2 = jnp.max(sc, axis=-1, keepdims=True)
            p = jnp.exp(sc - m2)
            l2 = jnp.sum(p, axis=-1, keepdims=True)
            a2 = _dot(p.astype(BF16), vb)
            m_ref.at[pi][qrows, :] = jnp.where(head == 0, m2[:128], m2[128:])
            l_ref.at[pi][qrows, :] = jnp.where(head == 0, l2[:128], l2[128:])
            a_ref.at[pi][qrows, :] = jnp.where(head == 0, a2[:128], a2[128:])
            return carry

        lax.fori_loop(0, s // 128, body, 0, unroll=4)

    def merge(i, carry):
        rows = pl.ds(pl.multiple_of(i * 512, 512), 512)
        ms = [m_ref[pi, rows, :] for pi in range(len(DIL_PATTERNS))]
        top = functools.reduce(jnp.maximum, ms)
        ws = [jnp.exp(m - top) for m in ms]
        den = sum(w * l_ref[pi, rows, :] for pi, w in enumerate(ws))
        num = sum(w * a_ref[pi, rows, :] for pi, w in enumerate(ws))
        o_ref[rows, :] = num / den
        return carry

    lax.fori_loop(0, s // 512, merge, 0)


def _dilated(proj, tabs):
    b, s, _ = proj.shape
    assert all(s // d >= 256 for _, d in DIL_PATTERNS)
    ts = 512
    rowt = pl.BlockSpec((ts, 128), lambda bi, i: (i, 0))
    outb = pl.BlockSpec((None, ts, BLK), lambda bi, i: (bi, i, 0))
    qr, kr = pl.pallas_call(
        _dil_prep_kernel,
        grid=(b, s // ts),
        in_specs=[
            pl.BlockSpec((None, ts, BLK), lambda bi, i: (bi, i, COL_DQ)),
            pl.BlockSpec((None, ts, BLK), lambda bi, i: (bi, i, COL_DK)),
            rowt, rowt, rowt,
        ],
        out_specs=[outb, outb],
        out_shape=[jax.ShapeDtypeStruct((b, s, BLK), F32)] * 2,
        compiler_params=_cparams(("parallel", "parallel")),
    )(proj, proj, *tabs)
    seqb = lambda c: pl.BlockSpec((None, s, 128), lambda bi, p: (bi, 0, 2 * c + p))
    return pl.pallas_call(
        functools.partial(_dil_kernel, s=s),
        grid=(b, 2),
        in_specs=[seqb(0), seqb(0), seqb(COL_DV)],
        out_specs=seqb(0),
        out_shape=jax.ShapeDtypeStruct((b, s, BLK), F32),
        scratch_shapes=[pltpu.VMEM((len(DIL_PATTERNS), s, 128), F32)] * 3,
        compiler_params=_cparams(("parallel", "parallel")),
    )(qr, kr, proj)


def _outproj_kernel(*refs, n_res):
    ya_ref, yb_ref, yc_ref, yd_ref, w_ref, lnw_ref, rw_ref, xo_ref, h_ref, aff_ref = refs[n_res:]
    acc = refs[0][...]
    for r in refs[1:n_res]:
        acc = acc + r[...]
    mixed = jnp.concatenate([r[...].astype(BF16) for r in (ya_ref, yb_ref, yc_ref, yd_ref)], axis=1)
    acc = acc + _dot(mixed, w_ref[...])
    xo_ref[...] = acc
    h = _rms(acc, lnw_ref[...])
    h_hi = h.astype(BF16)
    h_ref[...] = h_hi
    h_lo = (h - h_hi.astype(F32)).astype(BF16)
    rw = rw_ref[...]
    w_hi = rw.astype(BF16)
    w_lo = (rw - w_hi.astype(F32)).astype(BF16)
    logits = _dot(h_hi, w_hi) + _dot(h_hi, w_lo) + _dot(h_lo, w_hi) + _dot(h_lo, w_lo)
    lane = lax.broadcasted_iota(jnp.int32, logits.shape, 1)
    logits = jnp.where(lane < N_EXPERTS, logits, -jnp.inf)
    e = jnp.exp(logits - jnp.max(logits, axis=-1, keepdims=True))
    aff = e / jnp.sum(e, axis=-1, keepdims=True)
    aff_ref[...] = aff.T[:N_EXPERTS, :]


def _outproj(res, ys, w_out, ln2_w, router_w):
    t = res[0].shape[0]
    tm = 256
    rw = jnp.concatenate([router_w, jnp.zeros((D_MODEL, 128 - N_EXPERTS), F32)], axis=1)
    row = lambda n: pl.BlockSpec((tm, n), lambda i: (i, 0))
    full = lambda r, n: pl.BlockSpec((r, n), lambda i: (0, 0))
    return pl.pallas_call(
        functools.partial(_outproj_kernel, n_res=len(res)),
        grid=(t // tm,),
        in_specs=[row(D_MODEL)] * len(res) + [row(BLK)] * 4 + [
            full(D_MODEL, D_MODEL), full(1, D_MODEL), full(D_MODEL, 128)],
        out_specs=[row(D_MODEL), row(D_MODEL), pl.BlockSpec((N_EXPERTS, tm), lambda i: (0, i))],
        out_shape=[jax.ShapeDtypeStruct((t, D_MODEL), F32),
                   jax.ShapeDtypeStruct((t, D_MODEL), BF16),
                   jax.ShapeDtypeStruct((N_EXPERTS, t), F32)],
        compiler_params=_cparams(("parallel",)),
    )(*res, *ys, w_out.astype(BF16), ln2_w[None], rw)


def _exclusive_prefix(x, tri):
    n = x.shape[1] // 128
    off = jnp.zeros((x.shape[0], 1), F32)
    parts = []
    for i in range(n):
        xt = x[:, i * 128:(i + 1) * 128]
        incl = _dot(xt.astype(BF16), tri)
        parts.append(incl - xt + off)
        off = off + incl[:, 127:128]
    return jnp.concatenate(parts, axis=1)


def _select_kernel(aff_ref, tri_ref, key_ref, *, cap):
    a = aff_ref[...]
    n_exp = a.shape[0]
    thr_bits = jnp.zeros((n_exp, 1), jnp.int32)
    for bit in range(30, -1, -1):
        cand = thr_bits | (1 << bit)
        cnt = jnp.sum(jnp.where(a >= lax.bitcast_convert_type(cand, F32), 1.0, 0.0), axis=1, keepdims=True)
        thr_bits = jnp.where(cnt >= cap, cand, thr_bits)
    thr = lax.bitcast_convert_type(thr_bits, F32)
    above = jnp.where(a > thr, 1.0, 0.0)
    tie = jnp.where(a == thr, 1.0, 0.0)
    need = cap - jnp.sum(above, axis=1, keepdims=True)
    tri = tri_ref[...]
    sel = above + tie * jnp.where(_exclusive_prefix(tie, tri) < need, 1.0, 0.0)
    key_ref[...] = jnp.where(sel > 0.0, _exclusive_prefix(sel, tri), -1.0)


def _slot_kernel(key_ref, aff_ref, ltri_ref, idx_ref, gate_ref, *, cap, nt):
    n_exp = key_ref.shape[0]
    pad = jnp.zeros((128 - nt, 128), F32)
    slot = lax.broadcasted_iota(jnp.int32, (128, cap), 1).astype(F32)
    sub = lax.broadcasted_iota(jnp.int32, (128, cap), 0).astype(F32)
    ltri = ltri_ref[...]

    def per_expert(e, carry):
        key = jnp.concatenate([key_ref[e], pad - 1.0], axis=0)
        aff = jnp.concatenate([aff_ref[e], pad], axis=0)
        cnt = jnp.sum(jnp.where(key >= 0.0, 1.0, 0.0), axis=1, keepdims=True)
        cum = _dot(ltri, jnp.broadcast_to(cnt, (128, 128)).astype(BF16))
        in_tile = (jnp.where(slot >= (cum - cnt)[:, :1], 1.0, 0.0)
                   * jnp.where(slot < cum[:, :1], 1.0, 0.0))
        g = in_tile.astype(BF16)
        k_hi, k_lo = _split2(key.T)
        slot_of = _dot(k_hi, g) + _dot(k_lo, g)
        hit = jnp.where(slot_of == slot, 1.0, 0.0)
        aff_t = aff.T
        a_hi = aff_t.astype(BF16)
        r1 = aff_t - a_hi.astype(F32)
        a_mid = r1.astype(BF16)
        a_lo = (r1 - a_mid.astype(F32)).astype(BF16)
        aff_of = _dot(a_hi, g) + _dot(a_mid, g) + _dot(a_lo, g)
        tile = jnp.sum(in_tile * sub, axis=0, keepdims=True)
        within = jnp.sum(hit * sub, axis=0, keepdims=True)
        idx_ref[pl.ds(e, 1), :] = (128.0 * tile + within).astype(jnp.int32)
        gate_ref[pl.ds(e, 1), :] = jnp.sum(hit * aff_of, axis=0, keepdims=True)
        return carry

    lax.fori_loop(0, n_exp, per_expert, 0, unroll=4)


def _moe_select(aff_t, b, s, cap):
    nt = s // 128
    assert nt <= 128 and nt % 8 == 0 and cap % 128 == 0
    key = pl.pallas_call(
        functools.partial(_select_kernel, cap=cap),
        grid=(b,),
        in_specs=[pl.BlockSpec((N_EXPERTS, s), lambda bi: (0, bi)),
                  pl.BlockSpec((128, 128), lambda bi: (0, 0))],
        out_specs=pl.BlockSpec((None, N_EXPERTS, s), lambda bi: (bi, 0, 0)),
        out_shape=jax.ShapeDtypeStruct((b, N_EXPERTS, s), F32),
        compiler_params=_cparams(("parallel",)),
    )(aff_t, _tri(128, lambda i, j: i <= j))
    out = pl.BlockSpec((None, N_EXPERTS, cap), lambda bi: (bi, 0, 0))
    return pl.pallas_call(
        functools.partial(_slot_kernel, cap=cap, nt=nt),
        grid=(b,),
        in_specs=[pl.BlockSpec((None, N_EXPERTS, nt, 128), lambda bi: (bi, 0, 0, 0)),
                  pl.BlockSpec((N_EXPERTS, nt, 128), lambda bi: (0, bi, 0)),
                  pl.BlockSpec((128, 128), lambda bi: (0, 0))],
        out_specs=[out, out],
        out_shape=[jax.ShapeDtypeStruct((b, N_EXPERTS, cap), jnp.int32),
                   jax.ShapeDtypeStruct((b, N_EXPERTS, cap), F32)],
        compiler_params=_cparams(("parallel",)),
    )(key.reshape(b, N_EXPERTS, nt, 128), aff_t.reshape(N_EXPERTS, b * nt, 128), _tri(128, lambda i, j: j <= i))


ROWS_PER_ITER = 8


def _gather_kernel(idx_ref, h_ref, o_ref, hf_ref, buf_ref, *, cap, n_exp):
    e = pl.program_id(1)
    base = (pl.program_id(0) * n_exp + e) * cap

    @pl.when(e == 0)
    def _():
        hf_ref[...] = h_ref[...].astype(F32)

    def body(jb, carry):
        j0 = pl.multiple_of(jb * ROWS_PER_ITER, ROWS_PER_ITER)
        vals = [hf_ref[pl.ds(idx_ref[base + j0 + u], 1), :] for u in range(ROWS_PER_ITER)]
        for u in range(ROWS_PER_ITER):
            buf_ref[pl.ds(j0 + u, 1), :] = vals[u]
        return carry

    lax.fori_loop(0, cap // ROWS_PER_ITER, body, 0)
    o_ref[...] = buf_ref[...].astype(BF16)


def _moe_gather(idx_flat, h, cap):
    b, s, d = h.shape
    return pl.pallas_call(
        functools.partial(_gather_kernel, cap=cap, n_exp=N_EXPERTS),
        grid_spec=pltpu.PrefetchScalarGridSpec(
            num_scalar_prefetch=1,
            grid=(b, N_EXPERTS),
            in_specs=[pl.BlockSpec((None, s, d), lambda bi, e, idx: (bi, 0, 0))],
            out_specs=pl.BlockSpec((None, None, cap, d), lambda bi, e, idx: (e, bi, 0, 0)),
            scratch_shapes=[pltpu.VMEM((s, d), F32), pltpu.VMEM((cap, d), F32)],
        ),
        out_shape=jax.ShapeDtypeStruct((N_EXPERTS, b, cap, d), BF16),
        compiler_params=_cparams(("parallel", "arbitrary")),
    )(idx_flat, h)


def _ffn_kernel(x_ref, wg_ref, wu_ref, wd_ref, o_ref, *, rows, tr):
    f = pl.program_id(1)
    wg = wg_ref[...].astype(BF16)
    wu = wu_ref[...].astype(BF16)
    wd = wd_ref[...].astype(BF16)

    def partial_out(sl):
        x = x_ref[sl, :]
        hid = (_silu(_dot(x, wg)) * _dot(x, wu)).astype(BF16)
        return _dot(hid, wd)

    for r in range(rows // tr):
        sl = slice(r * tr, (r + 1) * tr)

        @pl.when(f == 0)
        def _():
            o_ref[sl, :] = partial_out(sl)

        @pl.when(f > 0)
        def _():
            o_ref[sl, :] += partial_out(sl)


def _moe_ffn(xin, w_gate, w_up, w_down, layer):
    n_exp, rows, d = xin.shape
    tf = 512
    return pl.pallas_call(
        functools.partial(_ffn_kernel, rows=rows, tr=math.gcd(rows, 1024)),
        grid=(n_exp, EXPERT_FF // tf),
        in_specs=[
            pl.BlockSpec((None, rows, d), lambda e, f: (e, 0, 0)),
            pl.BlockSpec((None, None, d, tf), lambda e, f: (layer, e, 0, f)),
            pl.BlockSpec((None, None, d, tf), lambda e, f: (layer, e, 0, f)),
            pl.BlockSpec((None, None, tf, d), lambda e, f: (layer, e, f, 0)),
        ],
        out_specs=pl.BlockSpec((None, rows, d), lambda e, f: (e, 0, 0)),
        out_shape=jax.ShapeDtypeStruct((n_exp, rows, d), F32),
        compiler_params=_cparams(("parallel", "arbitrary")),
    )(xin, w_gate, w_up, w_down)


def _combine_kernel(idx_ref, gate_ref, y_ref, o_ref, *, cap, n_exp):
    e = pl.program_id(1)
    base = (pl.program_id(0) * n_exp + e) * cap

    @pl.when(e == 0)
    def _():
        o_ref[...] = jnp.zeros_like(o_ref)

    def body(jb, carry):
        j0 = pl.multiple_of(jb * ROWS_PER_ITER, ROWS_PER_ITER)
        rows = [idx_ref[base + j0 + u] for u in range(ROWS_PER_ITER)]
        vals = [o_ref[pl.ds(rows[u], 1), :] + gate_ref[base + j0 + u] * y_ref[pl.ds(j0 + u, 1), :]
                for u in range(ROWS_PER_ITER)]
        for u in range(ROWS_PER_ITER):
            o_ref[pl.ds(rows[u], 1), :] = vals[u]
        return carry

    lax.fori_loop(0, cap // ROWS_PER_ITER, body, 0)


def _moe_combine(idx_flat, gate_flat, y, b, s, cap):
    d = y.shape[-1]
    return pl.pallas_call(
        functools.partial(_combine_kernel, cap=cap, n_exp=N_EXPERTS),
        grid_spec=pltpu.PrefetchScalarGridSpec(
            num_scalar_prefetch=2,
            grid=(b, N_EXPERTS),
            in_specs=[pl.BlockSpec((None, None, cap, d), lambda bi, e, idx, gate: (e, bi, 0, 0))],
            out_specs=pl.BlockSpec((None, s, d), lambda bi, e, idx, gate: (bi, 0, 0)),
        ),
        out_shape=jax.ShapeDtypeStruct((b, s, d), F32),
        compiler_params=_cparams(("parallel", "arbitrary")),
    )(idx_flat, gate_flat, y)


def _final_norm_kernel(xa_ref, xb_ref, w_ref, o_ref):
    o_ref[...] = _rms(xa_ref[...] + xb_ref[...], w_ref[...])


def _final_norm(xa, xb, w):
    t = xa.shape[0]
    tm = 1024
    row = pl.BlockSpec((tm, D_MODEL), lambda i: (i, 0))
    return pl.pallas_call(
        _final_norm_kernel,
        grid=(t // tm,),
        in_specs=[row, row, pl.BlockSpec((1, D_MODEL), lambda i: (0, 0))],
        out_specs=row,
        out_shape=jax.ShapeDtypeStruct((t, D_MODEL), F32),
        compiler_params=_cparams(("parallel",)),
    )(xa, xb, w[None])


def _moe(h, aff, w_gate, w_up, w_down, layer, b, s):
    d = h.shape[-1]
    cap = EC_CAPACITY_FACTOR * s // N_EXPERTS
    token_idx, gate = _moe_select(aff, b, s, cap)
    idx_flat = token_idx.reshape(-1)
    xin = _moe_gather(idx_flat, h.reshape(b, s, d), cap)
    y = _moe_ffn(xin.reshape(N_EXPERTS, b * cap, d), w_gate, w_up, w_down, layer)
    return _moe_combine(idx_flat, gate.reshape(-1), y.reshape(N_EXPERTS, b, cap, d), b, s, cap)


def kernel(x, ln1_w, w_in, mla_q_norm_w, mla_kv_norm_w, mla_w_uq, mla_w_ukv, ssm_conv_w, ssm_conv_b,
           ssm_a_log, ssm_dt_bias, ssm_d, ssm_norm_w, w_out, ln2_w, router_w, exp_w_gate, exp_w_up,
           exp_w_down, final_norm_w):
    b, s, d = x.shape
    depth = w_in.shape[0]
    ret_tabs = _ret_tables()
    ret_rope = _rope_tables(s, HEAD_DIM, RET_THETA, HEAD_DIM, 0, 2)
    mla_q_rope = _rope_tables(s, MLA_ROPE, ROPE_THETA, 128, MLA_NOPE, 1)
    mla_k_rope = _rope_tables(s, MLA_ROPE, ROPE_THETA, 128, 0, 1)
    dil_rope = _rope_tables(s, ROPE_DIM, ROPE_THETA, HEAD_DIM, 0, 2)
    ssd_tabs = _ssd_tables()
    res = [x.reshape(b * s, d)]
    for i in range(depth):
        proj = _inproj(res, ln1_w[i][None], _pad_cols(w_in[i]).astype(BF16)).reshape(b, s, PROJ_PAD)
        y_a = _retention(proj, ret_tabs, ret_rope)
        y_b = _mla(proj, mla_q_norm_w[i], mla_kv_norm_w[i], mla_w_uq[i], mla_w_ukv[i], mla_q_rope, mla_k_rope)
        xbc = _ssd_conv(proj, ssm_conv_w[i], ssm_conv_b[i])
        y_c = _ssd(proj, xbc, ssm_a_log[i], ssm_dt_bias[i], ssm_d[i], ssm_norm_w[i], ssd_tabs)
        y_d = _dilated(proj, dil_rope)
        ys = [y.reshape(b * s, BLK) for y in (y_a, y_b, y_c, y_d)]
        x_mid, h, aff = _outproj(res, ys, w_out[i], ln2_w[i], router_w[i])
        moe = _moe(h, aff, exp_w_gate, exp_w_up, exp_w_down, i, b, s)
        res = [x_mid, moe.reshape(b * s, d)]
    return _final_norm(res[0], res[1], final_norm_w).reshape(b, s, d)
```

```python
import functools
import math

import jax
import jax.numpy as jnp
from jax import lax
from jax.experimental import pallas as pl
from jax.experimental.pallas import tpu as pltpu

F32 = jnp.float32
BF16 = jnp.bfloat16

D_MODEL = 1024
RMS_EPS = 1e-6
GN_EPS = 1e-5
CHUNK = 128
HEAD_DIM = 64
N_HEADS = 4

RET_THETA = 10000.0
ROPE_THETA = 500000.0
ROPE_DIM = 16

MLA_Q_RANK = 256
MLA_KV_RANK = 128
MLA_NOPE = 64
MLA_ROPE = 32

SSM_GROUPS = 2
SSM_STATE = 128
SSM_CONV = 5
SSM_INNER = 256

DIL_PATTERNS = ((128, 1), (512, 4), (2048, 16))

N_EXPERTS = 16
EXPERT_FF = 2048
EC_CAPACITY_FACTOR = 2

BLK = 256
COL_RQ, COL_RK, COL_RV, COL_RG = 0, 1, 2, 3
COL_MCQ, COL_MKV = 4, 5
COL_XBC = 6
COL_Z = 9
COL_DT = 10
COL_DQ, COL_DK, COL_DV = 11, 12, 13
PROJ_PAD = 14 * BLK

VMEM_LIMIT = 56 * 1024 * 1024


def _cparams(sem):
    return pltpu.CompilerParams(dimension_semantics=sem, vmem_limit_bytes=VMEM_LIMIT)


def _split2(x):
    hi = x.astype(BF16)
    lo = (x - hi.astype(F32)).astype(BF16)
    return hi, lo


def _dot(a, b):
    return jnp.dot(a, b, preferred_element_type=F32)


def _dot_nt(a, b):
    return lax.dot_general(a, b, (((1,), (1,)), ((), ())), preferred_element_type=F32)


def _dot_x_exact(x, m):
    hi, lo = _split2(x)
    return _dot(jnp.concatenate([hi, lo], axis=1), jnp.concatenate([m, m], axis=0))


def _dot_exact_x(m, x):
    hi, lo = _split2(x)
    n = x.shape[1]
    r = _dot(m, jnp.concatenate([hi, lo], axis=1))
    return r[:, :n] + r[:, n:]


def _rope(x, c, s1, s2, half):
    w = x.shape[-1]
    return x * c + pltpu.roll(x, w - half, 1) * s1 + pltpu.roll(x, half, 1) * s2


def _silu(x):
    return x * (1.0 / (1.0 + jnp.exp(-x)))


def _softplus(x):
    return jnp.maximum(x, 0.0) + jnp.log(1.0 + jnp.exp(-jnp.abs(x)))


def _lane_head(shape, width):
    return lax.broadcasted_iota(jnp.int32, shape, 1) // width


def _rms(x, w):
    return x * lax.rsqrt(jnp.mean(x * x, axis=-1, keepdims=True) + RMS_EPS) * w


def _rope_tables(s, rot_dim, theta, head_dim, offset, n_heads):
    half = rot_dim // 2
    inv = 1.0 / (theta ** (jnp.arange(0, rot_dim, 2, dtype=F32) / rot_dim))
    ang = jnp.arange(s, dtype=F32)[:, None] * inv[None, :]
    cos, sin = jnp.cos(ang), jnp.sin(ang)
    pre0 = jnp.zeros((s, offset), F32)
    pre1 = jnp.ones((s, offset), F32)
    post0 = jnp.zeros((s, head_dim - offset - rot_dim), F32)
    post1 = jnp.ones((s, head_dim - offset - rot_dim), F32)
    zh = jnp.zeros((s, half), F32)
    c = jnp.concatenate([pre1, cos, cos, post1], axis=-1)
    s1 = jnp.concatenate([pre0, -sin, zh, post0], axis=-1)
    s2 = jnp.concatenate([pre0, zh, sin, post0], axis=-1)
    tile = lambda t: jnp.tile(t, (1, n_heads))
    return tile(c), tile(s1), tile(s2)


def _tri(n, fn):
    i = jnp.arange(n)
    return fn(i[:, None], i[None, :]).astype(BF16)


def _inproj_kernel(*refs, n_res, col_chunk):
    lnw_ref, w_ref, rc_ref, rs1_ref, rs2_ref, dc_ref, ds1_ref, ds2_ref, o_ref = refs[n_res:]
    x = refs[0][...]
    for r in refs[1:n_res]:
        x = x + r[...]
    yb = _rms(x, lnw_ref[...]).astype(BF16)
    for j in range(PROJ_PAD // col_chunk):
        sl = slice(j * col_chunk, (j + 1) * col_chunk)
        o_ref[:, sl] = _dot(yb, w_ref[:, sl])
    wide = lambda tabs: [jnp.concatenate([r[...]] * 2, axis=1) for r in tabs]
    ret = wide((rc_ref, rs1_ref, rs2_ref))
    dil = wide((dc_ref, ds1_ref, ds2_ref))
    for col, tabs, half, scale in ((COL_RQ, ret, HEAD_DIM // 2, HEAD_DIM ** -0.5), (COL_RK, ret, HEAD_DIM // 2, None),
                                   (COL_DQ, dil, ROPE_DIM // 2, HEAD_DIM ** -0.5), (COL_DK, dil, ROPE_DIM // 2, None)):
        sl = slice(col * BLK, (col + 1) * BLK)
        y = _rope(o_ref[:, sl], *tabs, half)
        o_ref[:, sl] = y if scale is None else y * scale


def _inproj(res, lnw, w_pad, ret_rope, dil_rope, s):
    t = res[0].shape[0]
    tm = math.gcd(s, 512)
    tab = pl.BlockSpec((tm, 128), lambda i: (i % (s // tm), 0))
    return pl.pallas_call(
        functools.partial(_inproj_kernel, n_res=len(res), col_chunk=512),
        grid=(t // tm,),
        in_specs=[pl.BlockSpec((tm, D_MODEL), lambda i: (i, 0))] * len(res) + [
            pl.BlockSpec((1, D_MODEL), lambda i: (0, 0)),
            pl.BlockSpec((D_MODEL, PROJ_PAD), lambda i: (0, 0)),
        ] + [tab] * 6,
        out_specs=pl.BlockSpec((tm, PROJ_PAD), lambda i: (i, 0)),
        out_shape=jax.ShapeDtypeStruct((t, PROJ_PAD), F32),
        compiler_params=_cparams(("parallel",)),
    )(*res, lnw, w_pad, *ret_rope, *dil_rope)


def _pad_cols(w):
    sizes = (256, 256, 256, 256, 256, 128, 32, 256, 768, 8, 256, 256, 256)
    pts, acc = [], 0
    for sz in sizes:
        pts.append((acc, acc + sz))
        acc += sz
    seg = lambda i: w[:, pts[i][0]:pts[i][1]]
    z = lambda n: jnp.zeros((w.shape[0], n), w.dtype)
    cols = [seg(0), seg(1), seg(2), seg(3), seg(4),
            seg(5), seg(6), z(BLK - 128 - 32),
            seg(8), seg(7), seg(9), z(BLK - 8),
            seg(10), seg(11), seg(12)]
    return jnp.concatenate(cols, axis=1)


def _ret_tables():
    pos = jnp.arange(CHUNK, dtype=F32)
    hh = jnp.arange(N_HEADS, dtype=F32)
    lg_f = jnp.log1p(-jnp.exp2(-5.0 - hh))
    lg_b = jnp.log1p(-jnp.exp2(-5.5 - hh))
    diff = pos[:, None] - pos[None, :]
    d_f = jnp.where(diff >= 0, jnp.exp(lg_f[:, None, None] * jnp.maximum(diff, 0.0)), 0.0)
    d_b = jnp.where(diff < 0, jnp.exp(lg_b[:, None, None] * jnp.maximum(-diff, 0.0)), 0.0)
    dmat = (d_f + d_b).reshape(2, 2 * CHUNK, CHUNK)

    def lanes(per_head):
        t = jnp.repeat(per_head[:, :, None], HEAD_DIM, axis=2)
        return t.reshape(2, 2, CHUNK, HEAD_DIM).transpose(0, 2, 1, 3).reshape(2, CHUNK, 2 * HEAD_DIM)

    xi_f = lanes(jnp.exp(lg_f[:, None] * (pos + 1.0)))
    zeta_f = lanes(jnp.exp(lg_f[:, None] * (CHUNK - 1.0 - pos)))
    xi_b = lanes(jnp.exp(lg_b[:, None] * (CHUNK - pos)))
    zeta_b = lanes(jnp.exp(lg_b[:, None] * pos))
    blk = (jnp.arange(128)[:, None] // HEAD_DIM) == (jnp.arange(128)[None, :] // HEAD_DIM)

    def cdec(lg):
        g = jnp.exp(lg * CHUNK).reshape(2, 2)
        rows = jnp.repeat(g, HEAD_DIM, axis=1)
        return jnp.where(blk[None], rows[:, :, None], 0.0)

    lane = jnp.arange(BLK) // HEAD_DIM
    avg = jnp.where(lane[:, None] == lane[None, :], 1.0 / HEAD_DIM, 0.0).astype(BF16)
    return dmat, xi_f, zeta_f, xi_b, zeta_b, cdec(lg_f), cdec(lg_b), avg


def _ret_kernel(q_ref, k_ref, v_ref, g_ref, dmat_ref, xif_ref, zf_ref,
                xib_ref, zb_ref, cdf_ref, cdb_ref, avg_ref, o_ref, sf_ref, sb_ref, pb_ref, *, nsteps, g):
    t = pl.program_id(1)

    @pl.when(t == 0)
    def _():
        sf_ref[...] = jnp.zeros_like(sf_ref)
        sb_ref[...] = jnp.zeros_like(sb_ref)

    def state_update(s_ref, p, k, vb, zeta, cdec):
        kz = (k * zeta).T.astype(BF16)
        blockmask = jnp.where(cdec > 0.0, 1.0, 0.0)
        s_ref[p] = s_ref[p] * cdec + _dot(kz, vb) * blockmask

    @pl.when(t < nsteps)
    def _():
        base = (nsteps - 1 - t) * g
        for ci in reversed(range(g)):
            rows = slice(ci * CHUNK, (ci + 1) * CHUNK)
            k = k_ref[rows, :]
            vb = v_ref[rows, :].astype(BF16)
            for p in range(2):
                sl = slice(p * 128, (p + 1) * 128)
                pb_ref[base + ci, p] = sb_ref[p].astype(BF16)
                state_update(sb_ref, p, k[:, sl], vb[:, sl], zb_ref[p], cdb_ref[p])

    @pl.when(t >= nsteps)
    def _():
        base = (t - nsteps) * g
        head = _lane_head((CHUNK, 128), HEAD_DIM)
        for ci in range(g):
            rows = slice(ci * CHUNK, (ci + 1) * CHUNK)
            k_all = k_ref[rows, :]
            q_all = q_ref[rows, :]
            vb_all = v_ref[rows, :].astype(BF16)
            for p in range(2):
                sl = slice(p * 128, (p + 1) * 128)
                q, k, vb = q_all[:, sl], k_all[:, sl], vb_all[:, sl]
                kb = k.astype(BF16)
                q2 = jnp.concatenate([jnp.where(head == hh, q, 0.0) for hh in range(2)], axis=0).astype(BF16)
                pm = (_dot_nt(q2, kb) * dmat_ref[p]).astype(BF16)
                o2 = _dot(pm, vb)
                inner = jnp.where(head == 0, o2[:CHUNK], o2[CHUNK:])
                cross_f = _dot((q * xif_ref[p]).astype(BF16), sf_ref[p].astype(BF16))
                cross_b = _dot((q * xib_ref[p]).astype(BF16), pb_ref[base + ci, p])
                o_ref[rows, sl] = inner + cross_f + cross_b
                state_update(sf_ref, p, k, vb, zf_ref[p], cdf_ref[p])

        avg = avg_ref[...]

        def group_mean(x):
            hi, lo = _split2(x)
            return _dot(hi, avg) + _dot(lo, avg)

        o = o_ref[...]
        d = o - group_mean(o)
        var = group_mean(d * d)
        o_ref[...] = _silu(g_ref[...]) * (d * lax.rsqrt(var + GN_EPS))


def _retention(proj, tables):
    b, s, _ = proj.shape
    nc = s // CHUNK
    g = math.gcd(nc, 4)
    nsteps = nc // g
    rows = g * CHUNK
    both = lambda t: jnp.where(t < nsteps, nsteps - 1 - t, t - nsteps)
    late = lambda t: jnp.where(t < nsteps, 0, t - nsteps)
    col = lambda base, idx: pl.BlockSpec((None, rows, BLK), lambda bi, t: (bi, idx(t), base))
    pair = pl.BlockSpec((2, CHUNK, 128), lambda bi, t: (0, 0, 0))
    return pl.pallas_call(
        functools.partial(_ret_kernel, nsteps=nsteps, g=g),
        grid=(b, 2 * nsteps),
        in_specs=[
            col(COL_RQ, late), col(COL_RK, both), col(COL_RV, both), col(COL_RG, late),
            pl.BlockSpec((2, 2 * CHUNK, CHUNK), lambda bi, t: (0, 0, 0)),
            pair, pair, pair, pair, pair, pair,
            pl.BlockSpec((BLK, BLK), lambda bi, t: (0, 0)),
        ],
        out_specs=pl.BlockSpec((None, rows, BLK), lambda bi, t: (bi, late(t), 0)),
        out_shape=jax.ShapeDtypeStruct((b, s, BLK), F32),
        scratch_shapes=[pltpu.VMEM((2, 128, 128), F32), pltpu.VMEM((2, 128, 128), F32),
                        pltpu.VMEM((nc, 2, 128, 128), BF16)],
        compiler_params=_cparams(("parallel", "arbitrary")),
    )(proj, proj, proj, proj, *tables)


def _mla_prep_kernel(cq_ref, kv_ref, qnw_ref, kvnw_ref, wq_ref, wk_ref, wv_ref, place_ref,
                     qc_ref, qs1_ref, qs2_ref, kc_ref, ks1_ref, ks2_ref, q_ref, k_ref, v_ref):
    cqn = _rms(cq_ref[...], qnw_ref[...])
    q = _dot(cqn.astype(BF16), wq_ref[...])
    per_head = lambda r: jnp.concatenate([r[...]] * N_HEADS, axis=1)
    q = _rope(q, per_head(qc_ref), per_head(qs1_ref), per_head(qs2_ref), MLA_ROPE // 2)
    q_ref[...] = (q * ((MLA_NOPE + MLA_ROPE) ** -0.5 * math.log2(math.e))).astype(BF16)
    blk = kv_ref[...]
    ckvn = _rms(blk[:, :MLA_KV_RANK], kvnw_ref[...]).astype(BF16)
    kr = _rope(blk[:, MLA_KV_RANK:], kc_ref[...], ks1_ref[...], ks2_ref[...], MLA_ROPE // 2)
    k_ref[...] = (_dot(ckvn, wk_ref[...]) + _dot(kr.astype(BF16), place_ref[...])).astype(BF16)
    v = _dot(ckvn, wv_ref[...])
    ones = (lax.broadcasted_iota(jnp.int32, v.shape, 1) % 128) >= HEAD_DIM
    v_ref[...] = jnp.where(ones, 1.0, v).astype(BF16)


def _mla_weights(w_uq, w_ukv):
    qh = w_uq.reshape(MLA_Q_RANK, N_HEADS, MLA_NOPE + MLA_ROPE)
    wq = jnp.concatenate([qh, jnp.zeros((MLA_Q_RANK, N_HEADS, 32), F32)], axis=-1).reshape(MLA_Q_RANK, 512)
    kvh = w_ukv.reshape(MLA_KV_RANK, N_HEADS, MLA_NOPE + HEAD_DIM)
    zk = jnp.zeros((MLA_KV_RANK, N_HEADS, 64), F32)
    wk = jnp.concatenate([kvh[..., :MLA_NOPE], zk], axis=-1).reshape(MLA_KV_RANK, 512)
    wv = jnp.concatenate([kvh[..., MLA_NOPE:], zk], axis=-1).reshape(MLA_KV_RANK, 512)
    src = jnp.arange(128)[:, None]
    dst = jnp.arange(512)[None, :]
    place = ((dst % 128 == src + MLA_NOPE) & (src < MLA_ROPE)).astype(BF16)
    return wq.astype(BF16), wk.astype(BF16), wv.astype(BF16), place


def _mla_flash_kernel(q_ref, k_ref, v_ref, o_ref, m_ref, acc_ref, *, strip):
    ki = pl.program_id(2)

    @pl.when(ki == 0)
    def _():
        m_ref[...] = jnp.full_like(m_ref, -jnp.inf)
        acc_ref[...] = jnp.zeros_like(acc_ref)

    for h in range(N_HEADS):
        sl = slice(h * 128, (h + 1) * 128)
        k = k_ref[:, sl]
        v = v_ref[:, sl]
        for r in range(q_ref.shape[0] // strip):
            rows = slice(r * strip, (r + 1) * strip)
            s = _dot_nt(q_ref[rows, sl], k)
            m_prev = m_ref[h, rows]
            m_new = jnp.maximum(m_prev, jnp.max(s, axis=-1, keepdims=True))
            p = jnp.exp2(s - m_new[:, :1])
            acc_ref[h, rows] = jnp.exp2(m_prev - m_new) * acc_ref[h, rows] + _dot(p.astype(BF16), v)
            m_ref[h, rows] = m_new

    @pl.when(ki == pl.num_programs(2) - 1)
    def _():
        outs = []
        for h in range(N_HEADS):
            a = acc_ref[h]
            outs.append(a[:, :HEAD_DIM] / a[:, HEAD_DIM:])
        o_ref[...] = jnp.concatenate(outs, axis=-1)


def _mla(proj, q_norm_w, kv_norm_w, w_uq, w_ukv, q_tabs, k_tabs):
    b, s, _ = proj.shape
    wq, wk, wv, place = _mla_weights(w_uq, w_ukv)
    ts = 512
    row = lambda n: pl.BlockSpec((ts, n), lambda bi, i: (i, 0))
    full = lambda r, n: pl.BlockSpec((r, n), lambda bi, i: (0, 0))
    out = pl.BlockSpec((None, ts, 512), lambda bi, i: (bi, i, 0))
    q, k, v = pl.pallas_call(
        _mla_prep_kernel,
        grid=(b, s // ts),
        in_specs=[
            pl.BlockSpec((None, ts, BLK), lambda bi, i: (bi, i, COL_MCQ)),
            pl.BlockSpec((None, ts, BLK), lambda bi, i: (bi, i, COL_MKV)),
            full(1, MLA_Q_RANK), full(1, MLA_KV_RANK),
            full(MLA_Q_RANK, 512), full(MLA_KV_RANK, 512), full(MLA_KV_RANK, 512), full(128, 512),
            row(128), row(128), row(128), row(128), row(128), row(128),
        ],
        out_specs=[out, out, out],
        out_shape=[jax.ShapeDtypeStruct((b, s, 512), BF16)] * 3,
        compiler_params=_cparams(("parallel", "parallel")),
    )(proj, proj, q_norm_w[None], kv_norm_w[None], wq, wk, wv, place, *q_tabs, *k_tabs)

    tq, tk = math.gcd(s, 2048), math.gcd(s, 512)
    return pl.pallas_call(
        functools.partial(_mla_flash_kernel, strip=tq),
        grid=(b, s // tq, s // tk),
        in_specs=[
            pl.BlockSpec((None, tq, 512), lambda bi, qi, ki: (bi, qi, 0)),
            pl.BlockSpec((None, tk, 512), lambda bi, qi, ki: (bi, ki, 0)),
            pl.BlockSpec((None, tk, 512), lambda bi, qi, ki: (bi, ki, 0)),
        ],
        out_specs=pl.BlockSpec((None, tq, N_HEADS * HEAD_DIM), lambda bi, qi, ki: (bi, qi, 0)),
        out_shape=jax.ShapeDtypeStruct((b, s, N_HEADS * HEAD_DIM), F32),
        scratch_shapes=[pltpu.VMEM((N_HEADS, tq, 128), F32), pltpu.VMEM((N_HEADS, tq, 128), F32)],
        compiler_params=_cparams(("parallel", "parallel", "arbitrary")),
    )(q, k, v)


def _conv_kernel(prev_ref, cur_ref, next_ref, w_ref, b_ref, o_ref, *, ts):
    i = pl.program_id(1)
    w = w_ref[...]

    def conv(x):
        n = x.shape[0]
        acc = x * w[2:3]
        for s in (-2, -1, 1, 2):
            acc = acc + pltpu.roll(x, (-s) % n, 0) * w[s + 2:s + 3]
        return acc

    act = lambda y: _silu(y + b_ref[...])
    cur = cur_ref[...]
    prev = prev_ref[...] * (i > 0).astype(F32)
    nxt = next_ref[...] * (i < pl.num_programs(1) - 1).astype(F32)
    o_ref[...] = act(conv(cur))
    top = conv(jnp.concatenate([prev, cur[:16]], axis=0))
    o_ref[0:8, :] = act(top[8:16])
    bot = conv(jnp.concatenate([cur[ts - 16:], nxt], axis=0))
    o_ref[ts - 8:ts, :] = act(bot[8:16])


def _ssd_conv(proj, conv_w, conv_b):
    b, s, _ = proj.shape
    ts = 512
    width = 3 * BLK
    cb = COL_XBC * BLK // width
    nb8 = ts // 8
    w8 = jnp.concatenate([conv_w, jnp.zeros((8 - SSM_CONV, width), F32)], axis=0)
    return pl.pallas_call(
        functools.partial(_conv_kernel, ts=ts),
        grid=(b, s // ts),
        in_specs=[
            pl.BlockSpec((None, 8, width), lambda bi, i: (bi, jnp.maximum(i * nb8 - 1, 0), cb)),
            pl.BlockSpec((None, ts, width), lambda bi, i: (bi, i, cb)),
            pl.BlockSpec((None, 8, width), lambda bi, i: (bi, jnp.minimum((i + 1) * nb8, s // 8 - 1), cb)),
            pl.BlockSpec((8, width), lambda bi, i: (0, 0)),
            pl.BlockSpec((1, width), lambda bi, i: (0, 0)),
        ],
        out_specs=pl.BlockSpec((None, ts, width), lambda bi, i: (bi, i, 0)),
        out_shape=jax.ShapeDtypeStruct((b, s, width), F32),
        compiler_params=_cparams(("parallel", "parallel")),
    )(proj, proj, proj, w8, conv_b[None])


def _ssd_tables():
    tril = _tri(CHUNK, lambda i, j: j <= i)
    triu = _tri(CHUNK, lambda i, j: j >= i)
    h = jnp.arange(128)[:, None]
    lane = jnp.arange(256)[None, :]
    e_f = ((lane // HEAD_DIM == h) & (h < N_HEADS)).astype(BF16)
    e_b = ((lane // HEAD_DIM == h - N_HEADS) & (h >= N_HEADS) & (h < 2 * N_HEADS)).astype(BF16)
    return tril, triu, e_f, e_b


def _ssd_kernel(xs_ref, bm_ref, cm_ref, z_ref, dt_ref, dtt_ref, bias_ref, a_ref, biasc_ref, ac_ref,
                dskip_ref, nw_ref, tril_ref, triu_ref, ef_ref, eb_ref, o_ref,
                sf_ref, sb_ref, pb_ref, *, nsteps, g):
    t = pl.program_id(1)
    tril, triu = tril_ref[...], triu_ref[...]

    @pl.when(t == 0)
    def _():
        sf_ref[...] = jnp.zeros_like(sf_ref)
        sb_ref[...] = jnp.zeros_like(sb_ref)

    chunks = [slice(ci * CHUNK, (ci + 1) * CHUNK) for ci in range(g)]

    def step_sizes():
        dt = _softplus(dt_ref[:, :128] + bias_ref[...])
        return dt, dt * a_ref[...]

    def cumulative(tri, dta, expand):
        return _dot_x_exact(jnp.concatenate([_dot_exact_x(tri, dta[r]) for r in chunks], axis=0), expand)

    def state_update(s_ref, bmf, weighted_x, total):
        for gi in range(SSM_GROUPS):
            sl = slice(gi * 128, (gi + 1) * 128)
            upd = _dot(bmf[:, sl].T.astype(BF16), weighted_x[:, sl].astype(BF16))
            s_ref[gi] = s_ref[gi] * jnp.exp(total[:, sl]) + upd

    @pl.when(t < nsteps)
    def _():
        base = (nsteps - 1 - t) * g
        dt, dta = step_sizes()
        rcs_all = cumulative(triu, dta, eb_ref[...])
        xdt_all = xs_ref[...] * _dot_x_exact(dt, eb_ref[...])
        for ci in reversed(range(g)):
            rows = chunks[ci]
            rcs_b = rcs_all[rows]
            tot_b = rcs_b[0:1, :]
            pb_ref[base + ci] = sb_ref[...].astype(BF16)
            state_update(sb_ref, bm_ref[rows, :], jnp.exp(tot_b - rcs_b) * xdt_all[rows], tot_b)

    @pl.when(t >= nsteps)
    def _():
        base = (t - nsteps) * g
        row = lax.broadcasted_iota(jnp.int32, (CHUNK, CHUNK), 0)
        colm = lax.broadcasted_iota(jnp.int32, (CHUNK, CHUNK), 1)
        head = _lane_head((CHUNK, 128), HEAD_DIM)
        efb = jnp.concatenate([ef_ref[...], eb_ref[...]], axis=1)
        dt, dta = step_sizes()
        rcs_all = cumulative(triu, dta, eb_ref[...])
        cs_all = cumulative(tril, dta, ef_ref[...])
        dt_exp = _dot_x_exact(dt, efb)
        dtat = jnp.concatenate([dtt_ref[:, r] for r in chunks], axis=0)
        per_chunk = lambda col: jnp.concatenate([col] * g, axis=0)
        dtat = _softplus(dtat + per_chunk(biasc_ref[...])) * per_chunk(ac_ref[...])
        cst_all = _dot_x_exact(dtat, triu)
        rcst_all = _dot_x_exact(dtat, tril)
        for ci in range(g):
            rows = chunks[ci]
            xs = xs_ref[rows, :]
            bmf = bm_ref[rows, :]
            bm = bmf.astype(BF16)
            cm = cm_ref[rows, :].astype(BF16)
            rcs_b = rcs_all[rows]
            cs_f = cs_all[rows]
            tot_f = cs_f[CHUNK - 1:CHUNK, :]
            xdt_f = xs * dt_exp[rows, :SSM_INNER]
            xdt_b = xs * dt_exp[rows, SSM_INNER:]
            cst = cst_all[8 * ci:8 * ci + 8]
            rcst = rcst_all[8 * ci:8 * ci + 8]
            pb = pb_ref[base + ci]
            ys = []
            for gi in range(SSM_GROUPS):
                sl = slice(gi * 128, (gi + 1) * 128)
                cb = _dot_nt(cm[:, sl], bm[:, sl])
                xcat = jnp.concatenate([xdt_f[:, sl], xdt_b[:, sl]], axis=0).astype(BF16)
                wcats = []
                for hh in range(2):
                    h = 2 * gi + hh
                    c0 = h * HEAD_DIM
                    seg_f = cs_f[:, c0:c0 + 1] - cst[h:h + 1, :]
                    seg_b = rcs_b[:, c0:c0 + 1] - rcst[N_HEADS + h:N_HEADS + h + 1, :]
                    dec_f = jnp.where(row >= colm, jnp.exp(jnp.minimum(seg_f, 0.0)), 0.0)
                    dec_b = jnp.where(row < colm, jnp.exp(jnp.minimum(seg_b, 0.0)), 0.0)
                    wcats.append(jnp.concatenate([cb * dec_f, cb * dec_b], axis=1))
                y2 = _dot(jnp.concatenate(wcats, axis=0).astype(BF16), xcat)
                yg = jnp.where(head == 0, y2[:CHUNK], y2[CHUNK:])
                states = jnp.concatenate([sf_ref[gi].astype(BF16), pb[gi]], axis=1)
                off = _dot(cm[:, sl], states)
                ys.append(yg + jnp.exp(cs_f[:, sl]) * off[:, :128] + jnp.exp(rcs_b[:, sl]) * off[:, 128:])
            y = jnp.concatenate(ys, axis=1) + dskip_ref[...] * xs
            y = y * _silu(z_ref[rows, :])
            o_ref[rows, :] = _rms(y, nw_ref[...])
            state_update(sf_ref, bmf, jnp.exp(tot_f - cs_f) * xdt_f, tot_f)


def _ssd(proj, xbc, a_log, dt_bias, d_skip, norm_w, tables):
    b, s, _ = proj.shape
    nc = s // CHUNK
    g = math.gcd(nc, 4)
    nsteps = nc // g
    rows = g * CHUNK
    dt_t = jnp.swapaxes(proj[:, :, COL_DT * BLK:COL_DT * BLK + 8], 1, 2)
    pad_row = lambda v: jnp.concatenate([v.reshape(1, 8), jnp.zeros((1, 120), F32)], axis=1)
    a = -jnp.exp(a_log.astype(F32))
    both = lambda t: jnp.where(t < nsteps, nsteps - 1 - t, t - nsteps)
    late = lambda t: jnp.where(t < nsteps, 0, t - nsteps)
    full = lambda r, n: pl.BlockSpec((r, n), lambda bi, t: (0, 0))
    blk = lambda idx, c: pl.BlockSpec((None, rows, BLK), lambda bi, t: (bi, idx(t), c))
    return pl.pallas_call(
        functools.partial(_ssd_kernel, nsteps=nsteps, g=g),
        grid=(b, 2 * nsteps),
        in_specs=[
            blk(both, 0), blk(both, 1), blk(late, 2), blk(late, COL_Z), blk(both, COL_DT),
            pl.BlockSpec((None, 8, rows), lambda bi, t: (bi, 0, late(t))),
            full(1, 128), full(1, 128), full(8, 1), full(8, 1),
            full(1, SSM_INNER), full(1, SSM_INNER),
            full(CHUNK, CHUNK), full(CHUNK, CHUNK), full(128, 256), full(128, 256),
        ],
        out_specs=blk(late, 0),
        out_shape=jax.ShapeDtypeStruct((b, s, SSM_INNER), F32),
        scratch_shapes=[pltpu.VMEM((SSM_GROUPS, 128, 128), F32), pltpu.VMEM((SSM_GROUPS, 128, 128), F32),
                        pltpu.VMEM((nc, SSM_GROUPS, 128, 128), BF16)],
        compiler_params=_cparams(("parallel", "arbitrary")),
    )(xbc, xbc, xbc, proj, proj, dt_t, pad_row(dt_bias), pad_row(a), dt_bias.reshape(8, 1), a.reshape(8, 1),
      jnp.repeat(d_skip, HEAD_DIM)[None], norm_w[None], *tables)


def _dil_kernel(q_ref, k_ref, v_ref, o_ref, m_ref, l_ref, a_ref, *, s):
    head = _lane_head((128, 128), HEAD_DIM)
    kw = 2 * 128
    for pi, (win, d) in enumerate(DIL_PATTERNS):
        half = win // (2 * d)
        seg = s // d
        per_seg = seg // 128

        def body(i, carry, d=d, pi=pi, half=half, seg=seg, per_seg=per_seg):
            r = i // per_seg
            m0 = (i % per_seg) * 128
            ks = jnp.clip(m0 - half, 0, seg - kw)
            if d == 1:
                qrows = pl.ds(pl.multiple_of(m0, 128), 128)
                krows = pl.ds(pl.multiple_of(ks, 64), kw)
            else:
                qrows = pl.ds(r + d * m0, 128, stride=d)
                krows = pl.ds(r + d * ks, kw, stride=d)
            q = q_ref[qrows, :]
            kb = k_ref[krows, :].astype(BF16)
            vb = v_ref[krows, :].astype(BF16)
            qpos = m0 + lax.broadcasted_iota(jnp.int32, (256, kw), 0) % 128
            kpos = ks + lax.broadcasted_iota(jnp.int32, (256, kw), 1)
            valid = jnp.abs(qpos - kpos) <= half
            q2 = jnp.concatenate([jnp.where(head == hh, q, 0.0) for hh in range(2)], axis=0).astype(BF16)
            sc = jnp.where(valid, _dot_nt(q2, kb), -jnp.inf)
            m2 = jnp.max(sc, axis=-1, keepdims=True)
            p = jnp.exp(sc - m2)
            l2 = jnp.sum(p, axis=-1, keepdims=True)
            a2 = _dot(p.astype(BF16), vb)
            m_ref.at[pi][qrows, :] = jnp.where(head == 0, m2[:128], m2[128:])
            l_ref.at[pi][qrows, :] = jnp.where(head == 0, l2[:128], l2[128:])
            a_ref.at[pi][qrows, :] = jnp.where(head == 0, a2[:128], a2[128:])
            return carry

        lax.fori_loop(0, s // 128, body, 0, unroll=4)

    def merge(i, carry):
        rows = pl.ds(pl.multiple_of(i * 512, 512), 512)
        ms = [m_ref[pi, rows, :] for pi in range(len(DIL_PATTERNS))]
        top = functools.reduce(jnp.maximum, ms)
        ws = [jnp.exp(m - top) for m in ms]
        den = sum(w * l_ref[pi, rows, :] for pi, w in enumerate(ws))
        num = sum(w * a_ref[pi, rows, :] for pi, w in enumerate(ws))
        o_ref[rows, :] = num / den
        return carry

    lax.fori_loop(0, s // 512, merge, 0)


def _dilated(proj):
    b, s, _ = proj.shape
    assert all(s // d >= 256 for _, d in DIL_PATTERNS)
    seqb = lambda c: pl.BlockSpec((None, s, 128), lambda bi, p: (bi, 0, 2 * c + p))
    return pl.pallas_call(
        functools.partial(_dil_kernel, s=s),
        grid=(b, 2),
        in_specs=[seqb(COL_DQ), seqb(COL_DK), seqb(COL_DV)],
        out_specs=seqb(0),
        out_shape=jax.ShapeDtypeStruct((b, s, BLK), F32),
        scratch_shapes=[pltpu.VMEM((len(DIL_PATTERNS), s, 128), F32)] * 3,
        compiler_params=_cparams(("parallel", "parallel")),
    )(proj, proj, proj)


def _outproj_kernel(*refs, n_res):
    ya_ref, yb_ref, yc_ref, yd_ref, w_ref, lnw_ref, rw_ref, xo_ref, h_ref, aff_ref = refs[n_res:]
    acc = refs[0][...]
    for r in refs[1:n_res]:
        acc = acc + r[...]
    mixed = jnp.concatenate([r[...].astype(BF16) for r in (ya_ref, yb_ref, yc_ref, yd_ref)], axis=1)
    acc = acc + _dot(mixed, w_ref[...])
    xo_ref[...] = acc
    h = _rms(acc, lnw_ref[...])
    h_hi = h.astype(BF16)
    h_ref[...] = h_hi
    h_lo = (h - h_hi.astype(F32)).astype(BF16)
    rw = rw_ref[...]
    w_hi = rw.astype(BF16)
    w_lo = (rw - w_hi.astype(F32)).astype(BF16)
    logits = _dot(h_hi, w_hi) + _dot(h_hi, w_lo) + _dot(h_lo, w_hi) + _dot(h_lo, w_lo)
    lane = lax.broadcasted_iota(jnp.int32, logits.shape, 1)
    logits = jnp.where(lane < N_EXPERTS, logits, -jnp.inf)
    e = jnp.exp(logits - jnp.max(logits, axis=-1, keepdims=True))
    aff = e / jnp.sum(e, axis=-1, keepdims=True)
    aff_ref[...] = aff.T[:N_EXPERTS, :]


def _outproj(res, ys, w_out, ln2_w, router_w):
    t = res[0].shape[0]
    tm = 256
    rw = jnp.concatenate([router_w, jnp.zeros((D_MODEL, 128 - N_EXPERTS), F32)], axis=1)
    row = lambda n: pl.BlockSpec((tm, n), lambda i: (i, 0))
    full = lambda r, n: pl.BlockSpec((r, n), lambda i: (0, 0))
    return pl.pallas_call(
        functools.partial(_outproj_kernel, n_res=len(res)),
        grid=(t // tm,),
        in_specs=[row(D_MODEL)] * len(res) + [row(BLK)] * 4 + [
            full(D_MODEL, D_MODEL), full(1, D_MODEL), full(D_MODEL, 128)],
        out_specs=[row(D_MODEL), row(D_MODEL), pl.BlockSpec((N_EXPERTS, tm), lambda i: (0, i))],
        out_shape=[jax.ShapeDtypeStruct((t, D_MODEL), F32),
                   jax.ShapeDtypeStruct((t, D_MODEL), BF16),
                   jax.ShapeDtypeStruct((N_EXPERTS, t), F32)],
        compiler_params=_cparams(("parallel",)),
    )(*res, *ys, w_out.astype(BF16), ln2_w[None], rw)


def _exclusive_prefix(x, tri):
    n = x.shape[1] // 128
    off = jnp.zeros((x.shape[0], 1), F32)
    parts = []
    for i in range(n):
        xt = x[:, i * 128:(i + 1) * 128]
        incl = _dot(xt.astype(BF16), tri)
        parts.append(incl - xt + off)
        off = off + incl[:, 127:128]
    return jnp.concatenate(parts, axis=1)


def _select_kernel(aff_ref, tri_ref, key_ref, *, cap):
    a = aff_ref[...]
    n_exp = a.shape[0]
    thr_bits = jnp.zeros((n_exp, 1), jnp.int32)
    for bit in range(30, -1, -1):
        cand = thr_bits | (1 << bit)
        cnt = jnp.sum(jnp.where(a >= lax.bitcast_convert_type(cand, F32), 1.0, 0.0), axis=1, keepdims=True)
        thr_bits = jnp.where(cnt >= cap, cand, thr_bits)
    thr = lax.bitcast_convert_type(thr_bits, F32)
    above = jnp.where(a > thr, 1.0, 0.0)
    tie = jnp.where(a == thr, 1.0, 0.0)
    need = cap - jnp.sum(above, axis=1, keepdims=True)
    tri = tri_ref[...]
    sel = above + tie * jnp.where(_exclusive_prefix(tie, tri) < need, 1.0, 0.0)
    key_ref[...] = jnp.where(sel > 0.0, _exclusive_prefix(sel, tri), -1.0)


def _slot_kernel(key_ref, aff_ref, ltri_ref, idx_ref, gate_ref, *, cap, nt):
    n_exp = key_ref.shape[0]
    pad = jnp.zeros((128 - nt, 128), F32)
    slot = lax.broadcasted_iota(jnp.int32, (128, cap), 1).astype(F32)
    sub = lax.broadcasted_iota(jnp.int32, (128, cap), 0).astype(F32)
    ltri = ltri_ref[...]

    def per_expert(e, carry):
        key = jnp.concatenate([key_ref[e], pad - 1.0], axis=0)
        aff = jnp.concatenate([aff_ref[e], pad], axis=0)
        cnt = jnp.sum(jnp.where(key >= 0.0, 1.0, 0.0), axis=1, keepdims=True)
        cum = _dot(ltri, jnp.broadcast_to(cnt, (128, 128)).astype(BF16))
        in_tile = (jnp.where(slot >= (cum - cnt)[:, :1], 1.0, 0.0)
                   * jnp.where(slot < cum[:, :1], 1.0, 0.0))
        g = in_tile.astype(BF16)
        k_hi, k_lo = _split2(key.T)
        slot_of = _dot(k_hi, g) + _dot(k_lo, g)
        hit = jnp.where(slot_of == slot, 1.0, 0.0)
        aff_t = aff.T
        a_hi = aff_t.astype(BF16)
        r1 = aff_t - a_hi.astype(F32)
        a_mid = r1.astype(BF16)
        a_lo = (r1 - a_mid.astype(F32)).astype(BF16)
        aff_of = _dot(a_hi, g) + _dot(a_mid, g) + _dot(a_lo, g)
        tile = jnp.sum(in_tile * sub, axis=0, keepdims=True)
        within = jnp.sum(hit * sub, axis=0, keepdims=True)
        idx_ref[pl.ds(e, 1), :] = (128.0 * tile + within).astype(jnp.int32)
        gate_ref[pl.ds(e, 1), :] = jnp.sum(hit * aff_of, axis=0, keepdims=True)
        return carry

    lax.fori_loop(0, n_exp, per_expert, 0, unroll=4)


def _moe_select(aff_t, b, s, cap):
    nt = s // 128
    assert nt <= 128 and nt % 8 == 0 and cap % 128 == 0
    key = pl.pallas_call(
        functools.partial(_select_kernel, cap=cap),
        grid=(b,),
        in_specs=[pl.BlockSpec((N_EXPERTS, s), lambda bi: (0, bi)),
                  pl.BlockSpec((128, 128), lambda bi: (0, 0))],
        out_specs=pl.BlockSpec((None, N_EXPERTS, s), lambda bi: (bi, 0, 0)),
        out_shape=jax.ShapeDtypeStruct((b, N_EXPERTS, s), F32),
        compiler_params=_cparams(("parallel",)),
    )(aff_t, _tri(128, lambda i, j: i <= j))
    out = pl.BlockSpec((None, N_EXPERTS, cap), lambda bi: (bi, 0, 0))
    return pl.pallas_call(
        functools.partial(_slot_kernel, cap=cap, nt=nt),
        grid=(b,),
        in_specs=[pl.BlockSpec((None, N_EXPERTS, nt, 128), lambda bi: (bi, 0, 0, 0)),
                  pl.BlockSpec((N_EXPERTS, nt, 128), lambda bi: (0, bi, 0)),
                  pl.BlockSpec((128, 128), lambda bi: (0, 0))],
        out_specs=[out, out],
        out_shape=[jax.ShapeDtypeStruct((b, N_EXPERTS, cap), jnp.int32),
                   jax.ShapeDtypeStruct((b, N_EXPERTS, cap), F32)],
        compiler_params=_cparams(("parallel",)),
    )(key.reshape(b, N_EXPERTS, nt, 128), aff_t.reshape(N_EXPERTS, b * nt, 128), _tri(128, lambda i, j: j <= i))


ROWS_PER_ITER = 8


def _gather_kernel(idx_ref, h_ref, o_ref, hf_ref, buf_ref, *, cap, n_exp):
    e = pl.program_id(1)
    base = (pl.program_id(0) * n_exp + e) * cap

    @pl.when(e == 0)
    def _():
        hf_ref[...] = h_ref[...].astype(F32)

    def body(jb, carry):
        j0 = pl.multiple_of(jb * ROWS_PER_ITER, ROWS_PER_ITER)
        vals = [hf_ref[pl.ds(idx_ref[base + j0 + u], 1), :] for u in range(ROWS_PER_ITER)]
        for u in range(ROWS_PER_ITER):
            buf_ref[pl.ds(j0 + u, 1), :] = vals[u]
        return carry

    lax.fori_loop(0, cap // ROWS_PER_ITER, body, 0)
    o_ref[...] = buf_ref[...].astype(BF16)


def _moe_gather(idx_flat, h, cap):
    b, s, d = h.shape
    return pl.pallas_call(
        functools.partial(_gather_kernel, cap=cap, n_exp=N_EXPERTS),
        grid_spec=pltpu.PrefetchScalarGridSpec(
            num_scalar_prefetch=1,
            grid=(b, N_EXPERTS),
            in_specs=[pl.BlockSpec((None, s, d), lambda bi, e, idx: (bi, 0, 0))],
            out_specs=pl.BlockSpec((None, None, cap, d), lambda bi, e, idx: (e, bi, 0, 0)),
            scratch_shapes=[pltpu.VMEM((s, d), F32), pltpu.VMEM((cap, d), F32)],
        ),
        out_shape=jax.ShapeDtypeStruct((N_EXPERTS, b, cap, d), BF16),
        compiler_params=_cparams(("parallel", "arbitrary")),
    )(idx_flat, h)


def _ffn_kernel(x_ref, wg_ref, wu_ref, wd_ref, o_ref, *, rows, tr):
    f = pl.program_id(1)
    wg = wg_ref[...].astype(BF16)
    wu = wu_ref[...].astype(BF16)
    wd = wd_ref[...].astype(BF16)

    def partial_out(sl):
        x = x_ref[sl, :]
        hid = (_silu(_dot(x, wg)) * _dot(x, wu)).astype(BF16)
        return _dot(hid, wd)

    for r in range(rows // tr):
        sl = slice(r * tr, (r + 1) * tr)

        @pl.when(f == 0)
        def _():
            o_ref[sl, :] = partial_out(sl)

        @pl.when(f > 0)
        def _():
            o_ref[sl, :] += partial_out(sl)


def _moe_ffn(xin, w_gate, w_up, w_down, layer):
    n_exp, rows, d = xin.shape
    tf = 512
    return pl.pallas_call(
        functools.partial(_ffn_kernel, rows=rows, tr=math.gcd(rows, 1024)),
        grid=(n_exp, EXPERT_FF // tf),
        in_specs=[
            pl.BlockSpec((None, rows, d), lambda e, f: (e, 0, 0)),
            pl.BlockSpec((None, None, d, tf), lambda e, f: (layer, e, 0, f)),
            pl.BlockSpec((None, None, d, tf), lambda e, f: (layer, e, 0, f)),
            pl.BlockSpec((None, None, tf, d), lambda e, f: (layer, e, f, 0)),
        ],
        out_specs=pl.BlockSpec((None, rows, d), lambda e, f: (e, 0, 0)),
        out_shape=jax.ShapeDtypeStruct((n_exp, rows, d), F32),
        compiler_params=_cparams(("parallel", "arbitrary")),
    )(xin, w_gate, w_up, w_down)


def _combine_kernel(idx_ref, gate_ref, y_ref, o_ref, *, cap, n_exp):
    e = pl.program_id(1)
    base = (pl.program_id(0) * n_exp + e) * cap

    @pl.when(e == 0)
    def _():
        o_ref[...] = jnp.zeros_like(o_ref)

    def body(jb, carry):
        j0 = pl.multiple_of(jb * ROWS_PER_ITER, ROWS_PER_ITER)
        rows = [idx_ref[base + j0 + u] for u in range(ROWS_PER_ITER)]
        vals = [o_ref[pl.ds(rows[u], 1), :] + gate_ref[base + j0 + u] * y_ref[pl.ds(j0 + u, 1), :]
                for u in range(ROWS_PER_ITER)]
        for u in range(ROWS_PER_ITER):
            o_ref[pl.ds(rows[u], 1), :] = vals[u]
        return carry

    lax.fori_loop(0, cap // ROWS_PER_ITER, body, 0)


def _moe_combine(idx_flat, gate_flat, y, b, s, cap):
    d = y.shape[-1]
    return pl.pallas_call(
        functools.partial(_combine_kernel, cap=cap, n_exp=N_EXPERTS),
        grid_spec=pltpu.PrefetchScalarGridSpec(
            num_scalar_prefetch=2,
            grid=(b, N_EXPERTS),
            in_specs=[pl.BlockSpec((None, None, cap, d), lambda bi, e, idx, gate: (e, bi, 0, 0))],
            out_specs=pl.BlockSpec((None, s, d), lambda bi, e, idx, gate: (bi, 0, 0)),
        ),
        out_shape=jax.ShapeDtypeStruct((b, s, d), F32),
        compiler_params=_cparams(("parallel", "arbitrary")),
    )(idx_flat, gate_flat, y)


def _final_norm_kernel(xa_ref, xb_ref, w_ref, o_ref):
    o_ref[...] = _rms(xa_ref[...] + xb_ref[...], w_ref[...])


def _final_norm(xa, xb, w):
    t = xa.shape[0]
    tm = 1024
    row = pl.BlockSpec((tm, D_MODEL), lambda i: (i, 0))
    return pl.pallas_call(
        _final_norm_kernel,
        grid=(t // tm,),
        in_specs=[row, row, pl.BlockSpec((1, D_MODEL), lambda i: (0, 0))],
        out_specs=row,
        out_shape=jax.ShapeDtypeStruct((t, D_MODEL), F32),
        compiler_params=_cparams(("parallel",)),
    )(xa, xb, w[None])


def _moe(h, aff, w_gate, w_up, w_down, layer, b, s):
    d = h.shape[-1]
    cap = EC_CAPACITY_FACTOR * s // N_EXPERTS
    token_idx, gate = _moe_select(aff, b, s, cap)
    idx_flat = token_idx.reshape(-1)
    xin = _moe_gather(idx_flat, h.reshape(b, s, d), cap)
    y = _moe_ffn(xin.reshape(N_EXPERTS, b * cap, d), w_gate, w_up, w_down, layer)
    return _moe_combine(idx_flat, gate.reshape(-1), y.reshape(N_EXPERTS, b, cap, d), b, s, cap)


def kernel(x, ln1_w, w_in, mla_q_norm_w, mla_kv_norm_w, mla_w_uq, mla_w_ukv, ssm_conv_w, ssm_conv_b,
           ssm_a_log, ssm_dt_bias, ssm_d, ssm_norm_w, w_out, ln2_w, router_w, exp_w_gate, exp_w_up,
           exp_w_down, final_norm_w):
    b, s, d = x.shape
    depth = w_in.shape[0]
    ret_tabs = _ret_tables()
    ret_rope = _rope_tables(s, HEAD_DIM, RET_THETA, HEAD_DIM, 0, 2)
    mla_q_rope = _rope_tables(s, MLA_ROPE, ROPE_THETA, 128, MLA_NOPE, 1)
    mla_k_rope = _rope_tables(s, MLA_ROPE, ROPE_THETA, 128, 0, 1)
    dil_rope = _rope_tables(s, ROPE_DIM, ROPE_THETA, HEAD_DIM, 0, 2)
    ssd_tabs = _ssd_tables()
    res = [x.reshape(b * s, d)]
    for i in range(depth):
        proj = _inproj(res, ln1_w[i][None], _pad_cols(w_in[i]).astype(BF16), ret_rope, dil_rope, s)
        proj = proj.reshape(b, s, PROJ_PAD)
        y_a = _retention(proj, ret_tabs)
        y_b = _mla(proj, mla_q_norm_w[i], mla_kv_norm_w[i], mla_w_uq[i], mla_w_ukv[i], mla_q_rope, mla_k_rope)
        xbc = _ssd_conv(proj, ssm_conv_w[i], ssm_conv_b[i])
        y_c = _ssd(proj, xbc, ssm_a_log[i], ssm_dt_bias[i], ssm_d[i], ssm_norm_w[i], ssd_tabs)
        y_d = _dilated(proj)
        ys = [y.reshape(b * s, BLK) for y in (y_a, y_b, y_c, y_d)]
        x_mid, h, aff = _outproj(res, ys, w_out[i], ln2_w[i], router_w[i])
        moe = _moe(h, aff, exp_w_gate, exp_w_up, exp_w_down, i, b, s)
        res = [x_mid, moe.reshape(b * s, d)]
    return _final_norm(res[0], res[1], final_norm_w).reshape(b, s, d)
```

```python
import functools
import math

import jax
import jax.numpy as jnp
from jax import lax
from jax.experimental import pallas as pl
from jax.experimental.pallas import tpu as pltpu

F32 = jnp.float32
BF16 = jnp.bfloat16

D_MODEL = 1024
RMS_EPS = 1e-6
GN_EPS = 1e-5
CHUNK = 128
HEAD_DIM = 64
N_HEADS = 4

RET_THETA = 10000.0
ROPE_THETA = 500000.0
ROPE_DIM = 16

MLA_Q_RANK = 256
MLA_KV_RANK = 128
MLA_NOPE = 64
MLA_ROPE = 32

SSM_GROUPS = 2
SSM_STATE = 128
SSM_CONV = 5
SSM_INNER = 256

DIL_PATTERNS = ((128, 1), (512, 4), (2048, 16))

N_EXPERTS = 16
EXPERT_FF = 2048
EC_CAPACITY_FACTOR = 2

BLK = 256
COL_RQ, COL_RK, COL_RV, COL_RG = 0, 1, 2, 3
COL_MCQ, COL_MKV = 4, 5
COL_XBC = 6
COL_Z = 9
COL_DT = 10
COL_DQ, COL_DK, COL_DV = 11, 12, 13
PROJ_PAD = 14 * BLK

VMEM_LIMIT = 56 * 1024 * 1024


def _cparams(sem):
    return pltpu.CompilerParams(dimension_semantics=sem, vmem_limit_bytes=VMEM_LIMIT)


def _split2(x):
    hi = x.astype(BF16)
    lo = (x - hi.astype(F32)).astype(BF16)
    return hi, lo


def _dot(a, b):
    return jnp.dot(a, b, preferred_element_type=F32)


def _dot_nt(a, b):
    return lax.dot_general(a, b, (((1,), (1,)), ((), ())), preferred_element_type=F32)


def _dot_x_exact(x, m):
    hi, lo = _split2(x)
    return _dot(jnp.concatenate([hi, lo], axis=1), jnp.concatenate([m, m], axis=0))


def _dot_exact_x(m, x):
    hi, lo = _split2(x)
    n = x.shape[1]
    r = _dot(m, jnp.concatenate([hi, lo], axis=1))
    return r[:, :n] + r[:, n:]


def _rope(x, c, s1, s2, half):
    w = x.shape[-1]
    return x * c + pltpu.roll(x, w - half, 1) * s1 + pltpu.roll(x, half, 1) * s2


def _silu(x):
    return x * (1.0 / (1.0 + jnp.exp(-x)))


def _softplus(x):
    return jnp.maximum(x, 0.0) + jnp.log(1.0 + jnp.exp(-jnp.abs(x)))


def _lane_head(shape, width):
    return lax.broadcasted_iota(jnp.int32, shape, 1) // width


def _rms(x, w):
    return x * lax.rsqrt(jnp.mean(x * x, axis=-1, keepdims=True) + RMS_EPS) * w


def _rope_tables(s, rot_dim, theta, head_dim, offset, n_heads):
    half = rot_dim // 2
    inv = 1.0 / (theta ** (jnp.arange(0, rot_dim, 2, dtype=F32) / rot_dim))
    ang = jnp.arange(s, dtype=F32)[:, None] * inv[None, :]
    cos, sin = jnp.cos(ang), jnp.sin(ang)
    pre0 = jnp.zeros((s, offset), F32)
    pre1 = jnp.ones((s, offset), F32)
    post0 = jnp.zeros((s, head_dim - offset - rot_dim), F32)
    post1 = jnp.ones((s, head_dim - offset - rot_dim), F32)
    zh = jnp.zeros((s, half), F32)
    c = jnp.concatenate([pre1, cos, cos, post1], axis=-1)
    s1 = jnp.concatenate([pre0, -sin, zh, post0], axis=-1)
    s2 = jnp.concatenate([pre0, zh, sin, post0], axis=-1)
    tile = lambda t: jnp.tile(t, (1, n_heads))
    return tile(c), tile(s1), tile(s2)


def _tri(n, fn):
    i = jnp.arange(n)
    return fn(i[:, None], i[None, :]).astype(BF16)


def _inproj_kernel(*refs, n_res, col_chunk):
    lnw_ref, w_ref, rc_ref, rs1_ref, rs2_ref, dc_ref, ds1_ref, ds2_ref, o_ref = refs[n_res:]
    x = refs[0][...]
    for r in refs[1:n_res]:
        x = x + r[...]
    yb = _rms(x, lnw_ref[...]).astype(BF16)
    for j in range(PROJ_PAD // col_chunk):
        sl = slice(j * col_chunk, (j + 1) * col_chunk)
        o_ref[:, sl] = _dot(yb, w_ref[:, sl])
    wide = lambda tabs: [jnp.concatenate([r[...]] * 2, axis=1) for r in tabs]
    ret = wide((rc_ref, rs1_ref, rs2_ref))
    dil = wide((dc_ref, ds1_ref, ds2_ref))
    for col, tabs, half, scale in ((COL_RQ, ret, HEAD_DIM // 2, HEAD_DIM ** -0.5), (COL_RK, ret, HEAD_DIM // 2, None),
                                   (COL_DQ, dil, ROPE_DIM // 2, HEAD_DIM ** -0.5), (COL_DK, dil, ROPE_DIM // 2, None)):
        sl = slice(col * BLK, (col + 1) * BLK)
        y = _rope(o_ref[:, sl], *tabs, half)
        o_ref[:, sl] = y if scale is None else y * scale


def _inproj(res, lnw, w_pad, ret_rope, dil_rope, s):
    t = res[0].shape[0]
    tm = math.gcd(s, 512)
    tab = pl.BlockSpec((tm, 128), lambda i: (i % (s // tm), 0))
    return pl.pallas_call(
        functools.partial(_inproj_kernel, n_res=len(res), col_chunk=512),
        grid=(t // tm,),
        in_specs=[pl.BlockSpec((tm, D_MODEL), lambda i: (i, 0))] * len(res) + [
            pl.BlockSpec((1, D_MODEL), lambda i: (0, 0)),
            pl.BlockSpec((D_MODEL, PROJ_PAD), lambda i: (0, 0)),
        ] + [tab] * 6,
        out_specs=pl.BlockSpec((tm, PROJ_PAD), lambda i: (i, 0)),
        out_shape=jax.ShapeDtypeStruct((t, PROJ_PAD), F32),
        compiler_params=_cparams(("parallel",)),
    )(*res, lnw, w_pad, *ret_rope, *dil_rope)


def _pad_cols(w):
    sizes = (256, 256, 256, 256, 256, 128, 32, 256, 768, 8, 256, 256, 256)
    pts, acc = [], 0
    for sz in sizes:
        pts.append((acc, acc + sz))
        acc += sz
    seg = lambda i: w[:, pts[i][0]:pts[i][1]]
    z = lambda n: jnp.zeros((w.shape[0], n), w.dtype)
    cols = [seg(0), seg(1), seg(2), seg(3), seg(4),
            seg(5), seg(6), z(BLK - 128 - 32),
            seg(8), seg(7), seg(9), z(BLK - 8),
            seg(10), seg(11), seg(12)]
    return jnp.concatenate(cols, axis=1)


def _ret_tables():
    pos = jnp.arange(CHUNK, dtype=F32)
    hh = jnp.arange(N_HEADS, dtype=F32)
    lg_f = jnp.log1p(-jnp.exp2(-5.0 - hh))
    lg_b = jnp.log1p(-jnp.exp2(-5.5 - hh))
    diff = pos[:, None] - pos[None, :]
    d_f = jnp.where(diff >= 0, jnp.exp(lg_f[:, None, None] * jnp.maximum(diff, 0.0)), 0.0)
    d_b = jnp.where(diff < 0, jnp.exp(lg_b[:, None, None] * jnp.maximum(-diff, 0.0)), 0.0)
    dmat = (d_f + d_b).reshape(2, 2 * CHUNK, CHUNK)

    def lanes(per_head):
        t = jnp.repeat(per_head[:, :, None], HEAD_DIM, axis=2)
        return t.reshape(2, 2, CHUNK, HEAD_DIM).transpose(0, 2, 1, 3).reshape(2, CHUNK, 2 * HEAD_DIM)

    xi_f = lanes(jnp.exp(lg_f[:, None] * (pos + 1.0)))
    zeta_f = lanes(jnp.exp(lg_f[:, None] * (CHUNK - 1.0 - pos)))
    xi_b = lanes(jnp.exp(lg_b[:, None] * (CHUNK - pos)))
    zeta_b = lanes(jnp.exp(lg_b[:, None] * pos))
    blk = (jnp.arange(128)[:, None] // HEAD_DIM) == (jnp.arange(128)[None, :] // HEAD_DIM)

    def cdec(lg):
        g = jnp.exp(lg * CHUNK).reshape(2, 2)
        rows = jnp.repeat(g, HEAD_DIM, axis=1)
        return jnp.where(blk[None], rows[:, :, None], 0.0)

    lane = jnp.arange(BLK) // HEAD_DIM
    avg = jnp.where(lane[:, None] == lane[None, :], 1.0 / HEAD_DIM, 0.0).astype(BF16)
    return dmat, xi_f, zeta_f, xi_b, zeta_b, cdec(lg_f), cdec(lg_b), avg


def _ret_kernel(q_ref, k_ref, v_ref, g_ref, dmat_ref, xif_ref, zf_ref,
                xib_ref, zb_ref, cdf_ref, cdb_ref, avg_ref, o_ref, sf_ref, sb_ref, pb_ref, *, nsteps, g):
    t = pl.program_id(1)

    @pl.when(t == 0)
    def _():
        sf_ref[...] = jnp.zeros_like(sf_ref)
        sb_ref[...] = jnp.zeros_like(sb_ref)

    def state_update(s_ref, p, k, vb, zeta, cdec):
        kz = (k * zeta).T.astype(BF16)
        blockmask = jnp.where(cdec > 0.0, 1.0, 0.0)
        s_ref[p] = s_ref[p] * cdec + _dot(kz, vb) * blockmask

    @pl.when(t < nsteps)
    def _():
        base = (nsteps - 1 - t) * g
        for ci in reversed(range(g)):
            rows = slice(ci * CHUNK, (ci + 1) * CHUNK)
            k = k_ref[rows, :]
            vb = v_ref[rows, :].astype(BF16)
            for p in range(2):
                sl = slice(p * 128, (p + 1) * 128)
                pb_ref[base + ci, p] = sb_ref[p].astype(BF16)
                state_update(sb_ref, p, k[:, sl], vb[:, sl], zb_ref[p], cdb_ref[p])

    @pl.when(t >= nsteps)
    def _():
        base = (t - nsteps) * g
        head = _lane_head((CHUNK, 128), HEAD_DIM)
        for ci in range(g):
            rows = slice(ci * CHUNK, (ci + 1) * CHUNK)
            k_all = k_ref[rows, :]
            q_all = q_ref[rows, :]
            vb_all = v_ref[rows, :].astype(BF16)
            for p in range(2):
                sl = slice(p * 128, (p + 1) * 128)
                q, k, vb = q_all[:, sl], k_all[:, sl], vb_all[:, sl]
                kb = k.astype(BF16)
                q2 = jnp.concatenate([jnp.where(head == hh, q, 0.0) for hh in range(2)], axis=0).astype(BF16)
                pm = (_dot_nt(q2, kb) * dmat_ref[p]).astype(BF16)
                o2 = _dot(pm, vb)
                inner = jnp.where(head == 0, o2[:CHUNK], o2[CHUNK:])
                cross_f = _dot((q * xif_ref[p]).astype(BF16), sf_ref[p].astype(BF16))
                cross_b = _dot((q * xib_ref[p]).astype(BF16), pb_ref[base + ci, p])
                o_ref[rows, sl] = inner + cross_f + cross_b
                state_update(sf_ref, p, k, vb, zf_ref[p], cdf_ref[p])

        avg = avg_ref[...]

        def group_mean(x):
            hi, lo = _split2(x)
            return _dot(hi, avg) + _dot(lo, avg)

        o = o_ref[...]
        d = o - group_mean(o)
        var = group_mean(d * d)
        o_ref[...] = _silu(g_ref[...]) * (d * lax.rsqrt(var + GN_EPS))


def _retention(proj, tables):
    b, s, _ = proj.shape
    nc = s // CHUNK
    g = math.gcd(nc, 4)
    nsteps = nc // g
    rows = g * CHUNK
    both = lambda t: jnp.where(t < nsteps, nsteps - 1 - t, t - nsteps)
    late = lambda t: jnp.where(t < nsteps, 0, t - nsteps)
    col = lambda base, idx: pl.BlockSpec((None, rows, BLK), lambda bi, t: (bi, idx(t), base))
    pair = pl.BlockSpec((2, CHUNK, 128), lambda bi, t: (0, 0, 0))
    return pl.pallas_call(
        functools.partial(_ret_kernel, nsteps=nsteps, g=g),
        grid=(b, 2 * nsteps),
        in_specs=[
            col(COL_RQ, late), col(COL_RK, both), col(COL_RV, both), col(COL_RG, late),
            pl.BlockSpec((2, 2 * CHUNK, CHUNK), lambda bi, t: (0, 0, 0)),
            pair, pair, pair, pair, pair, pair,
            pl.BlockSpec((BLK, BLK), lambda bi, t: (0, 0)),
        ],
        out_specs=pl.BlockSpec((None, rows, BLK), lambda bi, t: (bi, late(t), 0)),
        out_shape=jax.ShapeDtypeStruct((b, s, BLK), F32),
        scratch_shapes=[pltpu.VMEM((2, 128, 128), F32), pltpu.VMEM((2, 128, 128), F32),
                        pltpu.VMEM((nc, 2, 128, 128), BF16)],
        compiler_params=_cparams(("parallel", "arbitrary")),
    )(proj, proj, proj, proj, *tables)


def _mla_prep_kernel(cq_ref, kv_ref, qnw_ref, kvnw_ref, wq_ref, wk_ref, wv_ref, place_ref,
                     qc_ref, qs1_ref, qs2_ref, kc_ref, ks1_ref, ks2_ref, q_ref, k_ref, v_ref):
    cqn = _rms(cq_ref[...], qnw_ref[...])
    q = _dot(cqn.astype(BF16), wq_ref[...])
    per_head = lambda r: jnp.concatenate([r[...]] * N_HEADS, axis=1)
    q = _rope(q, per_head(qc_ref), per_head(qs1_ref), per_head(qs2_ref), MLA_ROPE // 2)
    q_ref[...] = (q * ((MLA_NOPE + MLA_ROPE) ** -0.5 * math.log2(math.e))).astype(BF16)
    blk = kv_ref[...]
    ckvn = _rms(blk[:, :MLA_KV_RANK], kvnw_ref[...]).astype(BF16)
    kr = _rope(blk[:, MLA_KV_RANK:], kc_ref[...], ks1_ref[...], ks2_ref[...], MLA_ROPE // 2)
    k_ref[...] = (_dot(ckvn, wk_ref[...]) + _dot(kr.astype(BF16), place_ref[...])).astype(BF16)
    v = _dot(ckvn, wv_ref[...])
    ones = (lax.broadcasted_iota(jnp.int32, v.shape, 1) % 128) >= HEAD_DIM
    v_ref[...] = jnp.where(ones, 1.0, v).astype(BF16)


def _mla_weights(w_uq, w_ukv):
    qh = w_uq.reshape(MLA_Q_RANK, N_HEADS, MLA_NOPE + MLA_ROPE)
    wq = jnp.concatenate([qh, jnp.zeros((MLA_Q_RANK, N_HEADS, 32), F32)], axis=-1).reshape(MLA_Q_RANK, 512)
    kvh = w_ukv.reshape(MLA_KV_RANK, N_HEADS, MLA_NOPE + HEAD_DIM)
    zk = jnp.zeros((MLA_KV_RANK, N_HEADS, 64), F32)
    wk = jnp.concatenate([kvh[..., :MLA_NOPE], zk], axis=-1).reshape(MLA_KV_RANK, 512)
    wv = jnp.concatenate([kvh[..., MLA_NOPE:], zk], axis=-1).reshape(MLA_KV_RANK, 512)
    src = jnp.arange(128)[:, None]
    dst = jnp.arange(512)[None, :]
    place = ((dst % 128 == src + MLA_NOPE) & (src < MLA_ROPE)).astype(BF16)
    return wq.astype(BF16), wk.astype(BF16), wv.astype(BF16), place


def _mla_flash_kernel(q_ref, k_ref, v_ref, o_ref, m_ref, acc_ref, *, strip):
    ki = pl.program_id(2)

    @pl.when(ki == 0)
    def _():
        m_ref[...] = jnp.full_like(m_ref, -jnp.inf)
        acc_ref[...] = jnp.zeros_like(acc_ref)

    for h in range(N_HEADS):
        sl = slice(h * 128, (h + 1) * 128)
        k = k_ref[:, sl]
        v = v_ref[:, sl]
        for r in range(q_ref.shape[0] // strip):
            rows = slice(r * strip, (r + 1) * strip)
            s = _dot_nt(q_ref[rows, sl], k)
            m_prev = m_ref[h, rows]
            m_new = jnp.maximum(m_prev, jnp.max(s, axis=-1, keepdims=True))
            p = jnp.exp2(s - m_new[:, :1])
            acc_ref[h, rows] = jnp.exp2(m_prev - m_new) * acc_ref[h, rows] + _dot(p.astype(BF16), v)
            m_ref[h, rows] = m_new

    @pl.when(ki == pl.num_programs(2) - 1)
    def _():
        outs = []
        for h in range(N_HEADS):
            a = acc_ref[h]
            outs.append(a[:, :HEAD_DIM] / a[:, HEAD_DIM:])
        o_ref[...] = jnp.concatenate(outs, axis=-1)


def _mla(proj, q_norm_w, kv_norm_w, w_uq, w_ukv, q_tabs, k_tabs):
    b, s, _ = proj.shape
    wq, wk, wv, place = _mla_weights(w_uq, w_ukv)
    ts = 512
    row = lambda n: pl.BlockSpec((ts, n), lambda bi, i: (i, 0))
    full = lambda r, n: pl.BlockSpec((r, n), lambda bi, i: (0, 0))
    out = pl.BlockSpec((None, ts, 512), lambda bi, i: (bi, i, 0))
    q, k, v = pl.pallas_call(
        _mla_prep_kernel,
        grid=(b, s // ts),
        in_specs=[
            pl.BlockSpec((None, ts, BLK), lambda bi, i: (bi, i, COL_MCQ)),
            pl.BlockSpec((None, ts, BLK), lambda bi, i: (bi, i, COL_MKV)),
            full(1, MLA_Q_RANK), full(1, MLA_KV_RANK),
            full(MLA_Q_RANK, 512), full(MLA_KV_RANK, 512), full(MLA_KV_RANK, 512), full(128, 512),
            row(128), row(128), row(128), row(128), row(128), row(128),
        ],
        out_specs=[out, out, out],
        out_shape=[jax.ShapeDtypeStruct((b, s, 512), BF16)] * 3,
        compiler_params=_cparams(("parallel", "parallel")),
    )(proj, proj, q_norm_w[None], kv_norm_w[None], wq, wk, wv, place, *q_tabs, *k_tabs)

    tq, tk = math.gcd(s, 2048), math.gcd(s, 512)
    return pl.pallas_call(
        functools.partial(_mla_flash_kernel, strip=tq),
        grid=(b, s // tq, s // tk),
        in_specs=[
            pl.BlockSpec((None, tq, 512), lambda bi, qi, ki: (bi, qi, 0)),
            pl.BlockSpec((None, tk, 512), lambda bi, qi, ki: (bi, ki, 0)),
            pl.BlockSpec((None, tk, 512), lambda bi, qi, ki: (bi, ki, 0)),
        ],
        out_specs=pl.BlockSpec((None, tq, N_HEADS * HEAD_DIM), lambda bi, qi, ki: (bi, qi, 0)),
        out_shape=jax.ShapeDtypeStruct((b, s, N_HEADS * HEAD_DIM), F32),
        scratch_shapes=[pltpu.VMEM((N_HEADS, tq, 128), F32), pltpu.VMEM((N_HEADS, tq, 128), F32)],
        compiler_params=_cparams(("parallel", "parallel", "arbitrary")),
    )(q, k, v)


def _conv_kernel(prev_ref, cur_ref, next_ref, w_ref, b_ref, o_ref, *, ts):
    i = pl.program_id(1)
    w = w_ref[...]

    def conv(x):
        n = x.shape[0]
        acc = x * w[2:3]
        for s in (-2, -1, 1, 2):
            acc = acc + pltpu.roll(x, (-s) % n, 0) * w[s + 2:s + 3]
        return acc

    act = lambda y: _silu(y + b_ref[...])
    cur = cur_ref[...]
    prev = prev_ref[...] * (i > 0).astype(F32)
    nxt = next_ref[...] * (i < pl.num_programs(1) - 1).astype(F32)
    o_ref[...] = act(conv(cur))
    top = conv(jnp.concatenate([prev, cur[:16]], axis=0))
    o_ref[0:8, :] = act(top[8:16])
    bot = conv(jnp.concatenate([cur[ts - 16:], nxt], axis=0))
    o_ref[ts - 8:ts, :] = act(bot[8:16])


def _ssd_conv(proj, conv_w, conv_b):
    b, s, _ = proj.shape
    ts = 512
    width = 3 * BLK
    cb = COL_XBC * BLK // width
    nb8 = ts // 8
    w8 = jnp.concatenate([conv_w, jnp.zeros((8 - SSM_CONV, width), F32)], axis=0)
    return pl.pallas_call(
        functools.partial(_conv_kernel, ts=ts),
        grid=(b, s // ts),
        in_specs=[
            pl.BlockSpec((None, 8, width), lambda bi, i: (bi, jnp.maximum(i * nb8 - 1, 0), cb)),
            pl.BlockSpec((None, ts, width), lambda bi, i: (bi, i, cb)),
            pl.BlockSpec((None, 8, width), lambda bi, i: (bi, jnp.minimum((i + 1) * nb8, s // 8 - 1), cb)),
            pl.BlockSpec((8, width), lambda bi, i: (0, 0)),
            pl.BlockSpec((1, width), lambda bi, i: (0, 0)),
        ],
        out_specs=pl.BlockSpec((None, ts, width), lambda bi, i: (bi, i, 0)),
        out_shape=jax.ShapeDtypeStruct((b, s, width), F32),
        compiler_params=_cparams(("parallel", "parallel")),
    )(proj, proj, proj, w8, conv_b[None])


def _ssd_tables():
    tril = _tri(CHUNK, lambda i, j: j <= i)
    triu = _tri(CHUNK, lambda i, j: j >= i)
    h = jnp.arange(128)[:, None]
    lane = jnp.arange(256)[None, :]
    e_f = ((lane // HEAD_DIM == h) & (h < N_HEADS)).astype(BF16)
    e_b = ((lane // HEAD_DIM == h - N_HEADS) & (h >= N_HEADS) & (h < 2 * N_HEADS)).astype(BF16)
    return tril, triu, e_f, e_b


def _ssd_kernel(xs_ref, bm_ref, cm_ref, z_ref, dt_ref, dtt_ref, bias_ref, a_ref, biasc_ref, ac_ref,
                dskip_ref, nw_ref, tril_ref, triu_ref, ef_ref, eb_ref, o_ref,
                sf_ref, sb_ref, pb_ref, *, nsteps, g):
    t = pl.program_id(1)
    tril, triu = tril_ref[...], triu_ref[...]

    @pl.when(t == 0)
    def _():
        sf_ref[...] = jnp.zeros_like(sf_ref)
        sb_ref[...] = jnp.zeros_like(sb_ref)

    chunks = [slice(ci * CHUNK, (ci + 1) * CHUNK) for ci in range(g)]

    def step_sizes():
        dt = _softplus(dt_ref[:, :128] + bias_ref[...])
        return dt, dt * a_ref[...]

    def cumulative(tri, dta, expand):
        return _dot_x_exact(jnp.concatenate([_dot_exact_x(tri, dta[r]) for r in chunks], axis=0), expand)

    def state_update(s_ref, bmf, weighted_x, total):
        for gi in range(SSM_GROUPS):
            sl = slice(gi * 128, (gi + 1) * 128)
            upd = _dot(bmf[:, sl].T.astype(BF16), weighted_x[:, sl].astype(BF16))
            s_ref[gi] = s_ref[gi] * jnp.exp(total[:, sl]) + upd

    @pl.when(t < nsteps)
    def _():
        base = (nsteps - 1 - t) * g
        dt, dta = step_sizes()
        rcs_all = cumulative(triu, dta, eb_ref[...])
        xdt_all = xs_ref[...] * _dot_x_exact(dt, eb_ref[...])
        for ci in reversed(range(g)):
            rows = chunks[ci]
            rcs_b = rcs_all[rows]
            tot_b = rcs_b[0:1, :]
            pb_ref[base + ci] = sb_ref[...].astype(BF16)
            state_update(sb_ref, bm_ref[rows, :], jnp.exp(tot_b - rcs_b) * xdt_all[rows], tot_b)

    @pl.when(t >= nsteps)
    def _():
        base = (t - nsteps) * g
        row = lax.broadcasted_iota(jnp.int32, (CHUNK, CHUNK), 0)
        colm = lax.broadcasted_iota(jnp.int32, (CHUNK, CHUNK), 1)
        head = _lane_head((CHUNK, 128), HEAD_DIM)
        efb = jnp.concatenate([ef_ref[...], eb_ref[...]], axis=1)
        dt, dta = step_sizes()
        rcs_all = cumulative(triu, dta, eb_ref[...])
        cs_all = cumulative(tril, dta, ef_ref[...])
        dt_exp = _dot_x_exact(dt, efb)
        dtat = jnp.concatenate([dtt_ref[:, r] for r in chunks], axis=0)
        per_chunk = lambda col: jnp.concatenate([col] * g, axis=0)
        dtat = _softplus(dtat + per_chunk(biasc_ref[...])) * per_chunk(ac_ref[...])
        cst_all = _dot_x_exact(dtat, triu)
        rcst_all = _dot_x_exact(dtat, tril)
        for ci in range(g):
            rows = chunks[ci]
            xs = xs_ref[rows, :]
            bmf = bm_ref[rows, :]
            bm = bmf.astype(BF16)
            cm = cm_ref[rows, :].astype(BF16)
            rcs_b = rcs_all[rows]
            cs_f = cs_all[rows]
            tot_f = cs_f[CHUNK - 1:CHUNK, :]
            xdt_f = xs * dt_exp[rows, :SSM_INNER]
            xdt_b = xs * dt_exp[rows, SSM_INNER:]
            cst = cst_all[8 * ci:8 * ci + 8]
            rcst = rcst_all[8 * ci:8 * ci + 8]
            pb = pb_ref[base + ci]
            ys = []
            for gi in range(SSM_GROUPS):
                sl = slice(gi * 128, (gi + 1) * 128)
                cb = _dot_nt(cm[:, sl], bm[:, sl])
                xcat = jnp.concatenate([xdt_f[:, sl], xdt_b[:, sl]], axis=0).astype(BF16)
                wcats = []
                for hh in range(2):
                    h = 2 * gi + hh
                    c0 = h * HEAD_DIM
                    seg_f = cs_f[:, c0:c0 + 1] - cst[h:h + 1, :]
                    seg_b = rcs_b[:, c0:c0 + 1] - rcst[N_HEADS + h:N_HEADS + h + 1, :]
                    dec_f = jnp.where(row >= colm, jnp.exp(jnp.minimum(seg_f, 0.0)), 0.0)
                    dec_b = jnp.where(row < colm, jnp.exp(jnp.minimum(seg_b, 0.0)), 0.0)
                    wcats.append(jnp.concatenate([cb * dec_f, cb * dec_b], axis=1))
                y2 = _dot(jnp.concatenate(wcats, axis=0).astype(BF16), xcat)
                yg = jnp.where(head == 0, y2[:CHUNK], y2[CHUNK:])
                states = jnp.concatenate([sf_ref[gi].astype(BF16), pb[gi]], axis=1)
                off = _dot(cm[:, sl], states)
                ys.append(yg + jnp.exp(cs_f[:, sl]) * off[:, :128] + jnp.exp(rcs_b[:, sl]) * off[:, 128:])
            y = jnp.concatenate(ys, axis=1) + dskip_ref[...] * xs
            y = y * _silu(z_ref[rows, :])
            o_ref[rows, :] = _rms(y, nw_ref[...])
            state_update(sf_ref, bmf, jnp.exp(tot_f - cs_f) * xdt_f, tot_f)


def _ssd(proj, xbc, a_log, dt_bias, d_skip, norm_w, tables):
    b, s, _ = proj.shape
    nc = s // CHUNK
    g = math.gcd(nc, 4)
    nsteps = nc // g
    rows = g * CHUNK
    dt_t = jnp.swapaxes(proj[:, :, COL_DT * BLK:COL_DT * BLK + 8], 1, 2)
    pad_row = lambda v: jnp.concatenate([v.reshape(1, 8), jnp.zeros((1, 120), F32)], axis=1)
    a = -jnp.exp(a_log.astype(F32))
    both = lambda t: jnp.where(t < nsteps, nsteps - 1 - t, t - nsteps)
    late = lambda t: jnp.where(t < nsteps, 0, t - nsteps)
    full = lambda r, n: pl.BlockSpec((r, n), lambda bi, t: (0, 0))
    blk = lambda idx, c: pl.BlockSpec((None, rows, BLK), lambda bi, t: (bi, idx(t), c))
    return pl.pallas_call(
        functools.partial(_ssd_kernel, nsteps=nsteps, g=g),
        grid=(b, 2 * nsteps),
        in_specs=[
            blk(both, 0), blk(both, 1), blk(late, 2), blk(late, COL_Z), blk(both, COL_DT),
            pl.BlockSpec((None, 8, rows), lambda bi, t: (bi, 0, late(t))),
            full(1, 128), full(1, 128), full(8, 1), full(8, 1),
            full(1, SSM_INNER), full(1, SSM_INNER),
            full(CHUNK, CHUNK), full(CHUNK, CHUNK), full(128, 256), full(128, 256),
        ],
        out_specs=blk(late, 0),
        out_shape=jax.ShapeDtypeStruct((b, s, SSM_INNER), F32),
        scratch_shapes=[pltpu.VMEM((SSM_GROUPS, 128, 128), F32), pltpu.VMEM((SSM_GROUPS, 128, 128), F32),
                        pltpu.VMEM((nc, SSM_GROUPS, 128, 128), BF16)],
        compiler_params=_cparams(("parallel", "arbitrary")),
    )(xbc, xbc, xbc, proj, proj, dt_t, pad_row(dt_bias), pad_row(a), dt_bias.reshape(8, 1), a.reshape(8, 1),
      jnp.repeat(d_skip, HEAD_DIM)[None], norm_w[None], *tables)


def _dil_kernel(q_ref, k_ref, v_ref, o_ref, m_ref, l_ref, a_ref, *, s):
    head = _lane_head((128, 128), HEAD_DIM)
    kw = 2 * 128
    for pi, (win, d) in enumerate(DIL_PATTERNS):
        half = win // (2 * d)
        seg = s // d
        per_seg = seg // 128

        def body(i, carry, d=d, pi=pi, half=half, seg=seg, per_seg=per_seg):
            r = i // per_seg
            m0 = (i % per_seg) * 128
            ks = jnp.clip(m0 - half, 0, seg - kw)
            if d == 1:
                qrows = pl.ds(pl.multiple_of(m0, 128), 128)
                krows = pl.ds(pl.multiple_of(ks, 64), kw)
            else:
                qrows = pl.ds(r + d * m0, 128, stride=d)
                krows = pl.ds(r + d * ks, kw, stride=d)
            q = q_ref[qrows, :]
            kb = k_ref[krows, :].astype(BF16)
            vb = v_ref[krows, :].astype(BF16)
            qpos = m0 + lax.broadcasted_iota(jnp.int32, (256, kw), 0) % 128
            kpos = ks + lax.broadcasted_iota(jnp.int32, (256, kw), 1)
            valid = jnp.abs(qpos - kpos) <= half
            q2 = jnp.concatenate([jnp.where(head == hh, q, 0.0) for hh in range(2)], axis=0).astype(BF16)
            sc = jnp.where(valid, _dot_nt(q2, kb), -jnp.inf)
            m2 = jnp.max(sc, axis=-1, keepdims=True)
            p = jnp.exp(sc - m2)
            l2 = jnp.sum(p, axis=-1, keepdims=True)
            a2 = _dot(p.astype(BF16), vb)
            m_ref.at[pi][qrows, :] = jnp.where(head == 0, m2[:128], m2[128:])
            l_ref.at[pi][qrows, :] = jnp.where(head == 0, l2[:128], l2[128:])
            a_ref.at[pi][qrows, :] = jnp.where(head == 0, a2[:128], a2[128:])
            return carry

        lax.fori_loop(0, s // 128, body, 0, unroll=4)

    def merge(i, carry):
        rows = pl.ds(pl.multiple_of(i * 512, 512), 512)
        ms = [m_ref[pi, rows, :] for pi in range(len(DIL_PATTERNS))]
        top = functools.reduce(jnp.maximum, ms)
        ws = [jnp.exp(m - top) for m in ms]
        den = sum(w * l_ref[pi, rows, :] for pi, w in enumerate(ws))
        num = sum(w * a_ref[pi, rows, :] for pi, w in enumerate(ws))
        o_ref[rows, :] = num / den
        return carry

    lax.fori_loop(0, s // 512, merge, 0)


def _dilated(proj):
    b, s, _ = proj.shape
    assert all(s // d >= 256 for _, d in DIL_PATTERNS)
    seqb = lambda c: pl.BlockSpec((None, s, 128), lambda bi, p: (bi, 0, 2 * c + p))
    return pl.pallas_call(
        functools.partial(_dil_kernel, s=s),
        grid=(b, 2),
        in_specs=[seqb(COL_DQ), seqb(COL_DK), seqb(COL_DV)],
        out_specs=seqb(0),
        out_shape=jax.ShapeDtypeStruct((b, s, BLK), F32),
        scratch_shapes=[pltpu.VMEM((len(DIL_PATTERNS), s, 128), F32)] * 3,
        compiler_params=_cparams(("parallel", "parallel")),
    )(proj, proj, proj)


def _outproj_kernel(*refs, n_res):
    ya_ref, yb_ref, yc_ref, yd_ref, w_ref, lnw_ref, rw_ref, xo_ref, h_ref, aff_ref = refs[n_res:]
    acc = refs[0][...]
    for r in refs[1:n_res]:
        acc = acc + r[...]
    mixed = jnp.concatenate([r[...].astype(BF16) for r in (ya_ref, yb_ref, yc_ref, yd_ref)], axis=1)
    acc = acc + _dot(mixed, w_ref[...])
    xo_ref[...] = acc
    h = _rms(acc, lnw_ref[...])
    h_hi = h.astype(BF16)
    h_ref[...] = h_hi
    h_lo = (h - h_hi.astype(F32)).astype(BF16)
    rw = rw_ref[...]
    w_hi = rw.astype(BF16)
    w_lo = (rw - w_hi.astype(F32)).astype(BF16)
    tm = h.shape[0]
    r = _dot(jnp.concatenate([h_hi, h_lo], axis=0), jnp.concatenate([w_hi, w_lo], axis=1))
    logits = r[:tm, :128] + r[:tm, 128:] + r[tm:, :128] + r[tm:, 128:]
    lane = lax.broadcasted_iota(jnp.int32, logits.shape, 1)
    logits = jnp.where(lane < N_EXPERTS, logits, -jnp.inf)
    e = jnp.exp(logits - jnp.max(logits, axis=-1, keepdims=True))
    aff = e / jnp.sum(e, axis=-1, keepdims=True)
    aff_ref[...] = aff.T[:N_EXPERTS, :]


def _outproj(res, ys, w_out, ln2_w, router_w):
    t = res[0].shape[0]
    tm = 256
    rw = jnp.concatenate([router_w, jnp.zeros((D_MODEL, 128 - N_EXPERTS), F32)], axis=1)
    row = lambda n: pl.BlockSpec((tm, n), lambda i: (i, 0))
    full = lambda r, n: pl.BlockSpec((r, n), lambda i: (0, 0))
    return pl.pallas_call(
        functools.partial(_outproj_kernel, n_res=len(res)),
        grid=(t // tm,),
        in_specs=[row(D_MODEL)] * len(res) + [row(BLK)] * 4 + [
            full(D_MODEL, D_MODEL), full(1, D_MODEL), full(D_MODEL, 128)],
        out_specs=[row(D_MODEL), row(D_MODEL), pl.BlockSpec((N_EXPERTS, tm), lambda i: (0, i))],
        out_shape=[jax.ShapeDtypeStruct((t, D_MODEL), F32),
                   jax.ShapeDtypeStruct((t, D_MODEL), BF16),
                   jax.ShapeDtypeStruct((N_EXPERTS, t), F32)],
        compiler_params=_cparams(("parallel",)),
    )(*res, *ys, w_out.astype(BF16), ln2_w[None], rw)


def _exclusive_prefix(x, tri):
    n = x.shape[1] // 128
    off = jnp.zeros((x.shape[0], 1), F32)
    parts = []
    for i in range(n):
        xt = x[:, i * 128:(i + 1) * 128]
        incl = _dot(xt.astype(BF16), tri)
        parts.append(incl - xt + off)
        off = off + incl[:, 127:128]
    return jnp.concatenate(parts, axis=1)


def _select_kernel(aff_ref, tri_ref, key_ref, *, cap):
    a = aff_ref[...]
    n_exp = a.shape[0]
    thr_bits = jnp.zeros((n_exp, 1), jnp.int32)
    for bit in range(30, -1, -1):
        cand = thr_bits | (1 << bit)
        cnt = jnp.sum(jnp.where(a >= lax.bitcast_convert_type(cand, F32), 1.0, 0.0), axis=1, keepdims=True)
        thr_bits = jnp.where(cnt >= cap, cand, thr_bits)
    thr = lax.bitcast_convert_type(thr_bits, F32)
    above = jnp.where(a > thr, 1.0, 0.0)
    tie = jnp.where(a == thr, 1.0, 0.0)
    need = cap - jnp.sum(above, axis=1, keepdims=True)
    tri = tri_ref[...]
    sel = above + tie * jnp.where(_exclusive_prefix(tie, tri) < need, 1.0, 0.0)
    key_ref[...] = jnp.where(sel > 0.0, _exclusive_prefix(sel, tri), -1.0)


def _slot_kernel(key_ref, aff_ref, ltri_ref, idx_ref, gate_ref, *, cap, nt):
    n_exp = key_ref.shape[0]
    pad = jnp.zeros((128 - nt, 128), F32)
    slot = lax.broadcasted_iota(jnp.int32, (128, cap), 1).astype(F32)
    sub = lax.broadcasted_iota(jnp.int32, (128, cap), 0).astype(F32)
    ltri = ltri_ref[...]

    def per_expert(e, carry):
        key = jnp.concatenate([key_ref[e], pad - 1.0], axis=0)
        aff = jnp.concatenate([aff_ref[e], pad], axis=0)
        cnt = jnp.sum(jnp.where(key >= 0.0, 1.0, 0.0), axis=1, keepdims=True)
        cum = _dot(ltri, jnp.broadcast_to(cnt, (128, 128)).astype(BF16))
        in_tile = (jnp.where(slot >= (cum - cnt)[:, :1], 1.0, 0.0)
                   * jnp.where(slot < cum[:, :1], 1.0, 0.0))
        g = in_tile.astype(BF16)
        k_hi, k_lo = _split2(key.T)
        slot_of = _dot(k_hi, g) + _dot(k_lo, g)
        hit = jnp.where(slot_of == slot, 1.0, 0.0)
        aff_t = aff.T
        a_hi = aff_t.astype(BF16)
        r1 = aff_t - a_hi.astype(F32)
        a_mid = r1.astype(BF16)
        a_lo = (r1 - a_mid.astype(F32)).astype(BF16)
        aff_of = _dot(a_hi, g) + _dot(a_mid, g) + _dot(a_lo, g)
        tile = jnp.sum(in_tile * sub, axis=0, keepdims=True)
        within = jnp.sum(hit * sub, axis=0, keepdims=True)
        idx_ref[pl.ds(e, 1), :] = (128.0 * tile + within).astype(jnp.int32)
        gate_ref[pl.ds(e, 1), :] = jnp.sum(hit * aff_of, axis=0, keepdims=True)
        return carry

    lax.fori_loop(0, n_exp, per_expert, 0, unroll=4)


def _moe_select(aff_t, b, s, cap):
    nt = s // 128
    assert nt <= 128 and nt % 8 == 0 and cap % 128 == 0
    key = pl.pallas_call(
        functools.partial(_select_kernel, cap=cap),
        grid=(b,),
        in_specs=[pl.BlockSpec((N_EXPERTS, s), lambda bi: (0, bi)),
                  pl.BlockSpec((128, 128), lambda bi: (0, 0))],
        out_specs=pl.BlockSpec((None, N_EXPERTS, s), lambda bi: (bi, 0, 0)),
        out_shape=jax.ShapeDtypeStruct((b, N_EXPERTS, s), F32),
        compiler_params=_cparams(("parallel",)),
    )(aff_t, _tri(128, lambda i, j: i <= j))
    out = pl.BlockSpec((None, N_EXPERTS, cap), lambda bi: (bi, 0, 0))
    return pl.pallas_call(
        functools.partial(_slot_kernel, cap=cap, nt=nt),
        grid=(b,),
        in_specs=[pl.BlockSpec((None, N_EXPERTS, nt, 128), lambda bi: (bi, 0, 0, 0)),
                  pl.BlockSpec((N_EXPERTS, nt, 128), lambda bi: (0, bi, 0)),
                  pl.BlockSpec((128, 128), lambda bi: (0, 0))],
        out_specs=[out, out],
        out_shape=[jax.ShapeDtypeStruct((b, N_EXPERTS, cap), jnp.int32),
                   jax.ShapeDtypeStruct((b, N_EXPERTS, cap), F32)],
        compiler_params=_cparams(("parallel",)),
    )(key.reshape(b, N_EXPERTS, nt, 128), aff_t.reshape(N_EXPERTS, b * nt, 128), _tri(128, lambda i, j: j <= i))


ROWS_PER_ITER = 8


def _gather_kernel(idx_ref, h_ref, o_ref, hf_ref, buf_ref, *, cap, n_exp):
    e = pl.program_id(1)
    base = (pl.program_id(0) * n_exp + e) * cap

    @pl.when(e == 0)
    def _():
        hf_ref[...] = h_ref[...].astype(F32)

    def body(jb, carry):
        j0 = pl.multiple_of(jb * ROWS_PER_ITER, ROWS_PER_ITER)
        vals = [hf_ref[pl.ds(idx_ref[base + j0 + u], 1), :] for u in range(ROWS_PER_ITER)]
        for u in range(ROWS_PER_ITER):
            buf_ref[pl.ds(j0 + u, 1), :] = vals[u]
        return carry

    lax.fori_loop(0, cap // ROWS_PER_ITER, body, 0)
    o_ref[...] = buf_ref[...].astype(BF16)


def _moe_gather(idx_flat, h, cap):
    b, s, d = h.shape
    return pl.pallas_call(
        functools.partial(_gather_kernel, cap=cap, n_exp=N_EXPERTS),
        grid_spec=pltpu.PrefetchScalarGridSpec(
            num_scalar_prefetch=1,
            grid=(b, N_EXPERTS),
            in_specs=[pl.BlockSpec((None, s, d), lambda bi, e, idx: (bi, 0, 0))],
            out_specs=pl.BlockSpec((None, None, cap, d), lambda bi, e, idx: (e, bi, 0, 0)),
            scratch_shapes=[pltpu.VMEM((s, d), F32), pltpu.VMEM((cap, d), F32)],
        ),
        out_shape=jax.ShapeDtypeStruct((N_EXPERTS, b, cap, d), BF16),
        compiler_params=_cparams(("parallel", "arbitrary")),
    )(idx_flat, h)


def _ffn_kernel(x_ref, wg_ref, wu_ref, wd_ref, o_ref, *, rows, tr):
    f = pl.program_id(1)
    wg = wg_ref[...].astype(BF16)
    wu = wu_ref[...].astype(BF16)
    wd = wd_ref[...].astype(BF16)

    def partial_out(sl):
        x = x_ref[sl, :]
        hid = (_silu(_dot(x, wg)) * _dot(x, wu)).astype(BF16)
        return _dot(hid, wd)

    for r in range(rows // tr):
        sl = slice(r * tr, (r + 1) * tr)

        @pl.when(f == 0)
        def _():
            o_ref[sl, :] = partial_out(sl)

        @pl.when(f > 0)
        def _():
            o_ref[sl, :] += partial_out(sl)


def _moe_ffn(xin, w_gate, w_up, w_down, layer):
    n_exp, rows, d = xin.shape
    tf = 512
    return pl.pallas_call(
        functools.partial(_ffn_kernel, rows=rows, tr=math.gcd(rows, 1024)),
        grid=(n_exp, EXPERT_FF // tf),
        in_specs=[
            pl.BlockSpec((None, rows, d), lambda e, f: (e, 0, 0)),
            pl.BlockSpec((None, None, d, tf), lambda e, f: (layer, e, 0, f)),
            pl.BlockSpec((None, None, d, tf), lambda e, f: (layer, e, 0, f)),
            pl.BlockSpec((None, None, tf, d), lambda e, f: (layer, e, f, 0)),
        ],
        out_specs=pl.BlockSpec((None, rows, d), lambda e, f: (e, 0, 0)),
        out_shape=jax.ShapeDtypeStruct((n_exp, rows, d), F32),
        compiler_params=_cparams(("parallel", "arbitrary")),
    )(xin, w_gate, w_up, w_down)


def _combine_kernel(idx_ref, gate_ref, y_ref, o_ref, *, cap, n_exp):
    e = pl.program_id(1)
    base = (pl.program_id(0) * n_exp + e) * cap

    @pl.when(e == 0)
    def _():
        o_ref[...] = jnp.zeros_like(o_ref)

    def body(jb, carry):
        j0 = pl.multiple_of(jb * ROWS_PER_ITER, ROWS_PER_ITER)
        rows = [idx_ref[base + j0 + u] for u in range(ROWS_PER_ITER)]
        vals = [o_ref[pl.ds(rows[u], 1), :] + gate_ref[base + j0 + u] * y_ref[pl.ds(j0 + u, 1), :]
                for u in range(ROWS_PER_ITER)]
        for u in range(ROWS_PER_ITER):
            o_ref[pl.ds(rows[u], 1), :] = vals[u]
        return carry

    lax.fori_loop(0, cap // ROWS_PER_ITER, body, 0)


def _moe_combine(idx_flat, gate_flat, y, b, s, cap):
    d = y.shape[-1]
    return pl.pallas_call(
        functools.partial(_combine_kernel, cap=cap, n_exp=N_EXPERTS),
        grid_spec=pltpu.PrefetchScalarGridSpec(
            num_scalar_prefetch=2,
            grid=(b, N_EXPERTS),
            in_specs=[pl.BlockSpec((None, None, cap, d), lambda bi, e, idx, gate: (e, bi, 0, 0))],
            out_specs=pl.BlockSpec((None, s, d), lambda bi, e, idx, gate: (bi, 0, 0)),
        ),
        out_shape=jax.ShapeDtypeStruct((b, s, d), F32),
        compiler_params=_cparams(("parallel", "arbitrary")),
    )(idx_flat, gate_flat, y)


def _final_norm_kernel(xa_ref, xb_ref, w_ref, o_ref):
    o_ref[...] = _rms(xa_ref[...] + xb_ref[...], w_ref[...])


def _final_norm(xa, xb, w):
    t = xa.shape[0]
    tm = 1024
    row = pl.BlockSpec((tm, D_MODEL), lambda i: (i, 0))
    return pl.pallas_call(
        _final_norm_kernel,
        grid=(t // tm,),
        in_specs=[row, row, pl.BlockSpec((1, D_MODEL), lambda i: (0, 0))],
        out_specs=row,
        out_shape=jax.ShapeDtypeStruct((t, D_MODEL), F32),
        compiler_params=_cparams(("parallel",)),
    )(xa, xb, w[None])


def _moe(h, aff, w_gate, w_up, w_down, layer, b, s):
    d = h.shape[-1]
    cap = EC_CAPACITY_FACTOR * s // N_EXPERTS
    token_idx, gate = _moe_select(aff, b, s, cap)
    idx_flat = token_idx.reshape(-1)
    xin = _moe_gather(idx_flat, h.reshape(b, s, d), cap)
    y = _moe_ffn(xin.reshape(N_EXPERTS, b * cap, d), w_gate, w_up, w_down, layer)
    return _moe_combine(idx_flat, gate.reshape(-1), y.reshape(N_EXPERTS, b, cap, d), b, s, cap)


def kernel(x, ln1_w, w_in, mla_q_norm_w, mla_kv_norm_w, mla_w_uq, mla_w_ukv, ssm_conv_w, ssm_conv_b,
           ssm_a_log, ssm_dt_bias, ssm_d, ssm_norm_w, w_out, ln2_w, router_w, exp_w_gate, exp_w_up,
           exp_w_down, final_norm_w):
    b, s, d = x.shape
    depth = w_in.shape[0]
    ret_tabs = _ret_tables()
    ret_rope = _rope_tables(s, HEAD_DIM, RET_THETA, HEAD_DIM, 0, 2)
    mla_q_rope = _rope_tables(s, MLA_ROPE, ROPE_THETA, 128, MLA_NOPE, 1)
    mla_k_rope = _rope_tables(s, MLA_ROPE, ROPE_THETA, 128, 0, 1)
    dil_rope = _rope_tables(s, ROPE_DIM, ROPE_THETA, HEAD_DIM, 0, 2)
    ssd_tabs = _ssd_tables()
    res = [x.reshape(b * s, d)]
    for i in range(depth):
        proj = _inproj(res, ln1_w[i][None], _pad_cols(w_in[i]).astype(BF16), ret_rope, dil_rope, s)
        proj = proj.reshape(b, s, PROJ_PAD)
        y_a = _retention(proj, ret_tabs)
        y_b = _mla(proj, mla_q_norm_w[i], mla_kv_norm_w[i], mla_w_uq[i], mla_w_ukv[i], mla_q_rope, mla_k_rope)
        xbc = _ssd_conv(proj, ssm_conv_w[i], ssm_conv_b[i])
        y_c = _ssd(proj, xbc, ssm_a_log[i], ssm_dt_bias[i], ssm_d[i], ssm_norm_w[i], ssd_tabs)
        y_d = _dilated(proj)
        ys = [y.reshape(b * s, BLK) for y in (y_a, y_b, y_c, y_d)]
        x_mid, h, aff = _outproj(res, ys, w_out[i], ln2_w[i], router_w[i])
        moe = _moe(h, aff, exp_w_gate, exp_w_up, exp_w_down, i, b, s)
        res = [x_mid, moe.reshape(b * s, d)]
    return _final_norm(res[0], res[1], final_norm_w).reshape(b, s, d)
```

```python
import functools
import math

import jax
import jax.numpy as jnp
from jax import lax
from jax.experimental import pallas as pl
from jax.experimental.pallas import tpu as pltpu

F32 = jnp.float32
BF16 = jnp.bfloat16

D_MODEL = 1024
RMS_EPS = 1e-6
GN_EPS = 1e-5
CHUNK = 128
HEAD_DIM = 64
N_HEADS = 4

RET_THETA = 10000.0
ROPE_THETA = 500000.0
ROPE_DIM = 16

MLA_Q_RANK = 256
MLA_KV_RANK = 128
MLA_NOPE = 64
MLA_ROPE = 32

SSM_GROUPS = 2
SSM_STATE = 128
SSM_CONV = 5
SSM_INNER = 256

DIL_PATTERNS = ((128, 1), (512, 4), (2048, 16))

N_EXPERTS = 16
EXPERT_FF = 2048
EC_CAPACITY_FACTOR = 2

BLK = 256
COL_RQ, COL_RK, COL_RV, COL_RG = 0, 1, 2, 3
COL_MCQ, COL_MKV = 4, 5
COL_XBC = 6
COL_Z = 9
COL_DT = 10
COL_DQ, COL_DK, COL_DV = 11, 12, 13
PROJ_PAD = 14 * BLK

VMEM_LIMIT = 56 * 1024 * 1024


def _cparams(sem):
    return pltpu.CompilerParams(dimension_semantics=sem, vmem_limit_bytes=VMEM_LIMIT)


def _split2(x):
    hi = x.astype(BF16)
    lo = (x - hi.astype(F32)).astype(BF16)
    return hi, lo


def _dot(a, b):
    return jnp.dot(a, b, preferred_element_type=F32)


def _dot_nt(a, b):
    return lax.dot_general(a, b, (((1,), (1,)), ((), ())), preferred_element_type=F32)


def _dot_x_exact(x, m):
    hi, lo = _split2(x)
    return _dot(jnp.concatenate([hi, lo], axis=1), jnp.concatenate([m, m], axis=0))


def _dot_exact_x(m, x):
    hi, lo = _split2(x)
    n = x.shape[1]
    r = _dot(m, jnp.concatenate([hi, lo], axis=1))
    return r[:, :n] + r[:, n:]


def _rope(x, c, s1, s2, half):
    w = x.shape[-1]
    return x * c + pltpu.roll(x, w - half, 1) * s1 + pltpu.roll(x, half, 1) * s2


def _silu(x):
    return x * (1.0 / (1.0 + jnp.exp(-x)))


def _softplus(x):
    return jnp.maximum(x, 0.0) + jnp.log(1.0 + jnp.exp(-jnp.abs(x)))


def _lane_head(shape, width):
    return lax.broadcasted_iota(jnp.int32, shape, 1) // width


def _rms(x, w):
    return x * lax.rsqrt(jnp.mean(x * x, axis=-1, keepdims=True) + RMS_EPS) * w


def _rope_tables(s, rot_dim, theta, head_dim, offset, n_heads):
    half = rot_dim // 2
    inv = 1.0 / (theta ** (jnp.arange(0, rot_dim, 2, dtype=F32) / rot_dim))
    ang = jnp.arange(s, dtype=F32)[:, None] * inv[None, :]
    cos, sin = jnp.cos(ang), jnp.sin(ang)
    pre0 = jnp.zeros((s, offset), F32)
    pre1 = jnp.ones((s, offset), F32)
    post0 = jnp.zeros((s, head_dim - offset - rot_dim), F32)
    post1 = jnp.ones((s, head_dim - offset - rot_dim), F32)
    zh = jnp.zeros((s, half), F32)
    c = jnp.concatenate([pre1, cos, cos, post1], axis=-1)
    s1 = jnp.concatenate([pre0, -sin, zh, post0], axis=-1)
    s2 = jnp.concatenate([pre0, zh, sin, post0], axis=-1)
    tile = lambda t: jnp.tile(t, (1, n_heads))
    return tile(c), tile(s1), tile(s2)


def _tri(n, fn):
    i = jnp.arange(n)
    return fn(i[:, None], i[None, :]).astype(BF16)


def _inproj_kernel(*refs, n_res, col_chunk):
    lnw_ref, w_ref, rc_ref, rs1_ref, rs2_ref, dc_ref, ds1_ref, ds2_ref, o_ref = refs[n_res:]
    x = refs[0][...]
    for r in refs[1:n_res]:
        x = x + r[...]
    yb = _rms(x, lnw_ref[...]).astype(BF16)
    for j in range(PROJ_PAD // col_chunk):
        sl = slice(j * col_chunk, (j + 1) * col_chunk)
        o_ref[:, sl] = _dot(yb, w_ref[:, sl])
    wide = lambda tabs: [jnp.concatenate([r[...]] * 2, axis=1) for r in tabs]
    ret = wide((rc_ref, rs1_ref, rs2_ref))
    dil = wide((dc_ref, ds1_ref, ds2_ref))
    for col, tabs, half, scale in ((COL_RQ, ret, HEAD_DIM // 2, HEAD_DIM ** -0.5), (COL_RK, ret, HEAD_DIM // 2, None),
                                   (COL_DQ, dil, ROPE_DIM // 2, HEAD_DIM ** -0.5), (COL_DK, dil, ROPE_DIM // 2, None)):
        sl = slice(col * BLK, (col + 1) * BLK)
        y = _rope(o_ref[:, sl], *tabs, half)
        o_ref[:, sl] = y if scale is None else y * scale


def _inproj(res, lnw, w_pad, ret_rope, dil_rope, s):
    t = res[0].shape[0]
    tm = math.gcd(s, 512)
    tab = pl.BlockSpec((tm, 128), lambda i: (i % (s // tm), 0))
    return pl.pallas_call(
        functools.partial(_inproj_kernel, n_res=len(res), col_chunk=512),
        grid=(t // tm,),
        in_specs=[pl.BlockSpec((tm, D_MODEL), lambda i: (i, 0))] * len(res) + [
            pl.BlockSpec((1, D_MODEL), lambda i: (0, 0)),
            pl.BlockSpec((D_MODEL, PROJ_PAD), lambda i: (0, 0)),
        ] + [tab] * 6,
        out_specs=pl.BlockSpec((tm, PROJ_PAD), lambda i: (i, 0)),
        out_shape=jax.ShapeDtypeStruct((t, PROJ_PAD), F32),
        compiler_params=_cparams(("parallel",)),
    )(*res, lnw, w_pad, *ret_rope, *dil_rope)


def _pad_cols(w):
    sizes = (256, 256, 256, 256, 256, 128, 32, 256, 768, 8, 256, 256, 256)
    pts, acc = [], 0
    for sz in sizes:
        pts.append((acc, acc + sz))
        acc += sz
    seg = lambda i: w[:, pts[i][0]:pts[i][1]]
    z = lambda n: jnp.zeros((w.shape[0], n), w.dtype)
    cols = [seg(0), seg(1), seg(2), seg(3), seg(4),
            seg(5), seg(6), z(BLK - 128 - 32),
            seg(8), seg(7), seg(9), z(BLK - 8),
            seg(10), seg(11), seg(12)]
    return jnp.concatenate(cols, axis=1)


def _ret_tables():
    pos = jnp.arange(CHUNK, dtype=F32)
    hh = jnp.arange(N_HEADS, dtype=F32)
    lg_f = jnp.log1p(-jnp.exp2(-5.0 - hh))
    lg_b = jnp.log1p(-jnp.exp2(-5.5 - hh))
    diff = pos[:, None] - pos[None, :]
    d_f = jnp.where(diff >= 0, jnp.exp(lg_f[:, None, None] * jnp.maximum(diff, 0.0)), 0.0)
    d_b = jnp.where(diff < 0, jnp.exp(lg_b[:, None, None] * jnp.maximum(-diff, 0.0)), 0.0)
    dmat = (d_f + d_b).reshape(2, 2 * CHUNK, CHUNK)

    def lanes(per_head):
        t = jnp.repeat(per_head[:, :, None], HEAD_DIM, axis=2)
        return t.reshape(2, 2, CHUNK, HEAD_DIM).transpose(0, 2, 1, 3).reshape(2, CHUNK, 2 * HEAD_DIM)

    xi_f = lanes(jnp.exp(lg_f[:, None] * (pos + 1.0)))
    zeta_f = lanes(jnp.exp(lg_f[:, None] * (CHUNK - 1.0 - pos)))
    xi_b = lanes(jnp.exp(lg_b[:, None] * (CHUNK - pos)))
    zeta_b = lanes(jnp.exp(lg_b[:, None] * pos))
    blk = (jnp.arange(128)[:, None] // HEAD_DIM) == (jnp.arange(128)[None, :] // HEAD_DIM)

    def cdec(lg):
        g = jnp.exp(lg * CHUNK).reshape(2, 2)
        rows = jnp.repeat(g, HEAD_DIM, axis=1)
        return jnp.where(blk[None], rows[:, :, None], 0.0)

    lane = jnp.arange(BLK) // HEAD_DIM
    avg = jnp.where(lane[:, None] == lane[None, :], 1.0 / HEAD_DIM, 0.0).astype(BF16)
    return dmat, xi_f, zeta_f, xi_b, zeta_b, cdec(lg_f), cdec(lg_b), avg


def _ret_kernel(q_ref, k_ref, v_ref, g_ref, dmat_ref, xif_ref, zf_ref,
                xib_ref, zb_ref, cdf_ref, cdb_ref, avg_ref, o_ref, sf_ref, sb_ref, pb_ref, *, nsteps, g):
    t = pl.program_id(1)

    @pl.when(t == 0)
    def _():
        sf_ref[...] = jnp.zeros_like(sf_ref)
        sb_ref[...] = jnp.zeros_like(sb_ref)

    def state_update(s_ref, p, k, vb, zeta, cdec):
        kz = (k * zeta).T.astype(BF16)
        blockmask = jnp.where(cdec > 0.0, 1.0, 0.0)
        s_ref[p] = s_ref[p] * cdec + _dot(kz, vb) * blockmask

    @pl.when(t < nsteps)
    def _():
        base = (nsteps - 1 - t) * g
        for ci in reversed(range(g)):
            rows = slice(ci * CHUNK, (ci + 1) * CHUNK)
            k = k_ref[rows, :]
            vb = v_ref[rows, :].astype(BF16)
            for p in range(2):
                sl = slice(p * 128, (p + 1) * 128)
                pb_ref[base + ci, p] = sb_ref[p].astype(BF16)
                state_update(sb_ref, p, k[:, sl], vb[:, sl], zb_ref[p], cdb_ref[p])

    @pl.when(t >= nsteps)
    def _():
        base = (t - nsteps) * g
        head = _lane_head((CHUNK, 128), HEAD_DIM)
        for ci in range(g):
            rows = slice(ci * CHUNK, (ci + 1) * CHUNK)
            k_all = k_ref[rows, :]
            q_all = q_ref[rows, :]
            vb_all = v_ref[rows, :].astype(BF16)
            for p in range(2):
                sl = slice(p * 128, (p + 1) * 128)
                q, k, vb = q_all[:, sl], k_all[:, sl], vb_all[:, sl]
                kb = k.astype(BF16)
                q2 = jnp.concatenate([jnp.where(head == hh, q, 0.0) for hh in range(2)], axis=0).astype(BF16)
                pm = (_dot_nt(q2, kb) * dmat_ref[p]).astype(BF16)
                o2 = _dot(pm, vb)
                inner = jnp.where(head == 0, o2[:CHUNK], o2[CHUNK:])
                cross_f = _dot((q * xif_ref[p]).astype(BF16), sf_ref[p].astype(BF16))
                cross_b = _dot((q * xib_ref[p]).astype(BF16), pb_ref[base + ci, p])
                o_ref[rows, sl] = inner + cross_f + cross_b
                state_update(sf_ref, p, k, vb, zf_ref[p], cdf_ref[p])

        avg = avg_ref[...]

        def group_mean(x):
            hi, lo = _split2(x)
            return _dot(hi, avg) + _dot(lo, avg)

        o = o_ref[...]
        d = o - group_mean(o)
        var = group_mean(d * d)
        o_ref[...] = _silu(g_ref[...]) * (d * lax.rsqrt(var + GN_EPS))


def _retention(proj, tables):
    b, s, _ = proj.shape
    nc = s // CHUNK
    g = math.gcd(nc, 4)
    nsteps = nc // g
    rows = g * CHUNK
    both = lambda t: jnp.where(t < nsteps, nsteps - 1 - t, t - nsteps)
    late = lambda t: jnp.where(t < nsteps, 0, t - nsteps)
    col = lambda base, idx: pl.BlockSpec((None, rows, BLK), lambda bi, t: (bi, idx(t), base))
    pair = pl.BlockSpec((2, CHUNK, 128), lambda bi, t: (0, 0, 0))
    return pl.pallas_call(
        functools.partial(_ret_kernel, nsteps=nsteps, g=g),
        grid=(b, 2 * nsteps),
        in_specs=[
            col(COL_RQ, late), col(COL_RK, both), col(COL_RV, both), col(COL_RG, late),
            pl.BlockSpec((2, 2 * CHUNK, CHUNK), lambda bi, t: (0, 0, 0)),
            pair, pair, pair, pair, pair, pair,
            pl.BlockSpec((BLK, BLK), lambda bi, t: (0, 0)),
        ],
        out_specs=pl.BlockSpec((None, rows, BLK), lambda bi, t: (bi, late(t), 0)),
        out_shape=jax.ShapeDtypeStruct((b, s, BLK), F32),
        scratch_shapes=[pltpu.VMEM((2, 128, 128), F32), pltpu.VMEM((2, 128, 128), F32),
                        pltpu.VMEM((nc, 2, 128, 128), BF16)],
        compiler_params=_cparams(("parallel", "arbitrary")),
    )(proj, proj, proj, proj, *tables)


def _mla_prep_kernel(cq_ref, kv_ref, qnw_ref, kvnw_ref, wq_ref, wk_ref, wv_ref, place_ref,
                     qc_ref, qs1_ref, qs2_ref, kc_ref, ks1_ref, ks2_ref, q_ref, k_ref, v_ref):
    cqn = _rms(cq_ref[...], qnw_ref[...])
    q = _dot(cqn.astype(BF16), wq_ref[...])
    per_head = lambda r: jnp.concatenate([r[...]] * N_HEADS, axis=1)
    q = _rope(q, per_head(qc_ref), per_head(qs1_ref), per_head(qs2_ref), MLA_ROPE // 2)
    q_ref[...] = (q * ((MLA_NOPE + MLA_ROPE) ** -0.5 * math.log2(math.e))).astype(BF16)
    blk = kv_ref[...]
    ckvn = _rms(blk[:, :MLA_KV_RANK], kvnw_ref[...]).astype(BF16)
    kr = _rope(blk[:, MLA_KV_RANK:], kc_ref[...], ks1_ref[...], ks2_ref[...], MLA_ROPE // 2)
    k_ref[...] = (_dot(ckvn, wk_ref[...]) + _dot(kr.astype(BF16), place_ref[...])).astype(BF16)
    v = _dot(ckvn, wv_ref[...])
    ones = (lax.broadcasted_iota(jnp.int32, v.shape, 1) % 128) >= HEAD_DIM
    v_ref[...] = jnp.where(ones, 1.0, v).astype(BF16)


def _mla_weights(w_uq, w_ukv):
    qh = w_uq.reshape(MLA_Q_RANK, N_HEADS, MLA_NOPE + MLA_ROPE)
    wq = jnp.concatenate([qh, jnp.zeros((MLA_Q_RANK, N_HEADS, 32), F32)], axis=-1).reshape(MLA_Q_RANK, 512)
    kvh = w_ukv.reshape(MLA_KV_RANK, N_HEADS, MLA_NOPE + HEAD_DIM)
    zk = jnp.zeros((MLA_KV_RANK, N_HEADS, 64), F32)
    wk = jnp.concatenate([kvh[..., :MLA_NOPE], zk], axis=-1).reshape(MLA_KV_RANK, 512)
    wv = jnp.concatenate([kvh[..., MLA_NOPE:], zk], axis=-1).reshape(MLA_KV_RANK, 512)
    src = jnp.arange(128)[:, None]
    dst = jnp.arange(512)[None, :]
    place = ((dst % 128 == src + MLA_NOPE) & (src < MLA_ROPE)).astype(BF16)
    return wq.astype(BF16), wk.astype(BF16), wv.astype(BF16), place


def _mla_flash_kernel(q_ref, k_ref, v_ref, o_ref, m_ref, acc_ref, *, strip):
    ki = pl.program_id(2)

    @pl.when(ki == 0)
    def _():
        m_ref[...] = jnp.full_like(m_ref, -jnp.inf)
        acc_ref[...] = jnp.zeros_like(acc_ref)

    for h in range(N_HEADS):
        sl = slice(h * 128, (h + 1) * 128)
        k = k_ref[:, sl]
        v = v_ref[:, sl]
        for r in range(q_ref.shape[0] // strip):
            rows = slice(r * strip, (r + 1) * strip)
            s = _dot_nt(q_ref[rows, sl], k)
            m_prev = m_ref[h, rows]
            m_new = jnp.maximum(m_prev, jnp.max(s, axis=-1, keepdims=True))
            p = jnp.exp2(s - m_new[:, :1])
            acc_ref[h, rows] = jnp.exp2(m_prev - m_new) * acc_ref[h, rows] + _dot(p.astype(BF16), v)
            m_ref[h, rows] = m_new

    @pl.when(ki == pl.num_programs(2) - 1)
    def _():
        outs = []
        for h in range(N_HEADS):
            a = acc_ref[h]
            outs.append(a[:, :HEAD_DIM] / a[:, HEAD_DIM:])
        o_ref[...] = jnp.concatenate(outs, axis=-1)


def _mla(proj, q_norm_w, kv_norm_w, w_uq, w_ukv, q_tabs, k_tabs):
    b, s, _ = proj.shape
    wq, wk, wv, place = _mla_weights(w_uq, w_ukv)
    ts = 512
    row = lambda n: pl.BlockSpec((ts, n), lambda bi, i: (i, 0))
    full = lambda r, n: pl.BlockSpec((r, n), lambda bi, i: (0, 0))
    out = pl.BlockSpec((None, ts, 512), lambda bi, i: (bi, i, 0))
    q, k, v = pl.pallas_call(
        _mla_prep_kernel,
        grid=(b, s // ts),
        in_specs=[
            pl.BlockSpec((None, ts, BLK), lambda bi, i: (bi, i, COL_MCQ)),
            pl.BlockSpec((None, ts, BLK), lambda bi, i: (bi, i, COL_MKV)),
            full(1, MLA_Q_RANK), full(1, MLA_KV_RANK),
            full(MLA_Q_RANK, 512), full(MLA_KV_RANK, 512), full(MLA_KV_RANK, 512), full(128, 512),
            row(128), row(128), row(128), row(128), row(128), row(128),
        ],
        out_specs=[out, out, out],
        out_shape=[jax.ShapeDtypeStruct((b, s, 512), BF16)] * 3,
        compiler_params=_cparams(("parallel", "parallel")),
    )(proj, proj, q_norm_w[None], kv_norm_w[None], wq, wk, wv, place, *q_tabs, *k_tabs)

    tq, tk = math.gcd(s, 2048), math.gcd(s, 512)
    return pl.pallas_call(
        functools.partial(_mla_flash_kernel, strip=tq),
        grid=(b, s // tq, s // tk),
        in_specs=[
            pl.BlockSpec((None, tq, 512), lambda bi, qi, ki: (bi, qi, 0)),
            pl.BlockSpec((None, tk, 512), lambda bi, qi, ki: (bi, ki, 0)),
            pl.BlockSpec((None, tk, 512), lambda bi, qi, ki: (bi, ki, 0)),
        ],
        out_specs=pl.BlockSpec((None, tq, N_HEADS * HEAD_DIM), lambda bi, qi, ki: (bi, qi, 0)),
        out_shape=jax.ShapeDtypeStruct((b, s, N_HEADS * HEAD_DIM), F32),
        scratch_shapes=[pltpu.VMEM((N_HEADS, tq, 128), F32), pltpu.VMEM((N_HEADS, tq, 128), F32)],
        compiler_params=_cparams(("parallel", "parallel", "arbitrary")),
    )(q, k, v)


def _conv_kernel(prev_ref, cur_ref, next_ref, w_ref, b_ref, o_ref, *, ts):
    i = pl.program_id(1)
    w = w_ref[...]

    def conv(x):
        n = x.shape[0]
        acc = x * w[2:3]
        for s in (-2, -1, 1, 2):
            acc = acc + pltpu.roll(x, (-s) % n, 0) * w[s + 2:s + 3]
        return acc

    act = lambda y: _silu(y + b_ref[...])
    cur = cur_ref[...]
    prev = prev_ref[...] * (i > 0).astype(F32)
    nxt = next_ref[...] * (i < pl.num_programs(1) - 1).astype(F32)
    o_ref[...] = act(conv(cur))
    top = conv(jnp.concatenate([prev, cur[:16]], axis=0))
    o_ref[0:8, :] = act(top[8:16])
    bot = conv(jnp.concatenate([cur[ts - 16:], nxt], axis=0))
    o_ref[ts - 8:ts, :] = act(bot[8:16])


def _ssd_conv(proj, conv_w, conv_b):
    b, s, _ = proj.shape
    ts = 512
    width = 3 * BLK
    cb = COL_XBC * BLK // width
    nb8 = ts // 8
    w8 = jnp.concatenate([conv_w, jnp.zeros((8 - SSM_CONV, width), F32)], axis=0)
    return pl.pallas_call(
        functools.partial(_conv_kernel, ts=ts),
        grid=(b, s // ts),
        in_specs=[
            pl.BlockSpec((None, 8, width), lambda bi, i: (bi, jnp.maximum(i * nb8 - 1, 0), cb)),
            pl.BlockSpec((None, ts, width), lambda bi, i: (bi, i, cb)),
            pl.BlockSpec((None, 8, width), lambda bi, i: (bi, jnp.minimum((i + 1) * nb8, s // 8 - 1), cb)),
            pl.BlockSpec((8, width), lambda bi, i: (0, 0)),
            pl.BlockSpec((1, width), lambda bi, i: (0, 0)),
        ],
        out_specs=pl.BlockSpec((None, ts, width), lambda bi, i: (bi, i, 0)),
        out_shape=jax.ShapeDtypeStruct((b, s, width), F32),
        compiler_params=_cparams(("parallel", "parallel")),
    )(proj, proj, proj, w8, conv_b[None])


def _ssd_tables():
    tril = _tri(CHUNK, lambda i, j: j <= i)
    triu = _tri(CHUNK, lambda i, j: j >= i)
    h = jnp.arange(128)[:, None]
    lane = jnp.arange(256)[None, :]
    e_f = ((lane // HEAD_DIM == h) & (h < N_HEADS)).astype(BF16)
    e_b = ((lane // HEAD_DIM == h - N_HEADS) & (h >= N_HEADS) & (h < 2 * N_HEADS)).astype(BF16)
    return tril, triu, e_f, e_b


def _ssd_kernel(xs_ref, bm_ref, cm_ref, z_ref, dt_ref, dtt_ref, bias_ref, a_ref, biasc_ref, ac_ref,
                dskip_ref, nw_ref, tril_ref, triu_ref, ef_ref, eb_ref, o_ref,
                sf_ref, sb_ref, pb_ref, *, nsteps, g):
    t = pl.program_id(1)
    tril, triu = tril_ref[...], triu_ref[...]

    @pl.when(t == 0)
    def _():
        sf_ref[...] = jnp.zeros_like(sf_ref)
        sb_ref[...] = jnp.zeros_like(sb_ref)

    chunks = [slice(ci * CHUNK, (ci + 1) * CHUNK) for ci in range(g)]

    def step_sizes():
        dt = _softplus(dt_ref[:, :128] + bias_ref[...])
        return dt, dt * a_ref[...]

    def cumulative(tri, dta, expand):
        return _dot_x_exact(jnp.concatenate([_dot_exact_x(tri, dta[r]) for r in chunks], axis=0), expand)

    def state_update(s_ref, bmf, weighted_x, total):
        for gi in range(SSM_GROUPS):
            sl = slice(gi * 128, (gi + 1) * 128)
            upd = _dot(bmf[:, sl].T.astype(BF16), weighted_x[:, sl].astype(BF16))
            s_ref[gi] = s_ref[gi] * jnp.exp(total[:, sl]) + upd

    @pl.when(t < nsteps)
    def _():
        base = (nsteps - 1 - t) * g
        dt, dta = step_sizes()
        rcs_all = cumulative(triu, dta, eb_ref[...])
        xdt_all = xs_ref[...] * _dot_x_exact(dt, eb_ref[...])
        for ci in reversed(range(g)):
            rows = chunks[ci]
            rcs_b = rcs_all[rows]
            tot_b = rcs_b[0:1, :]
            pb_ref[base + ci] = sb_ref[...].astype(BF16)
            state_update(sb_ref, bm_ref[rows, :], jnp.exp(tot_b - rcs_b) * xdt_all[rows], tot_b)

    @pl.when(t >= nsteps)
    def _():
        base = (t - nsteps) * g
        row = lax.broadcasted_iota(jnp.int32, (CHUNK, CHUNK), 0)
        colm = lax.broadcasted_iota(jnp.int32, (CHUNK, CHUNK), 1)
        head = _lane_head((CHUNK, 128), HEAD_DIM)
        efb = jnp.concatenate([ef_ref[...], eb_ref[...]], axis=1)
        dt, dta = step_sizes()
        rcs_all = cumulative(triu, dta, eb_ref[...])
        cs_all = cumulative(tril, dta, ef_ref[...])
        dt_exp = _dot_x_exact(dt, efb)
        dtat = jnp.concatenate([dtt_ref[:, r] for r in chunks], axis=0)
        per_chunk = lambda col: jnp.concatenate([col] * g, axis=0)
        dtat = _softplus(dtat + per_chunk(biasc_ref[...])) * per_chunk(ac_ref[...])
        cst_all = _dot_x_exact(dtat, triu)
        rcst_all = _dot_x_exact(dtat, tril)
        for ci in range(g):
            rows = chunks[ci]
            xs = xs_ref[rows, :]
            bmf = bm_ref[rows, :]
            bm = bmf.astype(BF16)
            cm = cm_ref[rows, :].astype(BF16)
            rcs_b = rcs_all[rows]
            cs_f = cs_all[rows]
            tot_f = cs_f[CHUNK - 1:CHUNK, :]
            xdt_f = xs * dt_exp[rows, :SSM_INNER]
            xdt_b = xs * dt_exp[rows, SSM_INNER:]
            cst = cst_all[8 * ci:8 * ci + 8]
            rcst = rcst_all[8 * ci:8 * ci + 8]
            pb = pb_ref[base + ci]
            ys = []
            for gi in range(SSM_GROUPS):
                sl = slice(gi * 128, (gi + 1) * 128)
                cb = _dot_nt(cm[:, sl], bm[:, sl])
                xcat = jnp.concatenate([xdt_f[:, sl], xdt_b[:, sl]], axis=0).astype(BF16)
                wcats = []
                for hh in range(2):
                    h = 2 * gi + hh
                    c0 = h * HEAD_DIM
                    seg_f = cs_f[:, c0:c0 + 1] - cst[h:h + 1, :]
                    seg_b = rcs_b[:, c0:c0 + 1] - rcst[N_HEADS + h:N_HEADS + h + 1, :]
                    dec_f = jnp.where(row >= colm, jnp.exp(jnp.minimum(seg_f, 0.0)), 0.0)
                    dec_b = jnp.where(row < colm, jnp.exp(jnp.minimum(seg_b, 0.0)), 0.0)
                    wcats.append(jnp.concatenate([cb * dec_f, cb * dec_b], axis=1))
                y2 = _dot(jnp.concatenate(wcats, axis=0).astype(BF16), xcat)
                yg = jnp.where(head == 0, y2[:CHUNK], y2[CHUNK:])
                states = jnp.concatenate([sf_ref[gi].astype(BF16), pb[gi]], axis=1)
                off = _dot(cm[:, sl], states)
                ys.append(yg + jnp.exp(cs_f[:, sl]) * off[:, :128] + jnp.exp(rcs_b[:, sl]) * off[:, 128:])
            y = jnp.concatenate(ys, axis=1) + dskip_ref[...] * xs
            y = y * _silu(z_ref[rows, :])
            o_ref[rows, :] = _rms(y, nw_ref[...])
            state_update(sf_ref, bmf, jnp.exp(tot_f - cs_f) * xdt_f, tot_f)


def _ssd(proj, xbc, a_log, dt_bias, d_skip, norm_w, tables):
    b, s, _ = proj.shape
    nc = s // CHUNK
    g = math.gcd(nc, 4)
    nsteps = nc // g
    rows = g * CHUNK
    dt_t = jnp.swapaxes(proj[:, :, COL_DT * BLK:COL_DT * BLK + 8], 1, 2)
    pad_row = lambda v: jnp.concatenate([v.reshape(1, 8), jnp.zeros((1, 120), F32)], axis=1)
    a = -jnp.exp(a_log.astype(F32))
    both = lambda t: jnp.where(t < nsteps, nsteps - 1 - t, t - nsteps)
    late = lambda t: jnp.where(t < nsteps, 0, t - nsteps)
    full = lambda r, n: pl.BlockSpec((r, n), lambda bi, t: (0, 0))
    blk = lambda idx, c: pl.BlockSpec((None, rows, BLK), lambda bi, t: (bi, idx(t), c))
    return pl.pallas_call(
        functools.partial(_ssd_kernel, nsteps=nsteps, g=g),
        grid=(b, 2 * nsteps),
        in_specs=[
            blk(both, 0), blk(both, 1), blk(late, 2), blk(late, COL_Z), blk(both, COL_DT),
            pl.BlockSpec((None, 8, rows), lambda bi, t: (bi, 0, late(t))),
            full(1, 128), full(1, 128), full(8, 1), full(8, 1),
            full(1, SSM_INNER), full(1, SSM_INNER),
            full(CHUNK, CHUNK), full(CHUNK, CHUNK), full(128, 256), full(128, 256),
        ],
        out_specs=blk(late, 0),
        out_shape=jax.ShapeDtypeStruct((b, s, SSM_INNER), F32),
        scratch_shapes=[pltpu.VMEM((SSM_GROUPS, 128, 128), F32), pltpu.VMEM((SSM_GROUPS, 128, 128), F32),
                        pltpu.VMEM((nc, SSM_GROUPS, 128, 128), BF16)],
        compiler_params=_cparams(("parallel", "arbitrary")),
    )(xbc, xbc, xbc, proj, proj, dt_t, pad_row(dt_bias), pad_row(a), dt_bias.reshape(8, 1), a.reshape(8, 1),
      jnp.repeat(d_skip, HEAD_DIM)[None], norm_w[None], *tables)


def _dil_kernel(q_ref, k_ref, v_ref, o_ref, m_ref, l_ref, a_ref, *, s):
    head = _lane_head((128, 128), HEAD_DIM)
    kw = 2 * 128
    for pi, (win, d) in enumerate(DIL_PATTERNS):
        half = win // (2 * d)
        seg = s // d
        per_seg = seg // 128

        def body(i, carry, d=d, pi=pi, half=half, seg=seg, per_seg=per_seg):
            r = i // per_seg
            m0 = (i % per_seg) * 128
            ks = jnp.clip(m0 - half, 0, seg - kw)
            if d == 1:
                qrows = pl.ds(pl.multiple_of(m0, 128), 128)
                krows = pl.ds(pl.multiple_of(ks, 64), kw)
            else:
                qrows = pl.ds(r + d * m0, 128, stride=d)
                krows = pl.ds(r + d * ks, kw, stride=d)
            q = q_ref[qrows, :]
            kb = k_ref[krows, :].astype(BF16)
            vb = v_ref[krows, :].astype(BF16)
            qpos = m0 + lax.broadcasted_iota(jnp.int32, (256, kw), 0) % 128
            kpos = ks + lax.broadcasted_iota(jnp.int32, (256, kw), 1)
            valid = jnp.abs(qpos - kpos) <= half
            q2 = jnp.concatenate([jnp.where(head == hh, q, 0.0) for hh in range(2)], axis=0).astype(BF16)
            sc = jnp.where(valid, _dot_nt(q2, kb), -jnp.inf)
            m2 = jnp.max(sc, axis=-1, keepdims=True)
            p = jnp.exp(sc - m2)
            l2 = jnp.sum(p, axis=-1, keepdims=True)
            a2 = _dot(p.astype(BF16), vb)
            m_ref.at[pi][qrows, :] = jnp.where(head == 0, m2[:128], m2[128:])
            l_ref.at[pi][qrows, :] = jnp.where(head == 0, l2[:128], l2[128:])
            a_ref.at[pi][qrows, :] = jnp.where(head == 0, a2[:128], a2[128:])
            return carry

        lax.fori_loop(0, s // 128, body, 0, unroll=8)

    def merge(i, carry):
        rows = pl.ds(pl.multiple_of(i * 512, 512), 512)
        ms = [m_ref[pi, rows, :] for pi in range(len(DIL_PATTERNS))]
        top = functools.reduce(jnp.maximum, ms)
        ws = [jnp.exp(m - top) for m in ms]
        den = sum(w * l_ref[pi, rows, :] for pi, w in enumerate(ws))
        num = sum(w * a_ref[pi, rows, :] for pi, w in enumerate(ws))
        o_ref[rows, :] = num / den
        return carry

    lax.fori_loop(0, s // 512, merge, 0)


def _dilated(proj):
    b, s, _ = proj.shape
    assert all(s // d >= 256 for _, d in DIL_PATTERNS)
    seqb = lambda c: pl.BlockSpec((None, s, 128), lambda bi, p: (bi, 0, 2 * c + p))
    return pl.pallas_call(
        functools.partial(_dil_kernel, s=s),
        grid=(b, 2),
        in_specs=[seqb(COL_DQ), seqb(COL_DK), seqb(COL_DV)],
        out_specs=seqb(0),
        out_shape=jax.ShapeDtypeStruct((b, s, BLK), F32),
        scratch_shapes=[pltpu.VMEM((len(DIL_PATTERNS), s, 128), F32)] * 3,
        compiler_params=_cparams(("parallel", "parallel")),
    )(proj, proj, proj)


def _outproj_kernel(*refs, n_res):
    ya_ref, yb_ref, yc_ref, yd_ref, w_ref, lnw_ref, rw_ref, xo_ref, h_ref, aff_ref = refs[n_res:]
    acc = refs[0][...]
    for r in refs[1:n_res]:
        acc = acc + r[...]
    mixed = jnp.concatenate([r[...].astype(BF16) for r in (ya_ref, yb_ref, yc_ref, yd_ref)], axis=1)
    acc = acc + _dot(mixed, w_ref[...])
    xo_ref[...] = acc
    h = _rms(acc, lnw_ref[...])
    h_hi = h.astype(BF16)
    h_ref[...] = h_hi
    h_lo = (h - h_hi.astype(F32)).astype(BF16)
    rw = rw_ref[...]
    w_hi = rw.astype(BF16)
    w_lo = (rw - w_hi.astype(F32)).astype(BF16)
    tm = h.shape[0]
    r = _dot(jnp.concatenate([h_hi, h_lo], axis=0), jnp.concatenate([w_hi, w_lo], axis=1))
    logits = r[:tm, :128] + r[:tm, 128:] + r[tm:, :128] + r[tm:, 128:]
    lane = lax.broadcasted_iota(jnp.int32, logits.shape, 1)
    logits = jnp.where(lane < N_EXPERTS, logits, -jnp.inf)
    e = jnp.exp(logits - jnp.max(logits, axis=-1, keepdims=True))
    aff = e / jnp.sum(e, axis=-1, keepdims=True)
    aff_ref[...] = aff.T[:N_EXPERTS, :]


def _outproj(res, ys, w_out, ln2_w, router_w):
    t = res[0].shape[0]
    tm = 256
    rw = jnp.concatenate([router_w, jnp.zeros((D_MODEL, 128 - N_EXPERTS), F32)], axis=1)
    row = lambda n: pl.BlockSpec((tm, n), lambda i: (i, 0))
    full = lambda r, n: pl.BlockSpec((r, n), lambda i: (0, 0))
    return pl.pallas_call(
        functools.partial(_outproj_kernel, n_res=len(res)),
        grid=(t // tm,),
        in_specs=[row(D_MODEL)] * len(res) + [row(BLK)] * 4 + [
            full(D_MODEL, D_MODEL), full(1, D_MODEL), full(D_MODEL, 128)],
        out_specs=[row(D_MODEL), row(D_MODEL), pl.BlockSpec((N_EXPERTS, tm), lambda i: (0, i))],
        out_shape=[jax.ShapeDtypeStruct((t, D_MODEL), F32),
                   jax.ShapeDtypeStruct((t, D_MODEL), BF16),
                   jax.ShapeDtypeStruct((N_EXPERTS, t), F32)],
        compiler_params=_cparams(("parallel",)),
    )(*res, *ys, w_out.astype(BF16), ln2_w[None], rw)


def _exclusive_prefix(x, tri):
    n = x.shape[1] // 128
    off = jnp.zeros((x.shape[0], 1), F32)
    parts = []
    for i in range(n):
        xt = x[:, i * 128:(i + 1) * 128]
        incl = _dot(xt.astype(BF16), tri)
        parts.append(incl - xt + off)
        off = off + incl[:, 127:128]
    return jnp.concatenate(parts, axis=1)


def _select_kernel(aff_ref, tri_ref, key_ref, *, cap):
    a = aff_ref[...]
    n_exp = a.shape[0]
    thr_bits = jnp.zeros((n_exp, 1), jnp.int32)
    for bit in range(30, -1, -1):
        cand = thr_bits | (1 << bit)
        cnt = jnp.sum(jnp.where(a >= lax.bitcast_convert_type(cand, F32), 1.0, 0.0), axis=1, keepdims=True)
        thr_bits = jnp.where(cnt >= cap, cand, thr_bits)
    thr = lax.bitcast_convert_type(thr_bits, F32)
    above = jnp.where(a > thr, 1.0, 0.0)
    tie = jnp.where(a == thr, 1.0, 0.0)
    need = cap - jnp.sum(above, axis=1, keepdims=True)
    tri = tri_ref[...]
    sel = above + tie * jnp.where(_exclusive_prefix(tie, tri) < need, 1.0, 0.0)
    key_ref[...] = jnp.where(sel > 0.0, _exclusive_prefix(sel, tri), -1.0)


def _slot_kernel(key_ref, aff_ref, ltri_ref, idx_ref, gate_ref, *, cap, nt):
    n_exp = key_ref.shape[0]
    pad = jnp.zeros((128 - nt, 128), F32)
    slot = lax.broadcasted_iota(jnp.int32, (128, cap), 1).astype(F32)
    sub = lax.broadcasted_iota(jnp.int32, (128, cap), 0).astype(F32)
    ltri = ltri_ref[...]

    def per_expert(e, carry):
        key = jnp.concatenate([key_ref[e], pad - 1.0], axis=0)
        aff = jnp.concatenate([aff_ref[e], pad], axis=0)
        cnt = jnp.sum(jnp.where(key >= 0.0, 1.0, 0.0), axis=1, keepdims=True)
        cum = _dot(ltri, jnp.broadcast_to(cnt, (128, 128)).astype(BF16))
        in_tile = (jnp.where(slot >= (cum - cnt)[:, :1], 1.0, 0.0)
                   * jnp.where(slot < cum[:, :1], 1.0, 0.0))
        g = in_tile.astype(BF16)
        k_hi, k_lo = _split2(key.T)
        slot_of = _dot(k_hi, g) + _dot(k_lo, g)
        hit = jnp.where(slot_of == slot, 1.0, 0.0)
        aff_t = aff.T
        a_hi = aff_t.astype(BF16)
        r1 = aff_t - a_hi.astype(F32)
        a_mid = r1.astype(BF16)
        a_lo = (r1 - a_mid.astype(F32)).astype(BF16)
        aff_of = _dot(a_hi, g) + _dot(a_mid, g) + _dot(a_lo, g)
        tile = jnp.sum(in_tile * sub, axis=0, keepdims=True)
        within = jnp.sum(hit * sub, axis=0, keepdims=True)
        idx_ref[pl.ds(e, 1), :] = (128.0 * tile + within).astype(jnp.int32)
        gate_ref[pl.ds(e, 1), :] = jnp.sum(hit * aff_of, axis=0, keepdims=True)
        return carry

    lax.fori_loop(0, n_exp, per_expert, 0, unroll=4)


def _moe_select(aff_t, b, s, cap):
    nt = s // 128
    assert nt <= 128 and nt % 8 == 0 and cap % 128 == 0
    key = pl.pallas_call(
        functools.partial(_select_kernel, cap=cap),
        grid=(b,),
        in_specs=[pl.BlockSpec((N_EXPERTS, s), lambda bi: (0, bi)),
                  pl.BlockSpec((128, 128), lambda bi: (0, 0))],
        out_specs=pl.BlockSpec((None, N_EXPERTS, s), lambda bi: (bi, 0, 0)),
        out_shape=jax.ShapeDtypeStruct((b, N_EXPERTS, s), F32),
        compiler_params=_cparams(("parallel",)),
    )(aff_t, _tri(128, lambda i, j: i <= j))
    out = pl.BlockSpec((None, N_EXPERTS, cap), lambda bi: (bi, 0, 0))
    return pl.pallas_call(
        functools.partial(_slot_kernel, cap=cap, nt=nt),
        grid=(b,),
        in_specs=[pl.BlockSpec((None, N_EXPERTS, nt, 128), lambda bi: (bi, 0, 0, 0)),
                  pl.BlockSpec((N_EXPERTS, nt, 128), lambda bi: (0, bi, 0)),
                  pl.BlockSpec((128, 128), lambda bi: (0, 0))],
        out_specs=[out, out],
        out_shape=[jax.ShapeDtypeStruct((b, N_EXPERTS, cap), jnp.int32),
                   jax.ShapeDtypeStruct((b, N_EXPERTS, cap), F32)],
        compiler_params=_cparams(("parallel",)),
    )(key.reshape(b, N_EXPERTS, nt, 128), aff_t.reshape(N_EXPERTS, b * nt, 128), _tri(128, lambda i, j: j <= i))


ROWS_PER_ITER = 8


def _gather_kernel(idx_ref, h_ref, o_ref, hf_ref, buf_ref, *, cap, n_exp):
    e = pl.program_id(1)
    base = (pl.program_id(0) * n_exp + e) * cap

    @pl.when(e == 0)
    def _():
        hf_ref[...] = h_ref[...].astype(F32)

    def body(jb, carry):
        j0 = pl.multiple_of(jb * ROWS_PER_ITER, ROWS_PER_ITER)
        vals = [hf_ref[pl.ds(idx_ref[base + j0 + u], 1), :] for u in range(ROWS_PER_ITER)]
        for u in range(ROWS_PER_ITER):
            buf_ref[pl.ds(j0 + u, 1), :] = vals[u]
        return carry

    lax.fori_loop(0, cap // ROWS_PER_ITER, body, 0)
    o_ref[...] = buf_ref[...].astype(BF16)


def _moe_gather(idx_flat, h, cap):
    b, s, d = h.shape
    return pl.pallas_call(
        functools.partial(_gather_kernel, cap=cap, n_exp=N_EXPERTS),
        grid_spec=pltpu.PrefetchScalarGridSpec(
            num_scalar_prefetch=1,
            grid=(b, N_EXPERTS),
            in_specs=[pl.BlockSpec((None, s, d), lambda bi, e, idx: (bi, 0, 0))],
            out_specs=pl.BlockSpec((None, None, cap, d), lambda bi, e, idx: (e, bi, 0, 0)),
            scratch_shapes=[pltpu.VMEM((s, d), F32), pltpu.VMEM((cap, d), F32)],
        ),
        out_shape=jax.ShapeDtypeStruct((N_EXPERTS, b, cap, d), BF16),
        compiler_params=_cparams(("parallel", "arbitrary")),
    )(idx_flat, h)


def _ffn_kernel(x_ref, wg_ref, wu_ref, wd_ref, o_ref, *, rows, tr):
    f = pl.program_id(1)
    wg = wg_ref[...].astype(BF16)
    wu = wu_ref[...].astype(BF16)
    wd = wd_ref[...].astype(BF16)

    def partial_out(sl):
        x = x_ref[sl, :]
        hid = (_silu(_dot(x, wg)) * _dot(x, wu)).astype(BF16)
        return _dot(hid, wd)

    for r in range(rows // tr):
        sl = slice(r * tr, (r + 1) * tr)

        @pl.when(f == 0)
        def _():
            o_ref[sl, :] = partial_out(sl)

        @pl.when(f > 0)
        def _():
            o_ref[sl, :] += partial_out(sl)


def _moe_ffn(xin, w_gate, w_up, w_down, layer):
    n_exp, rows, d = xin.shape
    tf = 512
    return pl.pallas_call(
        functools.partial(_ffn_kernel, rows=rows, tr=math.gcd(rows, 1024)),
        grid=(n_exp, EXPERT_FF // tf),
        in_specs=[
            pl.BlockSpec((None, rows, d), lambda e, f: (e, 0, 0)),
            pl.BlockSpec((None, None, d, tf), lambda e, f: (layer, e, 0, f)),
            pl.BlockSpec((None, None, d, tf), lambda e, f: (layer, e, 0, f)),
            pl.BlockSpec((None, None, tf, d), lambda e, f: (layer, e, f, 0)),
        ],
        out_specs=pl.BlockSpec((None, rows, d), lambda e, f: (e, 0, 0)),
        out_shape=jax.ShapeDtypeStruct((n_exp, rows, d), F32),
        compiler_params=_cparams(("parallel", "arbitrary")),
    )(xin, w_gate, w_up, w_down)


def _combine_kernel(idx_ref, gate_ref, y_ref, o_ref, *, cap, n_exp):
    e = pl.program_id(1)
    base = (pl.program_id(0) * n_exp + e) * cap

    @pl.when(e == 0)
    def _():
        o_ref[...] = jnp.zeros_like(o_ref)

    def body(jb, carry):
        j0 = pl.multiple_of(jb * ROWS_PER_ITER, ROWS_PER_ITER)
        rows = [idx_ref[base + j0 + u] for u in range(ROWS_PER_ITER)]
        vals = [o_ref[pl.ds(rows[u], 1), :] + gate_ref[base + j0 + u] * y_ref[pl.ds(j0 + u, 1), :]
                for u in range(ROWS_PER_ITER)]
        for u in range(ROWS_PER_ITER):
            o_ref[pl.ds(rows[u], 1), :] = vals[u]
        return carry

    lax.fori_loop(0, cap // ROWS_PER_ITER, body, 0)


def _moe_combine(idx_flat, gate_flat, y, b, s, cap):
    d = y.shape[-1]
    return pl.pallas_call(
        functools.partial(_combine_kernel, cap=cap, n_exp=N_EXPERTS),
        grid_spec=pltpu.PrefetchScalarGridSpec(
            num_scalar_prefetch=2,
            grid=(b, N_EXPERTS),
            in_specs=[pl.BlockSpec((None, None, cap, d), lambda bi, e, idx, gate: (e, bi, 0, 0))],
            out_specs=pl.BlockSpec((None, s, d), lambda bi, e, idx, gate: (bi, 0, 0)),
        ),
        out_shape=jax.ShapeDtypeStruct((b, s, d), F32),
        compiler_params=_cparams(("parallel", "arbitrary")),
    )(idx_flat, gate_flat, y)


def _final_norm_kernel(xa_ref, xb_ref, w_ref, o_ref):
    o_ref[...] = _rms(xa_ref[...] + xb_ref[...], w_ref[...])


def _final_norm(xa, xb, w):
    t = xa.shape[0]
    tm = 1024
    row = pl.BlockSpec((tm, D_MODEL), lambda i: (i, 0))
    return pl.pallas_call(
        _final_norm_kernel,
        grid=(t // tm,),
        in_specs=[row, row, pl.BlockSpec((1, D_MODEL), lambda i: (0, 0))],
        out_specs=row,
        out_shape=jax.ShapeDtypeStruct((t, D_MODEL), F32),
        compiler_params=_cparams(("parallel",)),
    )(xa, xb, w[None])


def _moe(h, aff, w_gate, w_up, w_down, layer, b, s):
    d = h.shape[-1]
    cap = EC_CAPACITY_FACTOR * s // N_EXPERTS
    token_idx, gate = _moe_select(aff, b, s, cap)
    idx_flat = token_idx.reshape(-1)
    xin = _moe_gather(idx_flat, h.reshape(b, s, d), cap)
    y = _moe_ffn(xin.reshape(N_EXPERTS, b * cap, d), w_gate, w_up, w_down, layer)
    return _moe_combine(idx_flat, gate.reshape(-1), y.reshape(N_EXPERTS, b, cap, d), b, s, cap)


def kernel(x, ln1_w, w_in, mla_q_norm_w, mla_kv_norm_w, mla_w_uq, mla_w_ukv, ssm_conv_w, ssm_conv_b,
           ssm_a_log, ssm_dt_bias, ssm_d, ssm_norm_w, w_out, ln2_w, router_w, exp_w_gate, exp_w_up,
           exp_w_down, final_norm_w):
    b, s, d = x.shape
    depth = w_in.shape[0]
    ret_tabs = _ret_tables()
    ret_rope = _rope_tables(s, HEAD_DIM, RET_THETA, HEAD_DIM, 0, 2)
    mla_q_rope = _rope_tables(s, MLA_ROPE, ROPE_THETA, 128, MLA_NOPE, 1)
    mla_k_rope = _rope_tables(s, MLA_ROPE, ROPE_THETA, 128, 0, 1)
    dil_rope = _rope_tables(s, ROPE_DIM, ROPE_THETA, HEAD_DIM, 0, 2)
    ssd_tabs = _ssd_tables()
    res = [x.reshape(b * s, d)]
    for i in range(depth):
        proj = _inproj(res, ln1_w[i][None], _pad_cols(w_in[i]).astype(BF16), ret_rope, dil_rope, s)
        proj = proj.reshape(b, s, PROJ_PAD)
        y_a = _retention(proj, ret_tabs)
        y_b = _mla(proj, mla_q_norm_w[i], mla_kv_norm_w[i], mla_w_uq[i], mla_w_ukv[i], mla_q_rope, mla_k_rope)
        xbc = _ssd_conv(proj, ssm_conv_w[i], ssm_conv_b[i])
        y_c = _ssd(proj, xbc, ssm_a_log[i], ssm_dt_bias[i], ssm_d[i], ssm_norm_w[i], ssd_tabs)
        y_d = _dilated(proj)
        ys = [y.reshape(b * s, BLK) for y in (y_a, y_b, y_c, y_d)]
        x_mid, h, aff = _outproj(res, ys, w_out[i], ln2_w[i], router_w[i])
        moe = _moe(h, aff, exp_w_gate, exp_w_up, exp_w_down, i, b, s)
        res = [x_mid, moe.reshape(b * s, d)]
    return _final_norm(res[0], res[1], final_norm_w).reshape(b, s, d)
```

```python
import functools
import math

import jax
import jax.numpy as jnp
from jax import lax
from jax.experimental import pallas as pl
from jax.experimental.pallas import tpu as pltpu

F32 = jnp.float32
BF16 = jnp.bfloat16

D_MODEL = 1024
RMS_EPS = 1e-6
GN_EPS = 1e-5
CHUNK = 128
HEAD_DIM = 64
N_HEADS = 4

RET_THETA = 10000.0
ROPE_THETA = 500000.0
ROPE_DIM = 16

MLA_Q_RANK = 256
MLA_KV_RANK = 128
MLA_NOPE = 64
MLA_ROPE = 32

SSM_GROUPS = 2
SSM_STATE = 128
SSM_CONV = 5
SSM_INNER = 256

DIL_PATTERNS = ((128, 1), (512, 4), (2048, 16))

N_EXPERTS = 16
EXPERT_FF = 2048
EC_CAPACITY_FACTOR = 2

BLK = 256
COL_RQ, COL_RK, COL_RV, COL_RG = 0, 1, 2, 3
COL_MCQ, COL_MKV = 4, 5
COL_XBC = 6
COL_Z = 9
COL_DT = 10
COL_DQ, COL_DK, COL_DV = 11, 12, 13
PROJ_PAD = 14 * BLK

VMEM_LIMIT = 56 * 1024 * 1024


def _cparams(sem):
    return pltpu.CompilerParams(dimension_semantics=sem, vmem_limit_bytes=VMEM_LIMIT)


def _split2(x):
    hi = x.astype(BF16)
    lo = (x - hi.astype(F32)).astype(BF16)
    return hi, lo


def _dot(a, b):
    return jnp.dot(a, b, preferred_element_type=F32)


def _dot_nt(a, b):
    return lax.dot_general(a, b, (((1,), (1,)), ((), ())), preferred_element_type=F32)


def _dot_x_exact(x, m):
    hi, lo = _split2(x)
    return _dot(jnp.concatenate([hi, lo], axis=1), jnp.concatenate([m, m], axis=0))


def _dot_exact_x(m, x):
    hi, lo = _split2(x)
    n = x.shape[1]
    r = _dot(m, jnp.concatenate([hi, lo], axis=1))
    return r[:, :n] + r[:, n:]


def _rope(x, c, s1, s2, half):
    w = x.shape[-1]
    return x * c + pltpu.roll(x, w - half, 1) * s1 + pltpu.roll(x, half, 1) * s2


def _silu(x):
    return x * (1.0 / (1.0 + jnp.exp(-x)))


def _softplus(x):
    return jnp.maximum(x, 0.0) + jnp.log(1.0 + jnp.exp(-jnp.abs(x)))


def _lane_head(shape, width):
    return lax.broadcasted_iota(jnp.int32, shape, 1) // width


def _rms(x, w):
    return x * lax.rsqrt(jnp.mean(x * x, axis=-1, keepdims=True) + RMS_EPS) * w


def _rope_tables(s, rot_dim, theta, head_dim, offset, n_heads):
    half = rot_dim // 2
    inv = 1.0 / (theta ** (jnp.arange(0, rot_dim, 2, dtype=F32) / rot_dim))
    ang = jnp.arange(s, dtype=F32)[:, None] * inv[None, :]
    cos, sin = jnp.cos(ang), jnp.sin(ang)
    pre0 = jnp.zeros((s, offset), F32)
    pre1 = jnp.ones((s, offset), F32)
    post0 = jnp.zeros((s, head_dim - offset - rot_dim), F32)
    post1 = jnp.ones((s, head_dim - offset - rot_dim), F32)
    zh = jnp.zeros((s, half), F32)
    c = jnp.concatenate([pre1, cos, cos, post1], axis=-1)
    s1 = jnp.concatenate([pre0, -sin, zh, post0], axis=-1)
    s2 = jnp.concatenate([pre0, zh, sin, post0], axis=-1)
    tile = lambda t: jnp.tile(t, (1, n_heads))
    return tile(c), tile(s1), tile(s2)


def _tri(n, fn):
    i = jnp.arange(n)
    return fn(i[:, None], i[None, :]).astype(BF16)


def _inproj_kernel(*refs, n_res, col_chunk):
    lnw_ref, w_ref, rc_ref, rs1_ref, rs2_ref, dc_ref, ds1_ref, ds2_ref, o_ref = refs[n_res:]
    x = refs[0][...]
    for r in refs[1:n_res]:
        x = x + r[...]
    yb = _rms(x, lnw_ref[...]).astype(BF16)
    for j in range(PROJ_PAD // col_chunk):
        sl = slice(j * col_chunk, (j + 1) * col_chunk)
        o_ref[:, sl] = _dot(yb, w_ref[:, sl])
    wide = lambda tabs: [jnp.concatenate([r[...]] * 2, axis=1) for r in tabs]
    ret = wide((rc_ref, rs1_ref, rs2_ref))
    dil = wide((dc_ref, ds1_ref, ds2_ref))
    for col, tabs, half, scale in ((COL_RQ, ret, HEAD_DIM // 2, HEAD_DIM ** -0.5), (COL_RK, ret, HEAD_DIM // 2, None),
                                   (COL_DQ, dil, ROPE_DIM // 2, HEAD_DIM ** -0.5), (COL_DK, dil, ROPE_DIM // 2, None)):
        sl = slice(col * BLK, (col + 1) * BLK)
        y = _rope(o_ref[:, sl], *tabs, half)
        o_ref[:, sl] = y if scale is None else y * scale


def _inproj(res, lnw, w_pad, ret_rope, dil_rope, s):
    t = res[0].shape[0]
    tm = math.gcd(s, 512)
    tab = pl.BlockSpec((tm, 128), lambda i: (i % (s // tm), 0))
    return pl.pallas_call(
        functools.partial(_inproj_kernel, n_res=len(res), col_chunk=512),
        grid=(t // tm,),
        in_specs=[pl.BlockSpec((tm, D_MODEL), lambda i: (i, 0))] * len(res) + [
            pl.BlockSpec((1, D_MODEL), lambda i: (0, 0)),
            pl.BlockSpec((D_MODEL, PROJ_PAD), lambda i: (0, 0)),
        ] + [tab] * 6,
        out_specs=pl.BlockSpec((tm, PROJ_PAD), lambda i: (i, 0)),
        out_shape=jax.ShapeDtypeStruct((t, PROJ_PAD), F32),
        compiler_params=_cparams(("parallel",)),
    )(*res, lnw, w_pad, *ret_rope, *dil_rope)


def _pad_cols(w):
    sizes = (256, 256, 256, 256, 256, 128, 32, 256, 768, 8, 256, 256, 256)
    pts, acc = [], 0
    for sz in sizes:
        pts.append((acc, acc + sz))
        acc += sz
    seg = lambda i: w[:, pts[i][0]:pts[i][1]]
    z = lambda n: jnp.zeros((w.shape[0], n), w.dtype)
    cols = [seg(0), seg(1), seg(2), seg(3), seg(4),
            seg(5), seg(6), z(BLK - 128 - 32),
            seg(8), seg(7), seg(9), z(BLK - 8),
            seg(10), seg(11), seg(12)]
    return jnp.concatenate(cols, axis=1)


def _ret_tables():
    pos = jnp.arange(CHUNK, dtype=F32)
    hh = jnp.arange(N_HEADS, dtype=F32)
    lg_f = jnp.log1p(-jnp.exp2(-5.0 - hh))
    lg_b = jnp.log1p(-jnp.exp2(-5.5 - hh))
    diff = pos[:, None] - pos[None, :]
    d_f = jnp.where(diff >= 0, jnp.exp(lg_f[:, None, None] * jnp.maximum(diff, 0.0)), 0.0)
    d_b = jnp.where(diff < 0, jnp.exp(lg_b[:, None, None] * jnp.maximum(-diff, 0.0)), 0.0)
    dmat = (d_f + d_b).reshape(2, 2 * CHUNK, CHUNK)

    def lanes(per_head):
        t = jnp.repeat(per_head[:, :, None], HEAD_DIM, axis=2)
        return t.reshape(2, 2, CHUNK, HEAD_DIM).transpose(0, 2, 1, 3).reshape(2, CHUNK, 2 * HEAD_DIM)

    xi_f = lanes(jnp.exp(lg_f[:, None] * (pos + 1.0)))
    zeta_f = lanes(jnp.exp(lg_f[:, None] * (CHUNK - 1.0 - pos)))
    xi_b = lanes(jnp.exp(lg_b[:, None] * (CHUNK - pos)))
    zeta_b = lanes(jnp.exp(lg_b[:, None] * pos))
    blk = (jnp.arange(128)[:, None] // HEAD_DIM) == (jnp.arange(128)[None, :] // HEAD_DIM)

    def cdec(lg):
        g = jnp.exp(lg * CHUNK).reshape(2, 2)
        rows = jnp.repeat(g, HEAD_DIM, axis=1)
        return jnp.where(blk[None], rows[:, :, None], 0.0)

    lane = jnp.arange(BLK) // HEAD_DIM
    avg = jnp.where(lane[:, None] == lane[None, :], 1.0 / HEAD_DIM, 0.0).astype(BF16)
    return dmat, xi_f, zeta_f, xi_b, zeta_b, cdec(lg_f), cdec(lg_b), avg


def _ret_kernel(q_ref, k_ref, v_ref, g_ref, dmat_ref, xif_ref, zf_ref,
                xib_ref, zb_ref, cdf_ref, cdb_ref, avg_ref, o_ref, sf_ref, sb_ref, pb_ref, *, nsteps, g):
    t = pl.program_id(1)

    @pl.when(t == 0)
    def _():
        sf_ref[...] = jnp.zeros_like(sf_ref)
        sb_ref[...] = jnp.zeros_like(sb_ref)

    def state_update(s_ref, p, k, vb, zeta, cdec):
        kz = (k * zeta).T.astype(BF16)
        blockmask = jnp.where(cdec > 0.0, 1.0, 0.0)
        s_ref[p] = s_ref[p] * cdec + _dot(kz, vb) * blockmask

    @pl.when(t < nsteps)
    def _():
        base = (nsteps - 1 - t) * g
        for ci in reversed(range(g)):
            rows = slice(ci * CHUNK, (ci + 1) * CHUNK)
            k = k_ref[rows, :]
            vb = v_ref[rows, :].astype(BF16)
            for p in range(2):
                sl = slice(p * 128, (p + 1) * 128)
                pb_ref[base + ci, p] = sb_ref[p].astype(BF16)
                state_update(sb_ref, p, k[:, sl], vb[:, sl], zb_ref[p], cdb_ref[p])

    @pl.when(t >= nsteps)
    def _():
        base = (t - nsteps) * g
        head = _lane_head((CHUNK, 128), HEAD_DIM)
        for ci in range(g):
            rows = slice(ci * CHUNK, (ci + 1) * CHUNK)
            k_all = k_ref[rows, :]
            q_all = q_ref[rows, :]
            vb_all = v_ref[rows, :].astype(BF16)
            for p in range(2):
                sl = slice(p * 128, (p + 1) * 128)
                q, k, vb = q_all[:, sl], k_all[:, sl], vb_all[:, sl]
                kb = k.astype(BF16)
                q2 = jnp.concatenate([jnp.where(head == hh, q, 0.0) for hh in range(2)], axis=0).astype(BF16)
                pm = (_dot_nt(q2, kb) * dmat_ref[p]).astype(BF16)
                o2 = _dot(pm, vb)
                inner = jnp.where(head == 0, o2[:CHUNK], o2[CHUNK:])
                cross_f = _dot((q * xif_ref[p]).astype(BF16), sf_ref[p].astype(BF16))
                cross_b = _dot((q * xib_ref[p]).astype(BF16), pb_ref[base + ci, p])
                o_ref[rows, sl] = inner + cross_f + cross_b
                state_update(sf_ref, p, k, vb, zf_ref[p], cdf_ref[p])

        avg = avg_ref[...]

        def group_mean(x):
            hi, lo = _split2(x)
            return _dot(hi, avg) + _dot(lo, avg)

        o = o_ref[...]
        d = o - group_mean(o)
        var = group_mean(d * d)
        o_ref[...] = _silu(g_ref[...]) * (d * lax.rsqrt(var + GN_EPS))


def _retention(proj, tables):
    b, s, _ = proj.shape
    nc = s // CHUNK
    g = math.gcd(nc, 4)
    nsteps = nc // g
    rows = g * CHUNK
    both = lambda t: jnp.where(t < nsteps, nsteps - 1 - t, t - nsteps)
    late = lambda t: jnp.where(t < nsteps, 0, t - nsteps)
    col = lambda base, idx: pl.BlockSpec((None, rows, BLK), lambda bi, t: (bi, idx(t), base))
    pair = pl.BlockSpec((2, CHUNK, 128), lambda bi, t: (0, 0, 0))
    return pl.pallas_call(
        functools.partial(_ret_kernel, nsteps=nsteps, g=g),
        grid=(b, 2 * nsteps),
        in_specs=[
            col(COL_RQ, late), col(COL_RK, both), col(COL_RV, both), col(COL_RG, late),
            pl.BlockSpec((2, 2 * CHUNK, CHUNK), lambda bi, t: (0, 0, 0)),
            pair, pair, pair, pair, pair, pair,
            pl.BlockSpec((BLK, BLK), lambda bi, t: (0, 0)),
        ],
        out_specs=pl.BlockSpec((None, rows, BLK), lambda bi, t: (bi, late(t), 0)),
        out_shape=jax.ShapeDtypeStruct((b, s, BLK), F32),
        scratch_shapes=[pltpu.VMEM((2, 128, 128), F32), pltpu.VMEM((2, 128, 128), F32),
                        pltpu.VMEM((nc, 2, 128, 128), BF16)],
        compiler_params=_cparams(("parallel", "arbitrary")),
    )(proj, proj, proj, proj, *tables)


def _mla_prep_kernel(cq_ref, kv_ref, qnw_ref, kvnw_ref, wq_ref, wk_ref, wv_ref, place_ref,
                     qc_ref, qs1_ref, qs2_ref, kc_ref, ks1_ref, ks2_ref, q_ref, k_ref, v_ref):
    cqn = _rms(cq_ref[...], qnw_ref[...])
    q = _dot(cqn.astype(BF16), wq_ref[...])
    per_head = lambda r: jnp.concatenate([r[...]] * N_HEADS, axis=1)
    q = _rope(q, per_head(qc_ref), per_head(qs1_ref), per_head(qs2_ref), MLA_ROPE // 2)
    q_ref[...] = (q * ((MLA_NOPE + MLA_ROPE) ** -0.5 * math.log2(math.e))).astype(BF16)
    blk = kv_ref[...]
    ckvn = _rms(blk[:, :MLA_KV_RANK], kvnw_ref[...]).astype(BF16)
    kr = _rope(blk[:, MLA_KV_RANK:], kc_ref[...], ks1_ref[...], ks2_ref[...], MLA_ROPE // 2)
    k_ref[...] = (_dot(ckvn, wk_ref[...]) + _dot(kr.astype(BF16), place_ref[...])).astype(BF16)
    v = _dot(ckvn, wv_ref[...])
    ones = (lax.broadcasted_iota(jnp.int32, v.shape, 1) % 128) >= HEAD_DIM
    v_ref[...] = jnp.where(ones, 1.0, v).astype(BF16)


def _mla_weights(w_uq, w_ukv):
    qh = w_uq.reshape(MLA_Q_RANK, N_HEADS, MLA_NOPE + MLA_ROPE)
    wq = jnp.concatenate([qh, jnp.zeros((MLA_Q_RANK, N_HEADS, 32), F32)], axis=-1).reshape(MLA_Q_RANK, 512)
    kvh = w_ukv.reshape(MLA_KV_RANK, N_HEADS, MLA_NOPE + HEAD_DIM)
    zk = jnp.zeros((MLA_KV_RANK, N_HEADS, 64), F32)
    wk = jnp.concatenate([kvh[..., :MLA_NOPE], zk], axis=-1).reshape(MLA_KV_RANK, 512)
    wv = jnp.concatenate([kvh[..., MLA_NOPE:], zk], axis=-1).reshape(MLA_KV_RANK, 512)
    src = jnp.arange(128)[:, None]
    dst = jnp.arange(512)[None, :]
    place = ((dst % 128 == src + MLA_NOPE) & (src < MLA_ROPE)).astype(BF16)
    return wq.astype(BF16), wk.astype(BF16), wv.astype(BF16), place


def _mla_flash_kernel(q_ref, k_ref, v_ref, o_ref, m_ref, acc_ref, *, strip):
    ki = pl.program_id(2)

    @pl.when(ki == 0)
    def _():
        m_ref[...] = jnp.full_like(m_ref, -jnp.inf)
        acc_ref[...] = jnp.zeros_like(acc_ref)

    for h in range(N_HEADS):
        sl = slice(h * 128, (h + 1) * 128)
        k = k_ref[:, sl]
        v = v_ref[:, sl]
        for r in range(q_ref.shape[0] // strip):
            rows = slice(r * strip, (r + 1) * strip)
            s = _dot_nt(q_ref[rows, sl], k)
            m_prev = m_ref[h, rows]
            m_new = jnp.maximum(m_prev, jnp.max(s, axis=-1, keepdims=True))
            p = jnp.exp2(s - m_new[:, :1])
            acc_ref[h, rows] = jnp.exp2(m_prev - m_new) * acc_ref[h, rows] + _dot(p.astype(BF16), v)
            m_ref[h, rows] = m_new

    @pl.when(ki == pl.num_programs(2) - 1)
    def _():
        outs = []
        for h in range(N_HEADS):
            a = acc_ref[h]
            outs.append(a[:, :HEAD_DIM] / a[:, HEAD_DIM:])
        o_ref[...] = jnp.concatenate(outs, axis=-1)


def _mla(proj, q_norm_w, kv_norm_w, w_uq, w_ukv, q_tabs, k_tabs):
    b, s, _ = proj.shape
    wq, wk, wv, place = _mla_weights(w_uq, w_ukv)
    ts = 512
    row = lambda n: pl.BlockSpec((ts, n), lambda bi, i: (i, 0))
    full = lambda r, n: pl.BlockSpec((r, n), lambda bi, i: (0, 0))
    out = pl.BlockSpec((None, ts, 512), lambda bi, i: (bi, i, 0))
    q, k, v = pl.pallas_call(
        _mla_prep_kernel,
        grid=(b, s // ts),
        in_specs=[
            pl.BlockSpec((None, ts, BLK), lambda bi, i: (bi, i, COL_MCQ)),
            pl.BlockSpec((None, ts, BLK), lambda bi, i: (bi, i, COL_MKV)),
            full(1, MLA_Q_RANK), full(1, MLA_KV_RANK),
            full(MLA_Q_RANK, 512), full(MLA_KV_RANK, 512), full(MLA_KV_RANK, 512), full(128, 512),
            row(128), row(128), row(128), row(128), row(128), row(128),
        ],
        out_specs=[out, out, out],
        out_shape=[jax.ShapeDtypeStruct((b, s, 512), BF16)] * 3,
        compiler_params=_cparams(("parallel", "parallel")),
    )(proj, proj, q_norm_w[None], kv_norm_w[None], wq, wk, wv, place, *q_tabs, *k_tabs)

    tq, tk = math.gcd(s, 2048), math.gcd(s, 512)
    return pl.pallas_call(
        functools.partial(_mla_flash_kernel, strip=tq),
        grid=(b, s // tq, s // tk),
        in_specs=[
            pl.BlockSpec((None, tq, 512), lambda bi, qi, ki: (bi, qi, 0)),
            pl.BlockSpec((None, tk, 512), lambda bi, qi, ki: (bi, ki, 0)),
            pl.BlockSpec((None, tk, 512), lambda bi, qi, ki: (bi, ki, 0)),
        ],
        out_specs=pl.BlockSpec((None, tq, N_HEADS * HEAD_DIM), lambda bi, qi, ki: (bi, qi, 0)),
        out_shape=jax.ShapeDtypeStruct((b, s, N_HEADS * HEAD_DIM), F32),
        scratch_shapes=[pltpu.VMEM((N_HEADS, tq, 128), F32), pltpu.VMEM((N_HEADS, tq, 128), F32)],
        compiler_params=_cparams(("parallel", "parallel", "arbitrary")),
    )(q, k, v)


def _conv_kernel(prev_ref, cur_ref, next_ref, w_ref, b_ref, o_ref, *, ts):
    i = pl.program_id(1)
    w = w_ref[...]

    def conv(x):
        n = x.shape[0]
        acc = x * w[2:3]
        for s in (-2, -1, 1, 2):
            acc = acc + pltpu.roll(x, (-s) % n, 0) * w[s + 2:s + 3]
        return acc

    act = lambda y: _silu(y + b_ref[...])
    cur = cur_ref[...]
    prev = prev_ref[...] * (i > 0).astype(F32)
    nxt = next_ref[...] * (i < pl.num_programs(1) - 1).astype(F32)
    o_ref[...] = act(conv(cur))
    top = conv(jnp.concatenate([prev, cur[:16]], axis=0))
    o_ref[0:8, :] = act(top[8:16])
    bot = conv(jnp.concatenate([cur[ts - 16:], nxt], axis=0))
    o_ref[ts - 8:ts, :] = act(bot[8:16])


def _ssd_conv(proj, conv_w, conv_b):
    b, s, _ = proj.shape
    ts = 512
    width = 3 * BLK
    cb = COL_XBC * BLK // width
    nb8 = ts // 8
    w8 = jnp.concatenate([conv_w, jnp.zeros((8 - SSM_CONV, width), F32)], axis=0)
    return pl.pallas_call(
        functools.partial(_conv_kernel, ts=ts),
        grid=(b, s // ts),
        in_specs=[
            pl.BlockSpec((None, 8, width), lambda bi, i: (bi, jnp.maximum(i * nb8 - 1, 0), cb)),
            pl.BlockSpec((None, ts, width), lambda bi, i: (bi, i, cb)),
            pl.BlockSpec((None, 8, width), lambda bi, i: (bi, jnp.minimum((i + 1) * nb8, s // 8 - 1), cb)),
            pl.BlockSpec((8, width), lambda bi, i: (0, 0)),
            pl.BlockSpec((1, width), lambda bi, i: (0, 0)),
        ],
        out_specs=pl.BlockSpec((None, ts, width), lambda bi, i: (bi, i, 0)),
        out_shape=jax.ShapeDtypeStruct((b, s, width), F32),
        compiler_params=_cparams(("parallel", "parallel")),
    )(proj, proj, proj, w8, conv_b[None])


def _ssd_tables():
    tril = _tri(CHUNK, lambda i, j: j <= i)
    triu = _tri(CHUNK, lambda i, j: j >= i)
    h = jnp.arange(128)[:, None]
    lane = jnp.arange(256)[None, :]
    e_f = ((lane // HEAD_DIM == h) & (h < N_HEADS)).astype(BF16)
    e_b = ((lane // HEAD_DIM == h - N_HEADS) & (h >= N_HEADS) & (h < 2 * N_HEADS)).astype(BF16)
    return tril, triu, e_f, e_b


def _ssd_kernel(xs_ref, bm_ref, cm_ref, z_ref, dt_ref, dtt_ref, bias_ref, a_ref, biasc_ref, ac_ref,
                dskip_ref, nw_ref, tril_ref, triu_ref, ef_ref, eb_ref, o_ref,
                sf_ref, sb_ref, pb_ref, *, nsteps, g):
    t = pl.program_id(1)
    tril, triu = tril_ref[...], triu_ref[...]

    @pl.when(t == 0)
    def _():
        sf_ref[...] = jnp.zeros_like(sf_ref)
        sb_ref[...] = jnp.zeros_like(sb_ref)

    chunks = [slice(ci * CHUNK, (ci + 1) * CHUNK) for ci in range(g)]

    def step_sizes():
        dt = _softplus(dt_ref[:, :128] + bias_ref[...])
        return dt, dt * a_ref[...]

    def cumulative(tri, dta, expand):
        return _dot_x_exact(jnp.concatenate([_dot_exact_x(tri, dta[r]) for r in chunks], axis=0), expand)

    def state_update(s_ref, bmf, weighted_x, total):
        for gi in range(SSM_GROUPS):
            sl = slice(gi * 128, (gi + 1) * 128)
            upd = _dot(bmf[:, sl].T.astype(BF16), weighted_x[:, sl].astype(BF16))
            s_ref[gi] = s_ref[gi] * jnp.exp(total[:, sl]) + upd

    @pl.when(t < nsteps)
    def _():
        base = (nsteps - 1 - t) * g
        dt, dta = step_sizes()
        rcs_all = cumulative(triu, dta, eb_ref[...])
        xdt_all = xs_ref[...] * _dot_x_exact(dt, eb_ref[...])
        for ci in reversed(range(g)):
            rows = chunks[ci]
            rcs_b = rcs_all[rows]
            tot_b = rcs_b[0:1, :]
            pb_ref[base + ci] = sb_ref[...].astype(BF16)
            state_update(sb_ref, bm_ref[rows, :], jnp.exp(tot_b - rcs_b) * xdt_all[rows], tot_b)

    @pl.when(t >= nsteps)
    def _():
        base = (t - nsteps) * g
        row = lax.broadcasted_iota(jnp.int32, (CHUNK, CHUNK), 0)
        colm = lax.broadcasted_iota(jnp.int32, (CHUNK, CHUNK), 1)
        head = _lane_head((CHUNK, 128), HEAD_DIM)
        efb = jnp.concatenate([ef_ref[...], eb_ref[...]], axis=1)
        dt, dta = step_sizes()
        rcs_all = cumulative(triu, dta, eb_ref[...])
        cs_all = cumulative(tril, dta, ef_ref[...])
        dt_exp = _dot_x_exact(dt, efb)
        dtat = jnp.concatenate([dtt_ref[:, r] for r in chunks], axis=0)
        per_chunk = lambda col: jnp.concatenate([col] * g, axis=0)
        dtat = _softplus(dtat + per_chunk(biasc_ref[...])) * per_chunk(ac_ref[...])
        cst_all = _dot_x_exact(dtat, triu)
        rcst_all = _dot_x_exact(dtat, tril)
        for ci in range(g):
            rows = chunks[ci]
            xs = xs_ref[rows, :]
            bmf = bm_ref[rows, :]
            bm = bmf.astype(BF16)
            cm = cm_ref[rows, :].astype(BF16)
            rcs_b = rcs_all[rows]
            cs_f = cs_all[rows]
            tot_f = cs_f[CHUNK - 1:CHUNK, :]
            xdt_f = xs * dt_exp[rows, :SSM_INNER]
            xdt_b = xs * dt_exp[rows, SSM_INNER:]
            cst = cst_all[8 * ci:8 * ci + 8]
            rcst = rcst_all[8 * ci:8 * ci + 8]
            pb = pb_ref[base + ci]
            ys = []
            for gi in range(SSM_GROUPS):
                sl = slice(gi * 128, (gi + 1) * 128)
                cb = _dot_nt(cm[:, sl], bm[:, sl])
                xcat = jnp.concatenate([xdt_f[:, sl], xdt_b[:, sl]], axis=0).astype(BF16)
                wcats = []
                for hh in range(2):
                    h = 2 * gi + hh
                    c0 = h * HEAD_DIM
                    seg_f = cs_f[:, c0:c0 + 1] - cst[h:h + 1, :]
                    seg_b = rcs_b[:, c0:c0 + 1] - rcst[N_HEADS + h:N_HEADS + h + 1, :]
                    dec_f = jnp.where(row >= colm, jnp.exp(jnp.minimum(seg_f, 0.0)), 0.0)
                    dec_b = jnp.where(row < colm, jnp.exp(jnp.minimum(seg_b, 0.0)), 0.0)
                    wcats.append(jnp.concatenate([cb * dec_f, cb * dec_b], axis=1))
                y2 = _dot(jnp.concatenate(wcats, axis=0).astype(BF16), xcat)
                yg = jnp.where(head == 0, y2[:CHUNK], y2[CHUNK:])
                states = jnp.concatenate([sf_ref[gi].astype(BF16), pb[gi]], axis=1)
                off = _dot(cm[:, sl], states)
                ys.append(yg + jnp.exp(cs_f[:, sl]) * off[:, :128] + jnp.exp(rcs_b[:, sl]) * off[:, 128:])
            y = jnp.concatenate(ys, axis=1) + dskip_ref[...] * xs
            y = y * _silu(z_ref[rows, :])
            o_ref[rows, :] = _rms(y, nw_ref[...])
            state_update(sf_ref, bmf, jnp.exp(tot_f - cs_f) * xdt_f, tot_f)


def _ssd(proj, xbc, a_log, dt_bias, d_skip, norm_w, tables):
    b, s, _ = proj.shape
    nc = s // CHUNK
    g = math.gcd(nc, 4)
    nsteps = nc // g
    rows = g * CHUNK
    dt_t = jnp.swapaxes(proj[:, :, COL_DT * BLK:COL_DT * BLK + 8], 1, 2)
    pad_row = lambda v: jnp.concatenate([v.reshape(1, 8), jnp.zeros((1, 120), F32)], axis=1)
    a = -jnp.exp(a_log.astype(F32))
    both = lambda t: jnp.where(t < nsteps, nsteps - 1 - t, t - nsteps)
    late = lambda t: jnp.where(t < nsteps, 0, t - nsteps)
    full = lambda r, n: pl.BlockSpec((r, n), lambda bi, t: (0, 0))
    blk = lambda idx, c: pl.BlockSpec((None, rows, BLK), lambda bi, t: (bi, idx(t), c))
    return pl.pallas_call(
        functools.partial(_ssd_kernel, nsteps=nsteps, g=g),
        grid=(b, 2 * nsteps),
        in_specs=[
            blk(both, 0), blk(both, 1), blk(late, 2), blk(late, COL_Z), blk(both, COL_DT),
            pl.BlockSpec((None, 8, rows), lambda bi, t: (bi, 0, late(t))),
            full(1, 128), full(1, 128), full(8, 1), full(8, 1),
            full(1, SSM_INNER), full(1, SSM_INNER),
            full(CHUNK, CHUNK), full(CHUNK, CHUNK), full(128, 256), full(128, 256),
        ],
        out_specs=blk(late, 0),
        out_shape=jax.ShapeDtypeStruct((b, s, SSM_INNER), F32),
        scratch_shapes=[pltpu.VMEM((SSM_GROUPS, 128, 128), F32), pltpu.VMEM((SSM_GROUPS, 128, 128), F32),
                        pltpu.VMEM((nc, SSM_GROUPS, 128, 128), BF16)],
        compiler_params=_cparams(("parallel", "arbitrary")),
    )(xbc, xbc, xbc, proj, proj, dt_t, pad_row(dt_bias), pad_row(a), dt_bias.reshape(8, 1), a.reshape(8, 1),
      jnp.repeat(d_skip, HEAD_DIM)[None], norm_w[None], *tables)


def _dil_kernel(q_ref, k_ref, v_ref, o_ref, m_ref, l_ref, a_ref, *, s):
    head = _lane_head((128, 128), HEAD_DIM)
    kw = 2 * 128
    for pi, (win, d) in enumerate(DIL_PATTERNS):
        half = win // (2 * d)
        seg = s // d
        per_seg = seg // 128

        def body(i, carry, d=d, pi=pi, half=half, seg=seg, per_seg=per_seg):
            r = i // per_seg
            m0 = (i % per_seg) * 128
            ks = jnp.clip(m0 - half, 0, seg - kw)
            if d == 1:
                qrows = pl.ds(pl.multiple_of(m0, 128), 128)
                krows = pl.ds(pl.multiple_of(ks, 64), kw)
            else:
                qrows = pl.ds(r + d * m0, 128, stride=d)
                krows = pl.ds(r + d * ks, kw, stride=d)
            q = q_ref[qrows, :]
            kb = k_ref[krows, :].astype(BF16)
            vb = v_ref[krows, :].astype(BF16)
            qpos = m0 + lax.broadcasted_iota(jnp.int32, (256, kw), 0) % 128
            kpos = ks + lax.broadcasted_iota(jnp.int32, (256, kw), 1)
            valid = jnp.abs(qpos - kpos) <= half
            q2 = jnp.concatenate([jnp.where(head == hh, q, 0.0) for hh in range(2)], axis=0).astype(BF16)
            sc = jnp.where(valid, _dot_nt(q2, kb), -jnp.inf)
            m2 = jnp.max(sc, axis=-1, keepdims=True)
            p = jnp.exp(sc - m2)
            l2 = jnp.sum(p, axis=-1, keepdims=True)
            a2 = _dot(p.astype(BF16), vb)
            m_ref.at[pi][qrows, :] = jnp.where(head == 0, m2[:128], m2[128:])
            l_ref.at[pi][qrows, :] = jnp.where(head == 0, l2[:128], l2[128:])
            a_ref.at[pi][qrows, :] = jnp.where(head == 0, a2[:128], a2[128:])
            return carry

        lax.fori_loop(0, s // 128, body, 0, unroll=8)

    def merge(i, carry):
        rows = pl.ds(pl.multiple_of(i * 512, 512), 512)
        ms = [m_ref[pi, rows, :] for pi in range(len(DIL_PATTERNS))]
        top = functools.reduce(jnp.maximum, ms)
        ws = [jnp.exp(m - top) for m in ms]
        den = sum(w * l_ref[pi, rows, :] for pi, w in enumerate(ws))
        num = sum(w * a_ref[pi, rows, :] for pi, w in enumerate(ws))
        o_ref[rows, :] = num / den
        return carry

    lax.fori_loop(0, s // 512, merge, 0)


def _dilated(proj):
    b, s, _ = proj.shape
    assert all(s // d >= 256 for _, d in DIL_PATTERNS)
    seqb = lambda c: pl.BlockSpec((None, s, 128), lambda bi, p: (bi, 0, 2 * c + p))
    return pl.pallas_call(
        functools.partial(_dil_kernel, s=s),
        grid=(b, 2),
        in_specs=[seqb(COL_DQ), seqb(COL_DK), seqb(COL_DV)],
        out_specs=seqb(0),
        out_shape=jax.ShapeDtypeStruct((b, s, BLK), F32),
        scratch_shapes=[pltpu.VMEM((len(DIL_PATTERNS), s, 128), F32)] * 3,
        compiler_params=_cparams(("parallel", "parallel")),
    )(proj, proj, proj)


def _outproj_kernel(*refs, n_res):
    ya_ref, yb_ref, yc_ref, yd_ref, w_ref, lnw_ref, rw_ref, xo_ref, h_ref, aff_ref = refs[n_res:]
    acc = refs[0][...]
    for r in refs[1:n_res]:
        acc = acc + r[...]
    mixed = jnp.concatenate([r[...].astype(BF16) for r in (ya_ref, yb_ref, yc_ref, yd_ref)], axis=1)
    acc = acc + _dot(mixed, w_ref[...])
    xo_ref[...] = acc
    h = _rms(acc, lnw_ref[...])
    h_hi = h.astype(BF16)
    h_ref[...] = h_hi
    h_lo = (h - h_hi.astype(F32)).astype(BF16)
    rw = rw_ref[...]
    w_hi = rw.astype(BF16)
    w_lo = (rw - w_hi.astype(F32)).astype(BF16)
    tm = h.shape[0]
    r = _dot(jnp.concatenate([h_hi, h_lo], axis=0), jnp.concatenate([w_hi, w_lo], axis=1))
    logits = r[:tm, :128] + r[:tm, 128:] + r[tm:, :128] + r[tm:, 128:]
    lane = lax.broadcasted_iota(jnp.int32, logits.shape, 1)
    logits = jnp.where(lane < N_EXPERTS, logits, -jnp.inf)
    e = jnp.exp(logits - jnp.max(logits, axis=-1, keepdims=True))
    aff = e / jnp.sum(e, axis=-1, keepdims=True)
    aff_ref[...] = aff.T[:N_EXPERTS, :]


def _outproj(res, ys, w_out, ln2_w, router_w):
    t = res[0].shape[0]
    tm = 256
    rw = jnp.concatenate([router_w, jnp.zeros((D_MODEL, 128 - N_EXPERTS), F32)], axis=1)
    row = lambda n: pl.BlockSpec((tm, n), lambda i: (i, 0))
    full = lambda r, n: pl.BlockSpec((r, n), lambda i: (0, 0))
    return pl.pallas_call(
        functools.partial(_outproj_kernel, n_res=len(res)),
        grid=(t // tm,),
        in_specs=[row(D_MODEL)] * len(res) + [row(BLK)] * 4 + [
            full(D_MODEL, D_MODEL), full(1, D_MODEL), full(D_MODEL, 128)],
        out_specs=[row(D_MODEL), row(D_MODEL), pl.BlockSpec((N_EXPERTS, tm), lambda i: (0, i))],
        out_shape=[jax.ShapeDtypeStruct((t, D_MODEL), F32),
                   jax.ShapeDtypeStruct((t, D_MODEL), BF16),
                   jax.ShapeDtypeStruct((N_EXPERTS, t), F32)],
        compiler_params=_cparams(("parallel",)),
    )(*res, *ys, w_out.astype(BF16), ln2_w[None], rw)


def _exclusive_prefix(x, tri):
    n = x.shape[1] // 128
    off = jnp.zeros((x.shape[0], 1), F32)
    parts = []
    for i in range(n):
        xt = x[:, i * 128:(i + 1) * 128]
        incl = _dot(xt.astype(BF16), tri)
        parts.append(incl - xt + off)
        off = off + incl[:, 127:128]
    return jnp.concatenate(parts, axis=1)


def _select_kernel(aff_ref, tri_ref, key_ref, *, cap):
    a = aff_ref[...]
    n_exp = a.shape[0]
    thr_bits = jnp.zeros((n_exp, 1), jnp.int32)
    for bit in range(30, -1, -1):
        cand = thr_bits | (1 << bit)
        cnt = jnp.sum(jnp.where(a >= lax.bitcast_convert_type(cand, F32), 1.0, 0.0), axis=1, keepdims=True)
        thr_bits = jnp.where(cnt >= cap, cand, thr_bits)
    thr = lax.bitcast_convert_type(thr_bits, F32)
    above = jnp.where(a > thr, 1.0, 0.0)
    tie = jnp.where(a == thr, 1.0, 0.0)
    need = cap - jnp.sum(above, axis=1, keepdims=True)
    tri = tri_ref[...]
    sel = above + tie * jnp.where(_exclusive_prefix(tie, tri) < need, 1.0, 0.0)
    key_ref[...] = jnp.where(sel > 0.0, _exclusive_prefix(sel, tri), -1.0)


def _slot_kernel(key_ref, aff_ref, ltri_ref, idx_ref, gate_ref, *, cap, nt):
    n_exp = key_ref.shape[0]
    pad = jnp.zeros((128 - nt, 128), F32)
    slot = lax.broadcasted_iota(jnp.int32, (128, cap), 1).astype(F32)
    sub = lax.broadcasted_iota(jnp.int32, (128, cap), 0).astype(F32)
    ltri = ltri_ref[...]

    def per_expert(e, carry):
        key = jnp.concatenate([key_ref[e], pad - 1.0], axis=0)
        aff = jnp.concatenate([aff_ref[e], pad], axis=0)
        cnt = jnp.sum(jnp.where(key >= 0.0, 1.0, 0.0), axis=1, keepdims=True)
        cum = _dot(ltri, jnp.broadcast_to(cnt, (128, 128)).astype(BF16))
        in_tile = (jnp.where(slot >= (cum - cnt)[:, :1], 1.0, 0.0)
                   * jnp.where(slot < cum[:, :1], 1.0, 0.0))
        g = in_tile.astype(BF16)
        k_hi, k_lo = _split2(key.T)
        slot_of = _dot(k_hi, g) + _dot(k_lo, g)
        hit = jnp.where(slot_of == slot, 1.0, 0.0)
        aff_t = aff.T
        a_hi = aff_t.astype(BF16)
        r1 = aff_t - a_hi.astype(F32)
        a_mid = r1.astype(BF16)
        a_lo = (r1 - a_mid.astype(F32)).astype(BF16)
        aff_of = _dot(a_hi, g) + _dot(a_mid, g) + _dot(a_lo, g)
        tile = jnp.sum(in_tile * sub, axis=0, keepdims=True)
        within = jnp.sum(hit * sub, axis=0, keepdims=True)
        idx_ref[pl.ds(e, 1), :] = (128.0 * tile + within).astype(jnp.int32)
        gate_ref[pl.ds(e, 1), :] = jnp.sum(hit * aff_of, axis=0, keepdims=True)
        return carry

    lax.fori_loop(0, n_exp, per_expert, 0, unroll=4)


def _moe_select(aff_t, b, s, cap):
    nt = s // 128
    assert nt <= 128 and nt % 8 == 0 and cap % 128 == 0
    key = pl.pallas_call(
        functools.partial(_select_kernel, cap=cap),
        grid=(b,),
        in_specs=[pl.BlockSpec((N_EXPERTS, s), lambda bi: (0, bi)),
                  pl.BlockSpec((128, 128), lambda bi: (0, 0))],
        out_specs=pl.BlockSpec((None, N_EXPERTS, s), lambda bi: (bi, 0, 0)),
        out_shape=jax.ShapeDtypeStruct((b, N_EXPERTS, s), F32),
        compiler_params=_cparams(("parallel",)),
    )(aff_t, _tri(128, lambda i, j: i <= j))
    out = pl.BlockSpec((None, N_EXPERTS, cap), lambda bi: (bi, 0, 0))
    return pl.pallas_call(
        functools.partial(_slot_kernel, cap=cap, nt=nt),
        grid=(b,),
        in_specs=[pl.BlockSpec((None, N_EXPERTS, nt, 128), lambda bi: (bi, 0, 0, 0)),
                  pl.BlockSpec((N_EXPERTS, nt, 128), lambda bi: (0, bi, 0)),
                  pl.BlockSpec((128, 128), lambda bi: (0, 0))],
        out_specs=[out, out],
        out_shape=[jax.ShapeDtypeStruct((b, N_EXPERTS, cap), jnp.int32),
                   jax.ShapeDtypeStruct((b, N_EXPERTS, cap), F32)],
        compiler_params=_cparams(("parallel",)),
    )(key.reshape(b, N_EXPERTS, nt, 128), aff_t.reshape(N_EXPERTS, b * nt, 128), _tri(128, lambda i, j: j <= i))


ROWS_PER_ITER = 8


def _gather_kernel(idx_ref, h_ref, o_ref, hf_ref, buf_ref, *, cap, n_exp):
    e = pl.program_id(1)
    base = (pl.program_id(0) * n_exp + e) * cap

    @pl.when(e == 0)
    def _():
        hf_ref[...] = h_ref[...].astype(F32)

    def body(jb, carry):
        j0 = pl.multiple_of(jb * ROWS_PER_ITER, ROWS_PER_ITER)
        vals = [hf_ref[pl.ds(idx_ref[base + j0 + u], 1), :] for u in range(ROWS_PER_ITER)]
        for u in range(ROWS_PER_ITER):
            buf_ref[pl.ds(j0 + u, 1), :] = vals[u]
        return carry

    lax.fori_loop(0, cap // ROWS_PER_ITER, body, 0)
    o_ref[...] = buf_ref[...].astype(BF16)


def _moe_gather(idx_flat, h, cap):
    b, s, d = h.shape
    return pl.pallas_call(
        functools.partial(_gather_kernel, cap=cap, n_exp=N_EXPERTS),
        grid_spec=pltpu.PrefetchScalarGridSpec(
            num_scalar_prefetch=1,
            grid=(b, N_EXPERTS),
            in_specs=[pl.BlockSpec((None, s, d), lambda bi, e, idx: (bi, 0, 0))],
            out_specs=pl.BlockSpec((None, None, cap, d), lambda bi, e, idx: (e, bi, 0, 0)),
            scratch_shapes=[pltpu.VMEM((s, d), F32), pltpu.VMEM((cap, d), F32)],
        ),
        out_shape=jax.ShapeDtypeStruct((N_EXPERTS, b, cap, d), BF16),
        compiler_params=_cparams(("parallel", "arbitrary")),
    )(idx_flat, h)


def _ffn_kernel(x_ref, wg_ref, wu_ref, wd_ref, o_ref, *, rows, tr):
    f = pl.program_id(1)
    wg = wg_ref[...].astype(BF16)
    wu = wu_ref[...].astype(BF16)
    wd = wd_ref[...].astype(BF16)

    def partial_out(sl):
        x = x_ref[sl, :]
        hid = (_silu(_dot(x, wg)) * _dot(x, wu)).astype(BF16)
        return _dot(hid, wd)

    for r in range(rows // tr):
        sl = slice(r * tr, (r + 1) * tr)

        @pl.when(f == 0)
        def _():
            o_ref[sl, :] = partial_out(sl)

        @pl.when(f > 0)
        def _():
            o_ref[sl, :] += partial_out(sl)


def _moe_ffn(xin, w_gate, w_up, w_down, layer):
    n_exp, rows, d = xin.shape
    tf = 512
    return pl.pallas_call(
        functools.partial(_ffn_kernel, rows=rows, tr=math.gcd(rows, 1024)),
        grid=(n_exp, EXPERT_FF // tf),
        in_specs=[
            pl.BlockSpec((None, rows, d), lambda e, f: (e, 0, 0)),
            pl.BlockSpec((None, None, d, tf), lambda e, f: (layer, e, 0, f)),
            pl.BlockSpec((None, None, d, tf), lambda e, f: (layer, e, 0, f)),
            pl.BlockSpec((None, None, tf, d), lambda e, f: (layer, e, f, 0)),
        ],
        out_specs=pl.BlockSpec((None, rows, d), lambda e, f: (e, 0, 0)),
        out_shape=jax.ShapeDtypeStruct((n_exp, rows, d), F32),
        compiler_params=_cparams(("parallel", "arbitrary")),
    )(xin, w_gate, w_up, w_down)


def _combine_kernel(idx_ref, gate_ref, y_ref, o_ref, ys_ref, *, cap, n_exp):
    e = pl.program_id(1)
    base = (pl.program_id(0) * n_exp + e) * cap

    @pl.when(e == 0)
    def _():
        o_ref[...] = jnp.zeros_like(o_ref)

    ys_ref[...] = gate_ref[...] * y_ref[...]

    def body(jb, carry):
        j0 = pl.multiple_of(jb * ROWS_PER_ITER, ROWS_PER_ITER)
        rows = [idx_ref[base + j0 + u] for u in range(ROWS_PER_ITER)]
        vals = [o_ref[pl.ds(rows[u], 1), :] + ys_ref[pl.ds(j0 + u, 1), :]
                for u in range(ROWS_PER_ITER)]
        for u in range(ROWS_PER_ITER):
            o_ref[pl.ds(rows[u], 1), :] = vals[u]
        return carry

    lax.fori_loop(0, cap // ROWS_PER_ITER, body, 0)


def _moe_combine(idx_flat, gate_flat, y, b, s, cap):
    d = y.shape[-1]
    return pl.pallas_call(
        functools.partial(_combine_kernel, cap=cap, n_exp=N_EXPERTS),
        grid_spec=pltpu.PrefetchScalarGridSpec(
            num_scalar_prefetch=1,
            grid=(b, N_EXPERTS),
            in_specs=[pl.BlockSpec((cap, 1), lambda bi, e, idx: (bi * N_EXPERTS + e, 0)),
                      pl.BlockSpec((None, None, cap, d), lambda bi, e, idx: (e, bi, 0, 0))],
            out_specs=pl.BlockSpec((None, s, d), lambda bi, e, idx: (bi, 0, 0)),
            scratch_shapes=[pltpu.VMEM((cap, d), F32)],
        ),
        out_shape=jax.ShapeDtypeStruct((b, s, d), F32),
        compiler_params=_cparams(("parallel", "arbitrary")),
    )(idx_flat, gate_flat.reshape(-1, 1), y)


def _final_norm_kernel(xa_ref, xb_ref, w_ref, o_ref):
    o_ref[...] = _rms(xa_ref[...] + xb_ref[...], w_ref[...])


def _final_norm(xa, xb, w):
    t = xa.shape[0]
    tm = 1024
    row = pl.BlockSpec((tm, D_MODEL), lambda i: (i, 0))
    return pl.pallas_call(
        _final_norm_kernel,
        grid=(t // tm,),
        in_specs=[row, row, pl.BlockSpec((1, D_MODEL), lambda i: (0, 0))],
        out_specs=row,
        out_shape=jax.ShapeDtypeStruct((t, D_MODEL), F32),
        compiler_params=_cparams(("parallel",)),
    )(xa, xb, w[None])


def _moe(h, aff, w_gate, w_up, w_down, layer, b, s):
    d = h.shape[-1]
    cap = EC_CAPACITY_FACTOR * s // N_EXPERTS
    token_idx, gate = _moe_select(aff, b, s, cap)
    idx_flat = token_idx.reshape(-1)
    xin = _moe_gather(idx_flat, h.reshape(b, s, d), cap)
    y = _moe_ffn(xin.reshape(N_EXPERTS, b * cap, d), w_gate, w_up, w_down, layer)
    return _moe_combine(idx_flat, gate.reshape(-1), y.reshape(N_EXPERTS, b, cap, d), b, s, cap)


def kernel(x, ln1_w, w_in, mla_q_norm_w, mla_kv_norm_w, mla_w_uq, mla_w_ukv, ssm_conv_w, ssm_conv_b,
           ssm_a_log, ssm_dt_bias, ssm_d, ssm_norm_w, w_out, ln2_w, router_w, exp_w_gate, exp_w_up,
           exp_w_down, final_norm_w):
    b, s, d = x.shape
    depth = w_in.shape[0]
    ret_tabs = _ret_tables()
    ret_rope = _rope_tables(s, HEAD_DIM, RET_THETA, HEAD_DIM, 0, 2)
    mla_q_rope = _rope_tables(s, MLA_ROPE, ROPE_THETA, 128, MLA_NOPE, 1)
    mla_k_rope = _rope_tables(s, MLA_ROPE, ROPE_THETA, 128, 0, 1)
    dil_rope = _rope_tables(s, ROPE_DIM, ROPE_THETA, HEAD_DIM, 0, 2)
    ssd_tabs = _ssd_tables()
    res = [x.reshape(b * s, d)]
    for i in range(depth):
        proj = _inproj(res, ln1_w[i][None], _pad_cols(w_in[i]).astype(BF16), ret_rope, dil_rope, s)
        proj = proj.reshape(b, s, PROJ_PAD)
        y_a = _retention(proj, ret_tabs)
        y_b = _mla(proj, mla_q_norm_w[i], mla_kv_norm_w[i], mla_w_uq[i], mla_w_ukv[i], mla_q_rope, mla_k_rope)
        xbc = _ssd_conv(proj, ssm_conv_w[i], ssm_conv_b[i])
        y_c = _ssd(proj, xbc, ssm_a_log[i], ssm_dt_bias[i], ssm_d[i], ssm_norm_w[i], ssd_tabs)
        y_d = _dilated(proj)
        ys = [y.reshape(b * s, BLK) for y in (y_a, y_b, y_c, y_d)]
        x_mid, h, aff = _outproj(res, ys, w_out[i], ln2_w[i], router_w[i])
        moe = _moe(h, aff, exp_w_gate, exp_w_up, exp_w_down, i, b, s)
        res = [x_mid, moe.reshape(b * s, d)]
    return _final_norm(res[0], res[1], final_norm_w).reshape(b, s, d)
```
